```python
import jax, jax.numpy as jnp
from jax import lax
import numpy as np

D_MODEL = 1024
BATCH = 4
SEQ = 8192
DEPTH = 2

CTX_LEN = 256
GRID_W = 64

HGRN_HEADS = 8
HGRN_DK = 128
HGRN_DV = 128
HGRN_WIDTH = HGRN_HEADS * HGRN_DK
HGRN_CHUNK = 64

LRU_WIDTH = 1024
LRU_BLOCKS = 8
LRU_BLOCK = LRU_WIDTH // LRU_BLOCKS
LRU_C = 8.0
CONV_W = 4
CONV_PAD = (2, 1)

N_EXPERTS = 16
EC_FACTOR = 2
EXPERT_FF = 1024

IN_COLS = 5 * HGRN_WIDTH + 2 * LRU_WIDTH + 2 * D_MODEL
DEEPNORM_ALPHA = (2.0 * DEPTH) ** 0.25
DEEPNORM_BETA = (8.0 * DEPTH) ** -0.25
LN_EPS = 1e-5
RMS_EPS = 1e-6

kernel_name = "hybrid_hgrn2_rglru_ecmoe_diffusion"


def layer_norm(x, g, b):
    xf = x.astype(jnp.float32)
    mu = jnp.mean(xf, axis=-1, keepdims=True)
    var = jnp.mean(jnp.square(xf - mu), axis=-1, keepdims=True)
    return ((xf - mu) * lax.rsqrt(var + LN_EPS) * g + b).astype(x.dtype)


def rms_norm(x, g):
    xf = x.astype(jnp.float32)
    return xf * lax.rsqrt(jnp.mean(jnp.square(xf), axis=-1, keepdims=True) + RMS_EPS) * g


def modulate(x, shift, scale):
    return x * (1.0 + scale) + shift


def flip(a):
    return jnp.flip(a, axis=1)


def heads(a):
    return a.reshape(a.shape[0], a.shape[1], HGRN_HEADS, -1)


def split_columns(p):
    widths = (HGRN_WIDTH,) * 5 + (LRU_WIDTH,) * 2 + (D_MODEL,) * 2
    return jnp.split(p, np.cumsum(widths)[:-1].tolist(), axis=-1)


def raster_to_colmajor(x):
    bsz, n, w = x.shape
    rows = n // GRID_W
    return x.reshape(bsz, rows, GRID_W, w).transpose(0, 2, 1, 3).reshape(bsz, n, w)


def colmajor_to_raster(x):
    bsz, n, w = x.shape
    rows = n // GRID_W
    return x.reshape(bsz, GRID_W, rows, w).transpose(0, 2, 1, 3).reshape(bsz, n, w)


def hgrn2_gates(f_pre, lb):
    f_pre = f_pre.astype(jnp.float32)
    log_f = jnp.logaddexp(jnp.log(lb), jnp.log1p(-lb) + jax.nn.log_sigmoid(f_pre))
    k = (1.0 - lb) * jax.nn.sigmoid(-f_pre)
    return log_f, k


def gla_chunk_scan(q, k, v, log_f, s0):
    bsz, t_len, h, _ = q.shape
    dv = v.shape[-1]
    nc = t_len // HGRN_CHUNK

    def to_chunks(a):
        return a.astype(jnp.float32).reshape(bsz, nc, HGRN_CHUNK, h, a.shape[-1]).transpose(1, 0, 3, 2, 4)

    lower_tri = jnp.tril(jnp.ones((HGRN_CHUNK, HGRN_CHUNK), bool))[:, :, None]

    def step(s, inp):
        qc, kc, vc, lf = inp
        b = jnp.cumsum(lf, axis=2)
        o_inter = jnp.einsum('bhtk,bhkv->bhtv', qc * jnp.exp(b), s)
        rel = jnp.where(lower_tri, b[:, :, :, None, :] - b[:, :, None, :, :], -jnp.inf)
        att = jnp.einsum('bhtk,bhtsk,bhsk->bhts', qc, jnp.exp(rel), kc)
        o = o_inter + jnp.einsum('bhts,bhsv->bhtv', att, vc)
        b_end = b[:, :, -1:, :]
        s_new = jnp.exp(b_end[:, :, 0, :, None]) * s + jnp.einsum('bhsk,bhsv->bhkv', kc * jnp.exp(b_end - b), vc)
        return s_new, o

    s_fin, o = lax.scan(step, s0.astype(jnp.float32), (to_chunks(q), to_chunks(k), to_chunks(v), to_chunks(log_f)))
    o = o.transpose(1, 0, 3, 2, 4).reshape(bsz, t_len, h, dv)
    return o, s_fin


def hgrn2_direction(q_c, v_c, fpre_c, q_l, v_l, fpre_l, lb, reverse):
    if reverse:
        q_c, v_c, fpre_c, q_l, v_l, fpre_l = (flip(a) for a in (q_c, v_c, fpre_c, q_l, v_l, fpre_l))
    s0 = jnp.zeros((q_c.shape[0], HGRN_HEADS, HGRN_DK, HGRN_DV), jnp.float32)
    lf, k = hgrn2_gates(fpre_c, lb)
    o_c, s_c = gla_chunk_scan(q_c, heads(k), v_c, heads(lf), s0)
    lf, k = hgrn2_gates(fpre_l, lb)
    o_l, _ = gla_chunk_scan(q_l, heads(k), v_l, heads(lf), s_c)
    if reverse:
        o_c, o_l = flip(o_c), flip(o_l)
    return o_c, o_l


def centred_dwconv(x, w, b):
    t_len = x.shape[1]
    xp = jnp.pad(x, ((0, 0), CONV_PAD, (0, 0)))
    return sum(xp[:, j:j + t_len] * w[j] for j in range(CONV_W)) + b


def linear_scan(a, u, h0):
    u = u.at[:, 0].add(a[:, 0] * h0)

    def combine(left, right):
        return left[0] * right[0], right[0] * left[1] + right[1]

    _, h = lax.associative_scan(combine, (a, u), axis=1)
    return h, h[:, -1]


def rg_lru(xc, w_a, b_a, w_x, b_x, lam, h0):
    bsz, t_len, w = xc.shape
    xf = xc.astype(jnp.float32)
    xb = xf.reshape(bsz, t_len, LRU_BLOCKS, LRU_BLOCK)
    r = jax.nn.sigmoid(jnp.einsum('btni,nij->btnj', xb, w_a).reshape(bsz, t_len, w) + b_a)
    i = jax.nn.sigmoid(jnp.einsum('btni,nij->btnj', xb, w_x).reshape(bsz, t_len, w) + b_x)
    log_a = -LRU_C * r * jax.nn.softplus(-lam)
    a = jnp.exp(log_a)
    u = jnp.sqrt(-jnp.expm1(2.0 * log_a)) * (i * xf)
    return linear_scan(a, u, h0)


def rglru_direction(x_c, x_l, w_a, b_a, w_x, b_x, lam, reverse):
    if reverse:
        x_c, x_l = flip(x_c), flip(x_l)
    h0 = jnp.zeros((x_c.shape[0], LRU_WIDTH), jnp.float32)
    h_c, hc_last = rg_lru(x_c, w_a, b_a, w_x, b_x, lam, h0)
    h_l, _ = rg_lru(x_l, w_a, b_a, w_x, b_x, lam, hc_last)
    if reverse:
        h_c, h_l = flip(h_c), flip(h_l)
    return h_c, h_l


def token_mixer(h_lat, h_ctx, w_in, lb, g_norm, conv_w, conv_b, lru_wa, lru_ba, lru_wx, lru_bx, lru_lam,
                w_branch_a, w_branch_b, w_out, need_ctx):
    q_l, v_l, ff_l, fb_l, og_l, lx_l, ly_l, ma_l, mb_l = split_columns(h_lat @ w_in)
    q_c, v_c, ff_c, fb_c, og_c, lx_c, ly_c, ma_c, mb_c = split_columns(h_ctx @ w_in)
    qh_l = heads(jax.nn.silu(q_l)) * HGRN_DK ** -0.5
    qh_c = heads(jax.nn.silu(q_c)) * HGRN_DK ** -0.5
    vh_l, vh_c = heads(v_l), heads(v_c)
    oc_f, ol_f = hgrn2_direction(qh_c, vh_c, ff_c, qh_l, vh_l, ff_l, lb[0], reverse=False)
    oc_b, ol_b = hgrn2_direction(qh_c, vh_c, fb_c, qh_l, vh_l, fb_l, lb[1], reverse=True)
    xc_l = centred_dwconv(raster_to_colmajor(lx_l), conv_w, conv_b)
    xc_c = centred_dwconv(lx_c, conv_w, conv_b)
    hc_f, hl_f = rglru_direction(xc_c, xc_l, lru_wa[0], lru_ba[0], lru_wx[0], lru_bx[0], lru_lam[0], reverse=False)
    hc_b, hl_b = rglru_direction(xc_c, xc_l, lru_wa[1], lru_ba[1], lru_wx[1], lru_bx[1], lru_lam[1], reverse=True)

    def merge(o_hgrn, og, h_lru, ly, ma, mb):
        bsz, t_len = og.shape[:2]
        o_a = rms_norm(o_hgrn, g_norm).reshape(bsz, t_len, HGRN_WIDTH) * jax.nn.silu(og.astype(jnp.float32))
        y_a = o_a.astype(og.dtype) @ w_branch_a
        y_b = (h_lru.astype(ly.dtype) * jax.nn.gelu(ly)) @ w_branch_b
        return (jax.nn.sigmoid(ma) * y_a + jax.nn.sigmoid(mb) * y_b) @ w_out

    y_lat = merge(ol_f + ol_b, og_l, colmajor_to_raster(hl_f + hl_b), ly_l, ma_l, mb_l)
    y_ctx = merge(oc_f + oc_b, og_c, hc_f + hc_b, ly_c, ma_c, mb_c) if need_ctx else None
    return y_lat, y_ctx


def expert_choice_ffn(h, w_router, w_gate, w_up, w_down):
    bsz, t_len, _ = h.shape
    cap = EC_FACTOR * t_len // N_EXPERTS
    affinity = jax.nn.softmax((h @ w_router).astype(jnp.float32), axis=-1)
    gate, idx = lax.top_k(affinity.transpose(0, 2, 1), cap)
    bidx = jnp.arange(bsz)[:, None, None]
    xs = h[bidx, idx]
    hid = jax.nn.silu(jnp.einsum('becd,edf->becf', xs, w_gate)) * jnp.einsum('becd,edf->becf', xs, w_up)
    out = jnp.einsum('becf,efd->becd', hid, w_down) * gate[..., None].astype(h.dtype)
    return jnp.zeros_like(h).at[bidx, idx].add(out)


def setup_inputs(seed: int = 0) -> dict:
    key = jax.random.key(seed)
    ks = jax.random.split(key, 26)
    nrm = jax.random.normal
    d, hw, lw, e, f = D_MODEL, HGRN_WIDTH, LRU_WIDTH, N_EXPERTS, EXPERT_FF
    u = jax.random.uniform(ks[15], (DEPTH, 2, lw), jnp.float32, 0.9, 0.999)
    a0 = u ** (1.0 / LRU_C)
    return {
        "x": nrm(ks[0], (BATCH, SEQ, d), jnp.float32),
        "c": nrm(ks[1], (BATCH, d), jnp.float32),
        "ctx": nrm(ks[2], (BATCH, CTX_LEN, d), jnp.float32),
        "c_ctx": nrm(ks[3], (d,), jnp.float32),
        "w_mod": nrm(ks[4], (DEPTH, d, 6 * d), jnp.float32) * (0.2 * d ** -0.5),
        "b_mod": nrm(ks[5], (DEPTH, 6 * d), jnp.float32) * 0.02,
        "w_in": nrm(ks[6], (DEPTH, d, IN_COLS), jnp.float32) * d ** -0.5,
        "hgrn_lb_logits": nrm(ks[7], (DEPTH, 2, hw), jnp.float32),
        "hgrn_norm_g": 1.0 + 0.02 * nrm(ks[8], (DEPTH, HGRN_DV), jnp.float32),
        "conv_w": nrm(ks[9], (DEPTH, CONV_W, lw), jnp.float32) * CONV_W ** -0.5,
        "conv_b": nrm(ks[10], (DEPTH, lw), jnp.float32) * 0.02,
        "lru_wa": nrm(ks[11], (DEPTH, 2, LRU_BLOCKS, LRU_BLOCK, LRU_BLOCK), jnp.float32) * LRU_BLOCK ** -0.5,
        "lru_ba": nrm(ks[12], (DEPTH, 2, lw), jnp.float32) * 0.02,
        "lru_wx": nrm(ks[13], (DEPTH, 2, LRU_BLOCKS, LRU_BLOCK, LRU_BLOCK), jnp.float32) * LRU_BLOCK ** -0.5,
        "lru_bx": nrm(ks[14], (DEPTH, 2, lw), jnp.float32) * 0.02,
        "lru_lambda": jnp.log(a0) - jnp.log1p(-a0),
        "w_branch_a": nrm(ks[16], (DEPTH, hw, d), jnp.float32) * (DEEPNORM_BETA * hw ** -0.5),
        "w_branch_b": nrm(ks[17], (DEPTH, lw, d), jnp.float32) * (DEEPNORM_BETA * lw ** -0.5),
        "w_out": nrm(ks[18], (DEPTH, d, d), jnp.float32) * (DEEPNORM_BETA * d ** -0.5),
        "ln_g": 1.0 + 0.02 * nrm(ks[19], (DEPTH, 2, d), jnp.float32),
        "ln_b": 0.02 * nrm(ks[20], (DEPTH, 2, d), jnp.float32),
        "w_router": nrm(ks[21], (DEPTH, d, e), jnp.float32) * d ** -0.5,
        "w_gate": nrm(ks[22], (DEPTH, e, d, f), jnp.float32) * d ** -0.5,
        "w_up": nrm(ks[23], (DEPTH, e, d, f), jnp.float32) * (DEEPNORM_BETA * d ** -0.5),
        "w_down": nrm(ks[24], (DEPTH, e, f, d), jnp.float32) * (DEEPNORM_BETA * f ** -0.5),
    }


def reference(x, c, ctx, c_ctx, w_mod, b_mod, w_in, hgrn_lb_logits, hgrn_norm_g, conv_w, conv_b,
              lru_wa, lru_ba, lru_wx, lru_bx, lru_lambda, w_branch_a, w_branch_b, w_out, ln_g, ln_b,
              w_router, w_gate, w_up, w_down):
    lb_cum = jnp.cumsum(jax.nn.softmax(hgrn_lb_logits.astype(jnp.float32), axis=0), axis=0)
    lower_bounds = lb_cum - lb_cum[0]
    for l in range(DEPTH):
        need_ctx = l < DEPTH - 1
        mod = jax.nn.silu(c) @ w_mod[l] + b_mod[l]
        mod_c = jax.nn.silu(c_ctx) @ w_mod[l] + b_mod[l]
        sh1, sc1, g1, sh2, sc2, g2 = jnp.split(mod[:, None, :], 6, axis=-1)
        csh1, csc1, cg1, csh2, csc2, cg2 = jnp.split(mod_c, 6)
        y_lat, y_ctx = token_mixer(
            modulate(x, sh1, sc1), modulate(ctx, csh1, csc1), w_in[l], lower_bounds[l], hgrn_norm_g[l],
            conv_w[l], conv_b[l], lru_wa[l], lru_ba[l], lru_wx[l], lru_bx[l], lru_lambda[l],
            w_branch_a[l], w_branch_b[l], w_out[l], need_ctx)
        x = layer_norm(DEEPNORM_ALPHA * x + g1 * y_lat, ln_g[l, 0], ln_b[l, 0])
        ffn_lat = expert_choice_ffn(modulate(x, sh2, sc2), w_router[l], w_gate[l], w_up[l], w_down[l])
        x = layer_norm(DEEPNORM_ALPHA * x + g2 * ffn_lat, ln_g[l, 1], ln_b[l, 1])
        if need_ctx:
            ctx = layer_norm(DEEPNORM_ALPHA * ctx + cg1 * y_ctx, ln_g[l, 0], ln_b[l, 0])
            ffn_ctx = expert_choice_ffn(modulate(ctx, csh2, csc2), w_router[l], w_gate[l], w_up[l], w_down[l])
            ctx = layer_norm(DEEPNORM_ALPHA * ctx + cg2 * ffn_ctx, ln_g[l, 1], ln_b[l, 1])
    return x
```

```python
import functools

import jax
import jax.numpy as jnp
from jax import lax
from jax.experimental import pallas as pl
from jax.experimental.pallas import tpu as pltpu

F32 = jnp.float32
BF16 = jnp.bfloat16
I32 = jnp.int32
HIGHEST = lax.Precision.HIGHEST

DEPTH = 2
D_MODEL = 1024
GRID_W = 64
N_HEADS = 8
HEAD_W = 128
CHUNK = 64
N_EXPERTS = 16
EC_FACTOR = 2
LRU_C = 8.0
ALPHA = (2.0 * DEPTH) ** 0.25
LN_EPS = 1e-5
RMS_EPS = 1e-6
EXP_CLAMP = 80.0

V7X_VMEM_LIMIT_BYTES = 56 * 1024 * 1024
OFF_STRIDE = 128

G_Q, G_V, G_FF, G_FB, G_OG, G_LX, G_LY, G_MA, G_MB = range(9)


def _params(*sem):
    return pltpu.CompilerParams(dimension_semantics=sem, vmem_limit_bytes=V7X_VMEM_LIMIT_BYTES)


def _sigmoid(x):
    return 1.0 / (1.0 + jnp.exp(-x))


def _layer_norm(x, g, b):
    mu = jnp.mean(x, axis=-1, keepdims=True)
    xc = x - mu
    var = jnp.mean(xc * xc, axis=-1, keepdims=True)
    return xc * lax.rsqrt(var + LN_EPS) * g + b


def _mod_kernel(c_ref, w_ref, b_ref, o_ref):
    c = c_ref[...]
    s = c * _sigmoid(c)
    o_ref[0] = jnp.dot(s, w_ref[0], preferred_element_type=F32, precision=HIGHEST) + b_ref[0]


def _mod_call(c_all, w_mod, b_mod):
    depth, d, n = w_mod.shape
    tn = 1536
    return pl.pallas_call(
        _mod_kernel,
        grid=(depth, n // tn),
        in_specs=[
            pl.BlockSpec((8, d), lambda l, j: (0, 0)),
            pl.BlockSpec((1, d, tn), lambda l, j: (l, 0, j)),
            pl.BlockSpec((1, 1, tn), lambda l, j: (l, 0, j)),
        ],
        out_specs=pl.BlockSpec((1, 8, tn), lambda l, j: (l, 0, j)),
        out_shape=jax.ShapeDtypeStruct((depth, 8, n), F32),
        compiler_params=_params("arbitrary", "arbitrary"),
        name="mod",
    )(c_all, w_mod, b_mod.reshape(depth, 1, n))


def _apply_act(act, p):
    if act == "id":
        return p
    if act == "silu_scale":
        return p * _sigmoid(p) * (HEAD_W ** -0.5)
    if act == "silu":
        return p * _sigmoid(p)
    if act == "sigmoid":
        return _sigmoid(p)
    if act == "gelu":
        return 0.5 * p * (1.0 + jnp.tanh(0.7978845608028654 * (p + 0.044715 * (p * p * p))))
    raise ValueError(act)


def _proj_kernel(x_ref, mod_ref, w_ref, o_ref, wbf_ref, *, acts, ctx_row):
    j = pl.program_id(0)
    b = pl.program_id(1)
    i = pl.program_id(2)
    d = D_MODEL

    @pl.when((b == 0) & (i == 0))
    def _():
        wbf_ref[...] = w_ref[...].astype(BF16)

    row = b if ctx_row is None else ctx_row
    sh = mod_ref[pl.ds(row, 1), 0:d]
    sc = mod_ref[pl.ds(row, 1), d:2 * d]
    h = (x_ref[0] * (1.0 + sc) + sh).astype(BF16)
    p = jnp.dot(h, wbf_ref[...], preferred_element_type=F32)
    for k, act in enumerate(acts):
        @pl.when(j == k)
        def _(act=act):
            o_ref[0, 0] = _apply_act(act, p).astype(o_ref.dtype)


def _proj_call(x, mod_l, w_in_l, groups, acts, out_dtype, ctx_row, name):
    bsz, t, d = x.shape
    tm = min(t, 1024)
    ng = len(groups)
    first = groups[0]
    if groups == tuple(range(first, first + ng)):
        col = lambda j: j + first
    else:
        assert groups[:2] == (G_Q, G_V) and groups[2:] == tuple(range(groups[2], groups[2] + ng - 2))
        skip = groups[2] - 2
        col = lambda j: jnp.where(j < 2, j, j + skip)
    return pl.pallas_call(
        functools.partial(_proj_kernel, acts=acts, ctx_row=ctx_row),
        grid=(ng, bsz, t // tm),
        in_specs=[
            pl.BlockSpec((1, tm, d), lambda j, b, i: (b, i, 0)),
            pl.BlockSpec((8, 6 * d), lambda j, b, i: (0, 0)),
            pl.BlockSpec((d, d), lambda j, b, i: (0, col(j))),
        ],
        out_specs=pl.BlockSpec((1, 1, tm, d), lambda j, b, i: (j, b, i, 0)),
        out_shape=jax.ShapeDtypeStruct((ng, bsz, t, d), out_dtype),
        scratch_shapes=[pltpu.VMEM((d, d), BF16)],
        compiler_params=_params("arbitrary", "arbitrary", "arbitrary"),
        name=name,
    )(x, mod_l, w_in_l)


def _hgrn_chunk(q, v, fp, lb, la, l1, st_ref, reverse):
    e = jnp.exp(-jnp.abs(fp))
    r = 1.0 / (1.0 + e)
    log_sig = jnp.minimum(fp, 0.0) - jnp.log(1.0 + e)
    k = (1.0 - lb) * jnp.where(fp >= 0.0, e * r, r)
    b2 = l1 + log_sig
    m = jnp.maximum(la, b2)
    log_f = m + jnp.log(jnp.exp(la - m) + jnp.exp(b2 - m))

    ri = lax.broadcasted_iota(I32, (CHUNK, CHUNK), 0)
    ci = lax.broadcasted_iota(I32, (CHUNK, CHUNK), 1)
    causal = (ci >= ri) if reverse else (ci <= ri)
    bcum = jnp.dot(causal.astype(F32), log_f, preferred_element_type=F32, precision=HIGHEST)
    mid = CHUNK // 2
    b_ref = bcum[mid:mid + 1, :]
    b_end = bcum[0:1, :] if reverse else bcum[CHUNK - 1:CHUNK, :]
    e1 = jnp.exp(jnp.minimum(bcum - b_ref, EXP_CLAMP))
    e2 = jnp.exp(jnp.minimum(b_ref - bcum, EXP_CLAMP))
    qt = q.astype(F32) * e1
    kt = k * e2
    att = lax.dot_general(qt.astype(BF16), kt.astype(BF16), (((1,), (1,)), ((), ())),
                          preferred_element_type=F32)
    att = jnp.where(causal, att, 0.0).astype(BF16)
    st = st_ref[...]
    qs = (qt * jnp.exp(b_ref)).astype(BF16)
    o = lax.dot_general(qs, st.astype(BF16), (((1,), (1,)), ((), ())), preferred_element_type=F32)
    o = o + jnp.dot(att, v, preferred_element_type=F32)
    ke = (kt * jnp.exp(b_end - b_ref)).astype(BF16)
    st_ref[...] = jnp.exp(b_end) * st + lax.dot_general(
        v, ke, (((0,), (0,)), ((), ())), preferred_element_type=F32)
    return o


def _hgrn_kernel(qc_ref, vc_ref, fcf_ref, fcb_ref, qf_ref, vf_ref, ff_ref, qb_ref, vb_ref, fb_ref,
                 lb_ref, la_ref, l1_ref, ocf_ref, ocb_ref, of_ref, ob_ref, sf_ref, sb_ref,
                 *, n_ctx_chunks, n_lat_chunks):
    s = pl.program_id(2)
    lb_f, la_f, l1_f = lb_ref[0:1, :], la_ref[0:1, :], l1_ref[0:1, :]
    lb_b, la_b, l1_b = lb_ref[1:2, :], la_ref[1:2, :], l1_ref[1:2, :]

    def run(n_chunks, q_f, v_f, f_f, o_f, q_b, v_b, f_b, o_b):
        def body(c, carry):
            rf = pl.ds(pl.multiple_of(c * CHUNK, CHUNK), CHUNK)
            rb = pl.ds(pl.multiple_of((n_chunks - 1 - c) * CHUNK, CHUNK), CHUNK)
            o_f[0, 0, rf, :] = _hgrn_chunk(q_f[0, 0, rf, :], v_f[0, 0, rf, :], f_f[0, 0, rf, :],
                                           lb_f, la_f, l1_f, sf_ref, False)
            o_b[0, 0, rb, :] = _hgrn_chunk(q_b[0, 0, rb, :], v_b[0, 0, rb, :], f_b[0, 0, rb, :],
                                           lb_b, la_b, l1_b, sb_ref, True)
            return carry
        lax.fori_loop(0, n_chunks, body, 0)

    @pl.when(s == 0)
    def _():
        sf_ref[...] = jnp.zeros_like(sf_ref)
        sb_ref[...] = jnp.zeros_like(sb_ref)
        run(n_ctx_chunks, qc_ref, vc_ref, fcf_ref, ocf_ref, qc_ref, vc_ref, fcb_ref, ocb_ref)

    run(n_lat_chunks, qf_ref, vf_ref, ff_ref, of_ref, qb_ref, vb_ref, fb_ref, ob_ref)


def _hgrn_call(p_lat, f_lat, p_ctx, f_ctx, lb, la, l1):
    _, bsz, t, d = p_lat.shape
    tc = p_ctx.shape[2]
    w = HEAD_W
    blk = min(t, 1024)
    ns = t // blk
    lat = lambda g, rev: pl.BlockSpec(
        (1, 1, blk, w), (lambda b, h, s: (g, b, ns - 1 - s, h)) if rev else (lambda b, h, s: (g, b, s, h)))
    ctx = lambda g: pl.BlockSpec((1, 1, tc, w), lambda b, h, s: (g, b, 0, h))
    row = pl.BlockSpec((2, w), lambda b, h, s: (0, h))
    return pl.pallas_call(
        functools.partial(_hgrn_kernel, n_ctx_chunks=tc // CHUNK, n_lat_chunks=blk // CHUNK),
        grid=(bsz, N_HEADS, ns),
        in_specs=[ctx(0), ctx(1), ctx(0), ctx(1),
                  lat(0, False), lat(1, False), lat(0, False),
                  lat(0, True), lat(1, True), lat(1, True),
                  row, row, row],
        out_specs=[ctx(0), ctx(0), lat(0, False), lat(0, True)],
        out_shape=[jax.ShapeDtypeStruct((1, bsz, tc, d), F32), jax.ShapeDtypeStruct((1, bsz, tc, d), F32),
                   jax.ShapeDtypeStruct((1, bsz, t, d), F32), jax.ShapeDtypeStruct((1, bsz, t, d), F32)],
        scratch_shapes=[pltpu.VMEM((w, w), F32), pltpu.VMEM((w, w), F32)],
        compiler_params=_params("arbitrary", "arbitrary", "arbitrary"),
        name="hgrn",
    )(p_ctx, p_ctx, f_ctx, f_ctx, p_lat, p_lat, f_lat, p_lat, p_lat, f_lat, lb, la, l1)


def _lru_gates(xc, wa_ref, wx_ref, ba_ref, bx_ref, sp_ref, dirn):
    xb = xc.astype(BF16)
    r = _sigmoid(jnp.dot(xb, wa_ref[dirn, 0].astype(BF16), preferred_element_type=F32)
                 + ba_ref[dirn:dirn + 1, :])
    i = _sigmoid(jnp.dot(xb, wx_ref[dirn, 0].astype(BF16), preferred_element_type=F32)
                 + bx_ref[dirn:dirn + 1, :])
    log_a = -LRU_C * r * sp_ref[dirn:dirn + 1, :]
    a = jnp.exp(log_a)
    u = jnp.sqrt(1.0 - jnp.exp(2.0 * log_a)) * (i * xc)
    return a, u


def _lru_kernel(xl_ref, xc_ref, cw_ref, cb_ref, wa_ref, wx_ref, ba_ref, bx_ref, lam_ref,
                hl_ref, hc_ref,
                xpad, cpad, a_f, u_f, a_b, u_b, hfwd, ca_f, cu_f, ca_b, cu_b, chf, ends, prods, carry, sp_ref,
                *, t, tc):
    w = HEAD_W
    gw = GRID_W
    rows = t // gw
    lam = lam_ref[...]
    neg = -lam
    sp_ref[...] = jnp.maximum(neg, 0.0) + jnp.log(1.0 + jnp.exp(-jnp.abs(neg)))
    w0, w1, w2, w3 = (cw_ref[k:k + 1, :] for k in range(4))
    cb = cb_ref[...]

    cpad[...] = jnp.zeros_like(cpad)
    cpad[8:8 + tc, :] = xc_ref[0, 0].astype(F32)
    xcc = (w0 * cpad[6:6 + tc, :] + w1 * cpad[7:7 + tc, :] + w2 * cpad[8:8 + tc, :]
           + w3 * cpad[9:9 + tc, :] + cb)
    ca_f[...], cu_f[...] = _lru_gates(xcc, wa_ref, wx_ref, ba_ref, bx_ref, sp_ref, 0)
    ca_b[...], cu_b[...] = _lru_gates(xcc, wa_ref, wx_ref, ba_ref, bx_ref, sp_ref, 1)

    def ctx_fwd(i, h):
        h = ca_f[pl.ds(i, 1), :] * h + cu_f[pl.ds(i, 1), :]
        chf[pl.ds(i, 1), :] = h
        return h
    s0_f = lax.fori_loop(0, tc, ctx_fwd, jnp.zeros((1, w), F32))

    def ctx_bwd(i, h):
        p = tc - 1 - i
        h = ca_b[pl.ds(p, 1), :] * h + cu_b[pl.ds(p, 1), :]
        chf[pl.ds(p, 1), :] = chf[pl.ds(p, 1), :] + h
        return h
    s0_b = lax.fori_loop(0, tc, ctx_bwd, jnp.zeros((1, w), F32))
    hc_ref[0] = chf[...].astype(hc_ref.dtype)

    xpad[2 * gw:2 * gw + t, :] = xl_ref[0, 0].astype(F32)
    col = lax.broadcasted_iota(I32, (gw, w), 0)
    body0 = 2 * gw
    xpad[gw:2 * gw, :] = jnp.where(col == 0, 0.0, xpad[pl.ds(body0 + (rows - 1) * gw - 1, gw), :])
    xpad[0:gw, :] = jnp.where(col == 0, 0.0, xpad[pl.ds(body0 + (rows - 2) * gw - 1, gw), :])
    xpad[body0 + t:body0 + t + gw, :] = jnp.where(col == gw - 1, 0.0, xpad[pl.ds(body0 + 1, gw), :])

    gate_rows = min(t, 512)

    def gate_body(c, carry_):
        base = pl.multiple_of(c * gate_rows, gate_rows)
        xcv = (w0 * xpad[pl.ds(base, gate_rows), :] + w1 * xpad[pl.ds(base + gw, gate_rows), :]
               + w2 * xpad[pl.ds(base + 2 * gw, gate_rows), :]
               + w3 * xpad[pl.ds(base + 3 * gw, gate_rows), :] + cb)
        sl = pl.ds(base, gate_rows)
        a_f[sl, :], u_f[sl, :] = _lru_gates(xcv, wa_ref, wx_ref, ba_ref, bx_ref, sp_ref, 0)
        a_b[sl, :], u_b[sl, :] = _lru_gates(xcv, wa_ref, wx_ref, ba_ref, bx_ref, sp_ref, 1)
        return carry_
    lax.fori_loop(0, t // gate_rows, gate_body, 0)

    def slab(r):
        return pl.ds(pl.multiple_of(r * gw, gw), gw)

    def scan_dir(a_ref, u_ref, s0, reverse, emit):
        order = (lambda i: rows - 1 - i) if reverse else (lambda i: i)

        def p1(i, hp):
            h, p = hp
            sl = slab(order(i))
            a = a_ref[sl, :]
            return a * h + u_ref[sl, :], a * p
        h_end, p_end = lax.fori_loop(0, rows, p1, (jnp.zeros((gw, w), F32), jnp.ones((gw, w), F32)))
        ends[...] = h_end
        prods[...] = p_end

        def chain(i, cur):
            c = (gw - 1 - i) if reverse else i
            carry[pl.ds(c, 1), :] = cur
            return prods[pl.ds(c, 1), :] * cur + ends[pl.ds(c, 1), :]
        lax.fori_loop(0, gw, chain, s0)

        def p2(i, h):
            sl = slab(order(i))
            h = a_ref[sl, :] * h + u_ref[sl, :]
            emit(sl, h)
            return h
        lax.fori_loop(0, rows, p2, carry[...])

    def emit_f(sl, h):
        hfwd[sl, :] = h

    def emit_b(sl, h):
        hl_ref[0, sl, :] = (hfwd[sl, :] + h).astype(hl_ref.dtype)

    scan_dir(a_f, u_f, s0_f, False, emit_f)
    scan_dir(a_b, u_b, s0_b, True, emit_b)


def _lru_call(p_lat, p_ctx, g_lat, g_ctx, conv_w, conv_b, wa, wx, ba, bx, lam):
    _, bsz, t, d = p_lat.shape
    tc = p_ctx.shape[2]
    w = HEAD_W
    nb = d // w
    gw = GRID_W
    vec = lambda n: pl.BlockSpec((n, w), lambda b, k: (0, k))
    mat = pl.BlockSpec((2, 1, w, w), lambda b, k: (0, k, 0, 0))
    return pl.pallas_call(
        functools.partial(_lru_kernel, t=t, tc=tc),
        grid=(bsz, nb),
        in_specs=[
            pl.BlockSpec((1, 1, t, w), lambda b, k: (g_lat, b, 0, k)),
            pl.BlockSpec((1, 1, tc, w), lambda b, k: (g_ctx, b, 0, k)),
            vec(4), vec(1), mat, mat, vec(2), vec(2), vec(2),
        ],
        out_specs=[pl.BlockSpec((1, t, w), lambda b, k: (b, 0, k)),
                   pl.BlockSpec((1, tc, w), lambda b, k: (b, 0, k))],
        out_shape=[jax.ShapeDtypeStruct((bsz, t, d), BF16), jax.ShapeDtypeStruct((bsz, tc, d), BF16)],
        scratch_shapes=[
            pltpu.VMEM((t + 3 * gw, w), F32), pltpu.VMEM((tc + 16, w), F32),
            pltpu.VMEM((t, w), F32), pltpu.VMEM((t, w), F32), pltpu.VMEM((t, w), F32), pltpu.VMEM((t, w), F32),
            pltpu.VMEM((t, w), F32),
            pltpu.VMEM((tc, w), F32), pltpu.VMEM((tc, w), F32), pltpu.VMEM((tc, w), F32), pltpu.VMEM((tc, w), F32),
            pltpu.VMEM((tc, w), F32),
            pltpu.VMEM((gw, w), F32), pltpu.VMEM((gw, w), F32), pltpu.VMEM((gw, w), F32),
            pltpu.VMEM((2, w), F32),
        ],
        compiler_params=_params("arbitrary", "arbitrary"),
        name="rglru",
    )(p_lat, p_ctx, conv_w, conv_b.reshape(1, d), wa, wx, ba, bx, lam)


def _merge_kernel(of_ref, ob_ref, og_ref, hl_ref, ly_ref, ma_ref, mb_ref, x_ref, mod_ref, gn_ref,
                  wa_ref, wb_ref, wo_ref, lng_ref, lnb_ref, wr_ref, wrt_ref,
                  x1_ref, h2_ref, aff_ref, afft_ref, *, ctx_row):
    d = D_MODEL
    b = pl.program_id(0)
    row = b if ctx_row is None else ctx_row
    o = of_ref[0, 0] + ob_ref[0, 0]
    gn = gn_ref[...]
    parts = []
    for h in range(N_HEADS):
        oh = o[:, h * HEAD_W:(h + 1) * HEAD_W]
        ms = jnp.mean(oh * oh, axis=-1, keepdims=True)
        parts.append(oh * lax.rsqrt(ms + RMS_EPS) * gn)
    o_a = (jnp.concatenate(parts, axis=1) * og_ref[0, 0].astype(F32)).astype(BF16)
    y_a = jnp.dot(o_a, wa_ref[...], preferred_element_type=F32)
    y_b = jnp.dot((hl_ref[0].astype(F32) * ly_ref[0, 0].astype(F32)).astype(BF16), wb_ref[...],
                  preferred_element_type=F32)
    z = ma_ref[0, 0].astype(F32) * y_a + mb_ref[0, 0].astype(F32) * y_b
    y = jnp.dot(z.astype(BF16), wo_ref[...], preferred_element_type=F32)
    g1 = mod_ref[pl.ds(row, 1), 2 * d:3 * d]
    x1 = _layer_norm(ALPHA * x_ref[0] + g1 * y, lng_ref[0:1, :], lnb_ref[0:1, :])
    x1_ref[0] = x1
    sh2 = mod_ref[pl.ds(row, 1), 3 * d:4 * d]
    sc2 = mod_ref[pl.ds(row, 1), 4 * d:5 * d]
    h2 = x1 * (1.0 + sc2) + sh2
    h2_ref[0] = h2.astype(BF16)
    logits = jnp.dot(h2, wr_ref[...], preferred_element_type=F32, precision=HIGHEST)
    logits_t = lax.dot_general(wrt_ref[...], h2, (((1,), (1,)), ((), ())),
                               preferred_element_type=F32, precision=HIGHEST)
    p = jnp.exp(logits - jnp.max(logits, axis=-1, keepdims=True))
    aff_ref[0] = p / jnp.sum(p, axis=-1, keepdims=True)
    pt = jnp.exp(logits_t - jnp.max(logits_t, axis=0, keepdims=True))
    afft_ref[0] = pt / jnp.sum(pt, axis=0, keepdims=True)


def _merge_call(o_f, o_b, p, groups, h_lru, x, mod_l, gn, wa, wb, wo, ln_g, ln_b, w_router, ctx_row, name):
    bsz, t, d = x.shape
    tm = min(t, 512)
    e = N_EXPERTS
    tile = lambda g: pl.BlockSpec((1, 1, tm, d), lambda b, i: (g, b, i, 0))
    tok = pl.BlockSpec((1, tm, d), lambda b, i: (b, i, 0))
    full = lambda shape: pl.BlockSpec(shape, lambda b, i: tuple(0 for _ in shape))
    return pl.pallas_call(
        functools.partial(_merge_kernel, ctx_row=ctx_row),
        grid=(bsz, t // tm),
        in_specs=[tile(0), tile(0), tile(groups["og"]), tok, tile(groups["ly"]), tile(groups["ma"]),
                  tile(groups["mb"]), tok, full((8, 6 * d)), full((1, HEAD_W)),
                  full((d, d)), full((d, d)), full((d, d)), full((2, d)), full((2, d)),
                  full((d, e)), full((e, d))],
        out_specs=[tok, tok, pl.BlockSpec((1, tm, e), lambda b, i: (b, i, 0)),
                   pl.BlockSpec((1, e, tm), lambda b, i: (b, 0, i))],
        out_shape=[jax.ShapeDtypeStruct((bsz, t, d), F32), jax.ShapeDtypeStruct((bsz, t, d), BF16),
                   jax.ShapeDtypeStruct((bsz, t, e), F32), jax.ShapeDtypeStruct((bsz, e, t), F32)],
        compiler_params=_params("arbitrary", "arbitrary"),
        name=name,
    )(o_f, o_b, p, h_lru, p, p, p, x, mod_l, gn.reshape(1, HEAD_W), wa, wb, wo, ln_g, ln_b,
      w_router, w_router.T)


def _route_kernel(a_ref, posm_ref, off_ref, *, cap, t, tt):
    e = N_EXPERTS
    u = lax.bitcast_convert_type(a_ref[0], I32)
    thr = jnp.zeros((e, 1), I32)
    for bit in range(30, -1, -1):
        cand = thr | (1 << bit)
        cnt = jnp.sum((u >= cand).astype(F32), axis=1, keepdims=True)
        thr = jnp.where(cnt >= cap, cand, thr)
    gt = u > thr
    eq = u == thr
    need = cap - jnp.sum(gt.astype(F32), axis=1, keepdims=True)
    ri = lax.broadcasted_iota(I32, (tt, tt), 0)
    ci = lax.broadcasted_iota(I32, (tt, tt), 1)
    before = (ri < ci).astype(BF16)
    lane = lax.broadcasted_iota(I32, (e, OFF_STRIDE), 1)
    offs = jnp.zeros((e, OFF_STRIDE), F32)
    n_eq = jnp.zeros((e, 1), F32)
    n_sel = jnp.zeros((e, 1), F32)
    for j in range(t // tt):
        sl = slice(j * tt, (j + 1) * tt)
        eq_j = eq[:, sl]
        rank_eq = jnp.dot(eq_j.astype(BF16), before, preferred_element_type=F32) + n_eq
        sel_j = gt[:, sl] | (eq_j & (rank_eq < need))
        sel_f = sel_j.astype(F32)
        pos = jnp.dot(sel_j.astype(BF16), before, preferred_element_type=F32) + n_sel
        posm_ref[0, :, sl] = jnp.where(sel_j, pos.astype(I32), -1)
        offs = jnp.where(lane == j, n_sel, offs)
        n_eq = n_eq + jnp.sum(eq_j.astype(F32), axis=1, keepdims=True)
        n_sel = n_sel + jnp.sum(sel_f, axis=1, keepdims=True)
    offs = jnp.where(lane == t // tt, n_sel, offs)
    off_ref[0] = offs.astype(I32)


def _route_call(aff_t, cap, tt, name):
    bsz, e, t = aff_t.shape
    return pl.pallas_call(
        functools.partial(_route_kernel, cap=cap, t=t, tt=tt),
        grid=(bsz,),
        in_specs=[pl.BlockSpec((1, e, t), lambda b: (b, 0, 0))],
        out_specs=[pl.BlockSpec((1, e, t), lambda b: (b, 0, 0)),
                   pl.BlockSpec((1, e, OFF_STRIDE), lambda b: (b, 0, 0))],
        out_shape=[jax.ShapeDtypeStruct((bsz, e, t), I32), jax.ShapeDtypeStruct((bsz, e, OFF_STRIDE), I32)],
        compiler_params=_params("arbitrary"),
        name=name,
    )(aff_t)


def _expert_kernel(off_ref, posm_ref, h_ref, wg_ref, wu_ref, wd_ref, y_ref, x_ref, *, cap, rt, tt, nt):
    b = pl.program_id(0)
    e = pl.program_id(1)
    base = (b * N_EXPERTS + e) * OFF_STRIDE
    x_ref[...] = jnp.zeros_like(x_ref)
    rows = lax.broadcasted_iota(I32, (rt, tt), 0)
    for i in range(cap // rt):
        def body(j, carry, i=i):
            lo = off_ref[base + j]
            hi = off_ref[base + j + 1]

            @pl.when((lo < (i + 1) * rt) & (hi > i * rt))
            def _():
                tok = pl.ds(pl.multiple_of(j * tt, tt), tt)
                onehot = ((posm_ref[0, 0, :, tok] - i * rt) == rows).astype(BF16)
                x_ref[i * rt:(i + 1) * rt, :] += jnp.dot(onehot, h_ref[0, tok, :],
                                                         preferred_element_type=F32)
            return carry
        lax.fori_loop(0, nt, body, 0)
    ft = min(cap, 256)
    for r in range(cap // ft):
        xb = x_ref[r * ft:(r + 1) * ft, :].astype(BF16)
        g = jnp.dot(xb, wg_ref[0], preferred_element_type=F32)
        u = jnp.dot(xb, wu_ref[0], preferred_element_type=F32)
        hid = (g * _sigmoid(g) * u).astype(BF16)
        y_ref[0, 0, r * ft:(r + 1) * ft, :] = jnp.dot(hid, wd_ref[0], preferred_element_type=F32).astype(BF16)


def _expert_call(off_flat, posm, h2, wg, wu, wd, cap, rt, tt, name):
    bsz, t, d = h2.shape
    e, _, f = wg.shape
    nt = t // tt
    grid_spec = pltpu.PrefetchScalarGridSpec(
        num_scalar_prefetch=1,
        grid=(bsz, e),
        in_specs=[
            pl.BlockSpec((1, 1, 1, t), lambda b, k, off: (b, k, 0, 0)),
            pl.BlockSpec((1, t, d), lambda b, k, off: (b, 0, 0), pipeline_mode=pl.Buffered(1)),
            pl.BlockSpec((1, d, f), lambda b, k, off: (k, 0, 0)),
            pl.BlockSpec((1, d, f), lambda b, k, off: (k, 0, 0)),
            pl.BlockSpec((1, f, d), lambda b, k, off: (k, 0, 0)),
        ],
        out_specs=pl.BlockSpec((1, 1, cap, d), lambda b, k, off: (b, k, 0, 0)),
        scratch_shapes=[pltpu.VMEM((cap, d), F32)],
    )
    return pl.pallas_call(
        functools.partial(_expert_kernel, cap=cap, rt=rt, tt=tt, nt=nt),
        grid_spec=grid_spec,
        out_shape=jax.ShapeDtypeStruct((bsz, e, cap, d), BF16),
        compiler_params=_params("arbitrary", "arbitrary"),
        name=name,
    )(off_flat, posm.reshape(bsz, e, 1, t), h2, wg, wu, wd)


def _combine_kernel(off_ref, posm_ref, aff_ref, x1_ref, y_hbm, mod_ref, lng_ref, lnb_ref, o_ref,
                    ybuf, sem, acc_ref, *, rt, tt, nw, ctx_row):
    d = D_MODEL
    b = pl.program_id(0)
    j = pl.program_id(1)
    row = b if ctx_row is None else ctx_row
    shift = rt.bit_length() - 1

    def window(e):
        base = (b * N_EXPERTS + e) * OFF_STRIDE
        lo = off_ref[base + j]
        hi = off_ref[base + j + 1]
        t0 = lax.shift_right_logical(lo, shift)
        n = jnp.where(hi > lo, lax.shift_right_logical(hi - 1, shift) - t0 + 1, 0)
        return t0, n

    def copy(e, wdx, t0):
        src = y_hbm.at[b, e, pl.ds(pl.multiple_of((t0 + wdx) * rt, rt), rt), :]
        return pltpu.make_async_copy(src, ybuf.at[e, wdx], sem.at[e, wdx])

    for e in range(N_EXPERTS):
        t0, n = window(e)
        for wdx in range(nw):
            @pl.when(wdx < n)
            def _(e=e, wdx=wdx, t0=t0):
                copy(e, wdx, t0).start()

    acc_ref[...] = jnp.zeros_like(acc_ref)
    rows = lax.broadcasted_iota(I32, (rt, tt), 0)
    for e in range(N_EXPERTS):
        t0, n = window(e)
        for wdx in range(nw):
            @pl.when(wdx < n)
            def _(e=e, wdx=wdx, t0=t0):
                copy(e, wdx, t0).wait()
                onehot = ((posm_ref[0, e] - (t0 + wdx) * rt) == rows).astype(BF16)
                got = lax.dot_general(onehot, ybuf[e, wdx], (((0,), (0,)), ((), ())),
                                      preferred_element_type=F32)
                acc_ref[...] += aff_ref[0, :, e:e + 1] * got

    g2 = mod_ref[pl.ds(row, 1), 5 * d:6 * d]
    o_ref[0] = _layer_norm(ALPHA * x1_ref[0] + g2 * acc_ref[...], lng_ref[1:2, :], lnb_ref[1:2, :])


def _combine_call(off_flat, posm, aff, x1, y, mod_l, ln_g, ln_b, rt, tt, ctx_row, name):
    bsz, t, d = x1.shape
    e = N_EXPERTS
    cap = y.shape[2]
    nw = min(cap // rt, (tt - 1) // rt + 2)
    grid_spec = pltpu.PrefetchScalarGridSpec(
        num_scalar_prefetch=1,
        grid=(bsz, t // tt),
        in_specs=[
            pl.BlockSpec((1, e, 1, tt), lambda b, j, off: (b, 0, 0, j)),
            pl.BlockSpec((1, tt, e), lambda b, j, off: (b, j, 0)),
            pl.BlockSpec((1, tt, d), lambda b, j, off: (b, j, 0)),
            pl.BlockSpec(memory_space=pl.ANY),
            pl.BlockSpec((8, 6 * d), lambda b, j, off: (0, 0)),
            pl.BlockSpec((2, d), lambda b, j, off: (0, 0)),
            pl.BlockSpec((2, d), lambda b, j, off: (0, 0)),
        ],
        out_specs=pl.BlockSpec((1, tt, d), lambda b, j, off: (b, j, 0)),
        scratch_shapes=[pltpu.VMEM((e, nw, rt, d), BF16), pltpu.SemaphoreType.DMA((e, nw)),
                        pltpu.VMEM((tt, d), F32)],
    )
    return pl.pallas_call(
        functools.partial(_combine_kernel, rt=rt, tt=tt, nw=nw, ctx_row=ctx_row),
        grid_spec=grid_spec,
        out_shape=jax.ShapeDtypeStruct((bsz, t, d), F32),
        compiler_params=_params("arbitrary", "arbitrary"),
        name=name,
    )(off_flat, posm.reshape(bsz, e, 1, t), aff, x1, y, mod_l, ln_g, ln_b)


def _moe(x1, h2, aff, aff_t, mod_l, wg, wu, wd, ln_g, ln_b, ctx_row, tag):
    bsz, t, d = x1.shape
    cap = EC_FACTOR * t // N_EXPERTS
    tt = min(t, 256)
    rt = min(cap, 128)
    posm, off = _route_call(aff_t, cap, tt, "route_" + tag)
    off_flat = off.reshape(-1)
    y = _expert_call(off_flat, posm, h2, wg, wu, wd, cap, rt, tt, "expert_" + tag)
    return _combine_call(off_flat, posm, aff, x1, y, mod_l, ln_g, ln_b, rt, tt, ctx_row, "combine_" + tag)


def kernel(x, c, ctx, c_ctx, w_mod, b_mod, w_in, hgrn_lb_logits, hgrn_norm_g, conv_w, conv_b, lru_wa,
           lru_ba, lru_wx, lru_bx, lru_lambda, w_branch_a, w_branch_b, w_out, ln_g, ln_b, w_router,
           w_gate, w_up, w_down):
    depth = w_in.shape[0]
    bsz = x.shape[0]
    ctx_row = bsz
    assert bsz < 8 and depth == DEPTH

    lb_cum = jnp.cumsum(jax.nn.softmax(hgrn_lb_logits.astype(F32), axis=0), axis=0)
    lbs = lb_cum - lb_cum[0]
    log_lb = jnp.log(lbs)
    log1m_lb = jnp.log1p(-lbs)

    c_all = jnp.zeros((8, x.shape[2]), F32).at[:bsz].set(c).at[ctx_row].set(c_ctx)
    mod = _mod_call(c_all, w_mod, b_mod)

    p_groups = (G_Q, G_V, G_OG, G_LX, G_LY, G_MA, G_MB)
    p_acts = ("silu_scale", "id", "silu", "id", "gelu", "sigmoid", "sigmoid")
    p_index = {"og": 2, "lx": 3, "ly": 4, "ma": 5, "mb": 6}
    f_groups = (G_FF, G_FB)
    f_acts = ("id", "id")

    for l in range(depth):
        need_ctx = l < depth - 1
        mod_l = mod[l]
        p_lat = _proj_call(x, mod_l, w_in[l], p_groups, p_acts, BF16, None, "proj_lat")
        f_lat = _proj_call(x, mod_l, w_in[l], f_groups, f_acts, F32, None, "gates_lat")
        p_ctx = _proj_call(ctx, mod_l, w_in[l], p_groups, p_acts, BF16, ctx_row, "proj_ctx")
        f_ctx = _proj_call(ctx, mod_l, w_in[l], f_groups, f_acts, F32, ctx_row, "gates_ctx")

        oc_f, oc_b, ol_f, ol_b = _hgrn_call(p_lat, f_lat, p_ctx, f_ctx, lbs[l], log_lb[l], log1m_lb[l])
        h_lat, h_ctx = _lru_call(p_lat, p_ctx, p_index["lx"], p_index["lx"], conv_w[l], conv_b[l],
                                 lru_wa[l], lru_wx[l], lru_ba[l], lru_bx[l], lru_lambda[l])

        wa, wb, wo = (w.astype(BF16) for w in (w_branch_a[l], w_branch_b[l], w_out[l]))
        wg, wu, wd = (w.astype(BF16) for w in (w_gate[l], w_up[l], w_down[l]))
        x1, h2, aff, aff_t = _merge_call(ol_f, ol_b, p_lat, p_index, h_lat, x, mod_l, hgrn_norm_g[l],
                                         wa, wb, wo, ln_g[l], ln_b[l], w_router[l], None, "merge_lat")
        if need_ctx:
            c1, ch2, caff, caff_t = _merge_call(oc_f, oc_b, p_ctx, p_index, h_ctx, ctx, mod_l, hgrn_norm_g[l],
                                                wa, wb, wo, ln_g[l], ln_b[l], w_router[l], ctx_row,
                                                "merge_ctx")
        x = _moe(x1, h2, aff, aff_t, mod_l, wg, wu, wd, ln_g[l], ln_b[l], None, "lat")
        if need_ctx:
            ctx = _moe(c1, ch2, caff, caff_t, mod_l, wg, wu, wd, ln_g[l], ln_b[l], ctx_row, "ctx")
    return x
```

```python
import functools

import jax
import jax.numpy as jnp
from jax import lax
from jax.experimental import pallas as pl
from jax.experimental.pallas import tpu as pltpu

F32 = jnp.float32
BF16 = jnp.bfloat16
I32 = jnp.int32
HIGHEST = lax.Precision.HIGHEST

DEPTH = 2
D_MODEL = 1024
GRID_W = 64
N_HEADS = 8
HEAD_W = 128
CHUNK = 64
N_EXPERTS = 16
EC_FACTOR = 2
LRU_C = 8.0
ALPHA = (2.0 * DEPTH) ** 0.25
LN_EPS = 1e-5
RMS_EPS = 1e-6
EXP_CLAMP = 80.0

V7X_VMEM_LIMIT_BYTES = 56 * 1024 * 1024
OFF_STRIDE = 128

G_Q, G_V, G_FF, G_FB, G_OG, G_LX, G_LY, G_MA, G_MB = range(9)


def _params(*sem):
    return pltpu.CompilerParams(dimension_semantics=sem, vmem_limit_bytes=V7X_VMEM_LIMIT_BYTES)


def _sigmoid(x):
    return 1.0 / (1.0 + jnp.exp(-x))


def _layer_norm(x, g, b):
    mu = jnp.mean(x, axis=-1, keepdims=True)
    xc = x - mu
    var = jnp.mean(xc * xc, axis=-1, keepdims=True)
    return xc * lax.rsqrt(var + LN_EPS) * g + b


def _mod_kernel(c_ref, w_ref, b_ref, o_ref):
    c = c_ref[...]
    s = c * _sigmoid(c)
    o_ref[0] = jnp.dot(s, w_ref[0], preferred_element_type=F32, precision=HIGHEST) + b_ref[0]


def _mod_call(c_all, w_mod, b_mod):
    depth, d, n = w_mod.shape
    tn = 1536
    return pl.pallas_call(
        _mod_kernel,
        grid=(depth, n // tn),
        in_specs=[
            pl.BlockSpec((8, d), lambda l, j: (0, 0)),
            pl.BlockSpec((1, d, tn), lambda l, j: (l, 0, j)),
            pl.BlockSpec((1, 1, tn), lambda l, j: (l, 0, j)),
        ],
        out_specs=pl.BlockSpec((1, 8, tn), lambda l, j: (l, 0, j)),
        out_shape=jax.ShapeDtypeStruct((depth, 8, n), F32),
        compiler_params=_params("arbitrary", "arbitrary"),
        name="mod",
    )(c_all, w_mod, b_mod.reshape(depth, 1, n))


def _apply_act(act, p):
    if act == "id":
        return p
    if act == "silu_scale":
        return p * _sigmoid(p) * (HEAD_W ** -0.5)
    if act == "silu":
        return p * _sigmoid(p)
    if act == "sigmoid":
        return _sigmoid(p)
    if act == "gelu":
        return 0.5 * p * (1.0 + jnp.tanh(0.7978845608028654 * (p + 0.044715 * (p * p * p))))
    raise ValueError(act)


P_GROUPS = (G_Q, G_V, G_OG, G_LX, G_LY, G_MA, G_MB)
P_ACTS = ("silu_scale", "id", "silu", "id", "gelu", "sigmoid", "sigmoid")
P_INDEX = {"q": 0, "v": 1, "og": 2, "lx": 3, "ly": 4, "ma": 5, "mb": 6}


def _hgrn_gates(fp, lb, la, l1):
    e = jnp.exp(-jnp.abs(fp))
    r = 1.0 / (1.0 + e)
    log_sig = jnp.minimum(fp, 0.0) - jnp.log(1.0 + e)
    k = (1.0 - lb) * jnp.where(fp >= 0.0, e * r, r)
    b2 = l1 + log_sig
    m = jnp.maximum(la, b2)
    log_f = m + jnp.log(jnp.exp(la - m) + jnp.exp(b2 - m))
    return log_f, k


def _proj_kernel(x_ref, mod_ref, w_ref, lb_ref, la_ref, l1_ref, p_ref, lf_ref, kk_ref, *, ctx_row):
    d = D_MODEL
    b = pl.program_id(0)
    row = b if ctx_row is None else ctx_row
    sh = mod_ref[pl.ds(row, 1), 0:d]
    sc = mod_ref[pl.ds(row, 1), d:2 * d]
    h = (x_ref[0] * (1.0 + sc) + sh).astype(BF16)
    for g, (col, act) in enumerate(zip(P_GROUPS, P_ACTS)):
        p = jnp.dot(h, w_ref[:, col * d:(col + 1) * d], preferred_element_type=F32)
        p_ref[g, 0] = _apply_act(act, p).astype(BF16)
    for dirn, col in enumerate((G_FF, G_FB)):
        fp = jnp.dot(h, w_ref[:, col * d:(col + 1) * d], preferred_element_type=F32)
        log_f, k = _hgrn_gates(fp, lb_ref[dirn:dirn + 1, :], la_ref[dirn:dirn + 1, :],
                               l1_ref[dirn:dirn + 1, :])
        lf_ref[dirn, 0] = log_f
        kk_ref[dirn, 0] = k.astype(BF16)


def _proj_call(x, mod_l, w_in_bf, lb, la, l1, ctx_row, name):
    bsz, t, d = x.shape
    tm = min(t, 512)
    ng = len(P_GROUPS)
    full = lambda shape: pl.BlockSpec(shape, lambda b, i: tuple(0 for _ in shape))
    out = lambda n: pl.BlockSpec((n, 1, tm, d), lambda b, i: (0, b, i, 0))
    return pl.pallas_call(
        functools.partial(_proj_kernel, ctx_row=ctx_row),
        grid=(bsz, t // tm),
        in_specs=[
            pl.BlockSpec((1, tm, d), lambda b, i: (b, i, 0)),
            full((8, 6 * d)),
            pl.BlockSpec(w_in_bf.shape, lambda b, i: (0, 0), pipeline_mode=pl.Buffered(1)),
            full((2, d)), full((2, d)), full((2, d)),
        ],
        out_specs=[out(ng), out(2), out(2)],
        out_shape=[jax.ShapeDtypeStruct((ng, bsz, t, d), BF16), jax.ShapeDtypeStruct((2, bsz, t, d), F32),
                   jax.ShapeDtypeStruct((2, bsz, t, d), BF16)],
        compiler_params=_params("arbitrary", "arbitrary"),
        name=name,
    )(x, mod_l, w_in_bf, lb, la, l1)


def _hgrn_rows(q_ref, v_ref, lf_ref, k_ref, o_ref, row0, n, st_ref, reverse):
    w = HEAD_W
    rows = pl.ds(row0, n * CHUNK)
    log_f = lf_ref[0, 0, rows, :]
    k = k_ref[0, 0, rows, :].astype(F32)
    q = q_ref[0, 0, rows, :]
    v = v_ref[0, 0, rows, :]

    ri = lax.broadcasted_iota(I32, (CHUNK, CHUNK), 0)
    ci = lax.broadcasted_iota(I32, (CHUNK, CHUNK), 1)
    causal = (ci >= ri) if reverse else (ci <= ri)
    tri = causal.astype(BF16)
    lf = jnp.concatenate([log_f[c * CHUNK:(c + 1) * CHUNK, :] for c in range(n)], axis=1)
    hi = lf.astype(BF16)
    rest = lf - hi.astype(F32)
    mid = rest.astype(BF16)
    lo = (rest - mid.astype(F32)).astype(BF16)
    bc = (jnp.dot(tri, hi, preferred_element_type=F32) + jnp.dot(tri, mid, preferred_element_type=F32)
          + jnp.dot(tri, lo, preferred_element_type=F32))

    half = CHUNK // 2
    o_intra, qs_all, upd, decay = [], [], [], []
    for c in range(n):
        sl = slice(c * CHUNK, (c + 1) * CHUNK)
        bcum = bc[:, c * w:(c + 1) * w]
        b_ref = bcum[half:half + 1, :]
        b_end = bcum[0:1, :] if reverse else bcum[CHUNK - 1:CHUNK, :]
        e1 = jnp.exp(jnp.minimum(bcum - b_ref, EXP_CLAMP))
        e2 = jnp.exp(jnp.minimum(b_ref - bcum, EXP_CLAMP))
        qt = q[sl].astype(F32) * e1
        kt = k[sl] * e2
        att = lax.dot_general(qt.astype(BF16), kt.astype(BF16), (((1,), (1,)), ((), ())),
                              preferred_element_type=F32)
        att = jnp.where(causal, att, 0.0).astype(BF16)
        o_intra.append(jnp.dot(att, v[sl], preferred_element_type=F32))
        qs_all.append((qt * jnp.exp(b_ref)).astype(BF16))
        ke = (kt * jnp.exp(b_end - b_ref)).astype(BF16)
        upd.append(lax.dot_general(v[sl], ke, (((0,), (0,)), ((), ())), preferred_element_type=F32))
        decay.append(jnp.exp(b_end))

    st = st_ref[...]
    for c in (range(n - 1, -1, -1) if reverse else range(n)):
        o = o_intra[c] + lax.dot_general(qs_all[c], st.astype(BF16), (((1,), (1,)), ((), ())),
                                         preferred_element_type=F32)
        o_ref[0, 0, pl.ds(row0 + c * CHUNK, CHUNK), :] = o
        st = decay[c] * st + upd[c]
    st_ref[...] = st


HGRN_SUB = 8


def _hgrn_kernel(qc_ref, vc_ref, lcf_ref, lcb_ref, kcf_ref, kcb_ref,
                 qf_ref, vf_ref, lff_ref, kf_ref, qb_ref, vb_ref, lfb_ref, kb_ref,
                 ocf_ref, ocb_ref, of_ref, ob_ref, sf_ref, sb_ref, *, n_ctx_chunks, n_lat_chunks):
    s = pl.program_id(2)

    def run(n_chunks, fwd, bwd):
        sub = min(n_chunks, HGRN_SUB)
        n_sub = n_chunks // sub

        def body(i, carry):
            rf = pl.multiple_of(i * (sub * CHUNK), sub * CHUNK)
            rb = pl.multiple_of((n_sub - 1 - i) * (sub * CHUNK), sub * CHUNK)
            _hgrn_rows(*fwd, rf, sub, sf_ref, False)
            _hgrn_rows(*bwd, rb, sub, sb_ref, True)
            return carry
        lax.fori_loop(0, n_sub, body, 0)

    @pl.when(s == 0)
    def _():
        sf_ref[...] = jnp.zeros_like(sf_ref)
        sb_ref[...] = jnp.zeros_like(sb_ref)
        run(n_ctx_chunks, (qc_ref, vc_ref, lcf_ref, kcf_ref, ocf_ref), (qc_ref, vc_ref, lcb_ref, kcb_ref, ocb_ref))

    run(n_lat_chunks, (qf_ref, vf_ref, lff_ref, kf_ref, of_ref), (qb_ref, vb_ref, lfb_ref, kb_ref, ob_ref))


def _hgrn_call(p_lat, lf_lat, kk_lat, p_ctx, lf_ctx, kk_ctx):
    _, bsz, t, d = p_lat.shape
    tc = p_ctx.shape[2]
    w = HEAD_W
    blk = min(t, 1024)
    ns = t // blk
    lat = lambda g, rev: pl.BlockSpec(
        (1, 1, blk, w), (lambda b, h, s: (g, b, ns - 1 - s, h)) if rev else (lambda b, h, s: (g, b, s, h)))
    ctx = lambda g: pl.BlockSpec((1, 1, tc, w), lambda b, h, s: (g, b, 0, h))
    return pl.pallas_call(
        functools.partial(_hgrn_kernel, n_ctx_chunks=tc // CHUNK, n_lat_chunks=blk // CHUNK),
        grid=(bsz, N_HEADS, ns),
        in_specs=[ctx(0), ctx(1), ctx(0), ctx(1), ctx(0), ctx(1),
                  lat(0, False), lat(1, False), lat(0, False), lat(0, False),
                  lat(0, True), lat(1, True), lat(1, True), lat(1, True)],
        out_specs=[ctx(0), ctx(0), lat(0, False), lat(0, True)],
        out_shape=[jax.ShapeDtypeStruct((1, bsz, tc, d), F32), jax.ShapeDtypeStruct((1, bsz, tc, d), F32),
                   jax.ShapeDtypeStruct((1, bsz, t, d), F32), jax.ShapeDtypeStruct((1, bsz, t, d), F32)],
        scratch_shapes=[pltpu.VMEM((w, w), F32), pltpu.VMEM((w, w), F32)],
        compiler_params=_params("arbitrary", "arbitrary", "arbitrary"),
        name="hgrn",
    )(p_ctx, p_ctx, lf_ctx, lf_ctx, kk_ctx, kk_ctx,
      p_lat, p_lat, lf_lat, kk_lat, p_lat, p_lat, lf_lat, kk_lat)


def _lru_gates(xc, wa_ref, wx_ref, ba_ref, bx_ref, sp_ref, dirn):
    xb = xc.astype(BF16)
    r = _sigmoid(jnp.dot(xb, wa_ref[dirn, 0].astype(BF16), preferred_element_type=F32)
                 + ba_ref[dirn:dirn + 1, :])
    i = _sigmoid(jnp.dot(xb, wx_ref[dirn, 0].astype(BF16), preferred_element_type=F32)
                 + bx_ref[dirn:dirn + 1, :])
    log_a = -LRU_C * r * sp_ref[dirn:dirn + 1, :]
    a = jnp.exp(log_a)
    u = jnp.sqrt(1.0 - jnp.exp(2.0 * log_a)) * (i * xc)
    return a, u


def _lru_kernel(xl_ref, xc_ref, cw_ref, cb_ref, wa_ref, wx_ref, ba_ref, bx_ref, lam_ref,
                hl_ref, hc_ref,
                xpad, cpad, a_f, u_f, a_b, u_b, hfwd, ca_f, cu_f, ca_b, cu_b, chf, ends, prods, carry, sp_ref,
                *, t, tc):
    w = HEAD_W
    gw = GRID_W
    rows = t // gw
    lam = lam_ref[...]
    neg = -lam
    sp_ref[...] = jnp.maximum(neg, 0.0) + jnp.log(1.0 + jnp.exp(-jnp.abs(neg)))
    w0, w1, w2, w3 = (cw_ref[k:k + 1, :] for k in range(4))
    cb = cb_ref[...]

    cpad[...] = jnp.zeros_like(cpad)
    cpad[8:8 + tc, :] = xc_ref[0, 0].astype(F32)
    xcc = (w0 * cpad[6:6 + tc, :] + w1 * cpad[7:7 + tc, :] + w2 * cpad[8:8 + tc, :]
           + w3 * cpad[9:9 + tc, :] + cb)
    ca_f[...], cu_f[...] = _lru_gates(xcc, wa_ref, wx_ref, ba_ref, bx_ref, sp_ref, 0)
    ca_b[...], cu_b[...] = _lru_gates(xcc, wa_ref, wx_ref, ba_ref, bx_ref, sp_ref, 1)

    def ctx_fwd(i, h):
        h = ca_f[pl.ds(i, 1), :] * h + cu_f[pl.ds(i, 1), :]
        chf[pl.ds(i, 1), :] = h
        return h
    s0_f = lax.fori_loop(0, tc, ctx_fwd, jnp.zeros((1, w), F32))

    def ctx_bwd(i, h):
        p = tc - 1 - i
        h = ca_b[pl.ds(p, 1), :] * h + cu_b[pl.ds(p, 1), :]
        chf[pl.ds(p, 1), :] = chf[pl.ds(p, 1), :] + h
        return h
    s0_b = lax.fori_loop(0, tc, ctx_bwd, jnp.zeros((1, w), F32))
    hc_ref[0] = chf[...].astype(hc_ref.dtype)

    xpad[2 * gw:2 * gw + t, :] = xl_ref[0, 0].astype(F32)
    col = lax.broadcasted_iota(I32, (gw, w), 0)
    body0 = 2 * gw
    xpad[gw:2 * gw, :] = jnp.where(col == 0, 0.0, xpad[pl.ds(body0 + (rows - 1) * gw - 1, gw), :])
    xpad[0:gw, :] = jnp.where(col == 0, 0.0, xpad[pl.ds(body0 + (rows - 2) * gw - 1, gw), :])
    xpad[body0 + t:body0 + t + gw, :] = jnp.where(col == gw - 1, 0.0, xpad[pl.ds(body0 + 1, gw), :])

    gate_rows = min(t, 512)

    def gate_body(c, carry_):
        base = pl.multiple_of(c * gate_rows, gate_rows)
        xcv = (w0 * xpad[pl.ds(base, gate_rows), :] + w1 * xpad[pl.ds(base + gw, gate_rows), :]
               + w2 * xpad[pl.ds(base + 2 * gw, gate_rows), :]
               + w3 * xpad[pl.ds(base + 3 * gw, gate_rows), :] + cb)
        sl = pl.ds(base, gate_rows)
        a_f[sl, :], u_f[sl, :] = _lru_gates(xcv, wa_ref, wx_ref, ba_ref, bx_ref, sp_ref, 0)
        a_b[sl, :], u_b[sl, :] = _lru_gates(xcv, wa_ref, wx_ref, ba_ref, bx_ref, sp_ref, 1)
        return carry_
    lax.fori_loop(0, t // gate_rows, gate_body, 0)

    def slab(r):
        return pl.ds(pl.multiple_of(r * gw, gw), gw)

    def scan_dir(a_ref, u_ref, s0, reverse, emit):
        order = (lambda i: rows - 1 - i) if reverse else (lambda i: i)

        def p1(i, hp):
            h, p = hp
            sl = slab(order(i))
            a = a_ref[sl, :]
            return a * h + u_ref[sl, :], a * p
        h_end, p_end = lax.fori_loop(0, rows, p1, (jnp.zeros((gw, w), F32), jnp.ones((gw, w), F32)))
        ends[...] = h_end
        prods[...] = p_end

        def chain(i, cur):
            c = (gw - 1 - i) if reverse else i
            carry[pl.ds(c, 1), :] = cur
            return prods[pl.ds(c, 1), :] * cur + ends[pl.ds(c, 1), :]
        lax.fori_loop(0, gw, chain, s0)

        def p2(i, h):
            sl = slab(order(i))
            h = a_ref[sl, :] * h + u_ref[sl, :]
            emit(sl, h)
            return h
        lax.fori_loop(0, rows, p2, carry[...])

    def emit_f(sl, h):
        hfwd[sl, :] = h

    def emit_b(sl, h):
        hl_ref[0, sl, :] = (hfwd[sl, :] + h).astype(hl_ref.dtype)

    scan_dir(a_f, u_f, s0_f, False, emit_f)
    scan_dir(a_b, u_b, s0_b, True, emit_b)


def _lru_call(p_lat, p_ctx, g_lat, g_ctx, conv_w, conv_b, wa, wx, ba, bx, lam):
    _, bsz, t, d = p_lat.shape
    tc = p_ctx.shape[2]
    w = HEAD_W
    nb = d // w
    gw = GRID_W
    vec = lambda n: pl.BlockSpec((n, w), lambda b, k: (0, k))
    mat = pl.BlockSpec((2, 1, w, w), lambda b, k: (0, k, 0, 0))
    return pl.pallas_call(
        functools.partial(_lru_kernel, t=t, tc=tc),
        grid=(bsz, nb),
        in_specs=[
            pl.BlockSpec((1, 1, t, w), lambda b, k: (g_lat, b, 0, k)),
            pl.BlockSpec((1, 1, tc, w), lambda b, k: (g_ctx, b, 0, k)),
            vec(4), vec(1), mat, mat, vec(2), vec(2), vec(2),
        ],
        out_specs=[pl.BlockSpec((1, t, w), lambda b, k: (b, 0, k)),
                   pl.BlockSpec((1, tc, w), lambda b, k: (b, 0, k))],
        out_shape=[jax.ShapeDtypeStruct((bsz, t, d), BF16), jax.ShapeDtypeStruct((bsz, tc, d), BF16)],
        scratch_shapes=[
            pltpu.VMEM((t + 3 * gw, w), F32), pltpu.VMEM((tc + 16, w), F32),
            pltpu.VMEM((t, w), F32), pltpu.VMEM((t, w), F32), pltpu.VMEM((t, w), F32), pltpu.VMEM((t, w), F32),
            pltpu.VMEM((t, w), F32),
            pltpu.VMEM((tc, w), F32), pltpu.VMEM((tc, w), F32), pltpu.VMEM((tc, w), F32), pltpu.VMEM((tc, w), F32),
            pltpu.VMEM((tc, w), F32),
            pltpu.VMEM((gw, w), F32), pltpu.VMEM((gw, w), F32), pltpu.VMEM((gw, w), F32),
            pltpu.VMEM((2, w), F32),
        ],
        compiler_params=_params("arbitrary", "arbitrary"),
        name="rglru",
    )(p_lat, p_ctx, conv_w, conv_b.reshape(1, d), wa, wx, ba, bx, lam)


def _merge_kernel(of_ref, ob_ref, og_ref, hl_ref, ly_ref, ma_ref, mb_ref, x_ref, mod_ref, gn_ref,
                  wa_ref, wb_ref, wo_ref, lng_ref, lnb_ref, wr_ref, wrt_ref,
                  x1_ref, h2_ref, aff_ref, afft_ref, *, ctx_row):
    d = D_MODEL
    b = pl.program_id(0)
    row = b if ctx_row is None else ctx_row
    o = of_ref[0, 0] + ob_ref[0, 0]
    gn = gn_ref[...]
    parts = []
    for h in range(N_HEADS):
        oh = o[:, h * HEAD_W:(h + 1) * HEAD_W]
        ms = jnp.mean(oh * oh, axis=-1, keepdims=True)
        parts.append(oh * lax.rsqrt(ms + RMS_EPS) * gn)
    o_a = (jnp.concatenate(parts, axis=1) * og_ref[0, 0].astype(F32)).astype(BF16)
    y_a = jnp.dot(o_a, wa_ref[...], preferred_element_type=F32)
    y_b = jnp.dot((hl_ref[0].astype(F32) * ly_ref[0, 0].astype(F32)).astype(BF16), wb_ref[...],
                  preferred_element_type=F32)
    z = ma_ref[0, 0].astype(F32) * y_a + mb_ref[0, 0].astype(F32) * y_b
    y = jnp.dot(z.astype(BF16), wo_ref[...], preferred_element_type=F32)
    g1 = mod_ref[pl.ds(row, 1), 2 * d:3 * d]
    x1 = _layer_norm(ALPHA * x_ref[0] + g1 * y, lng_ref[0:1, :], lnb_ref[0:1, :])
    x1_ref[0] = x1
    sh2 = mod_ref[pl.ds(row, 1), 3 * d:4 * d]
    sc2 = mod_ref[pl.ds(row, 1), 4 * d:5 * d]
    h2 = x1 * (1.0 + sc2) + sh2
    h2_ref[0] = h2.astype(BF16)
    logits = jnp.dot(h2, wr_ref[...], preferred_element_type=F32, precision=HIGHEST)
    logits_t = lax.dot_general(wrt_ref[...], h2, (((1,), (1,)), ((), ())),
                               preferred_element_type=F32, precision=HIGHEST)
    p = jnp.exp(logits - jnp.max(logits, axis=-1, keepdims=True))
    aff_ref[0] = p / jnp.sum(p, axis=-1, keepdims=True)
    pt = jnp.exp(logits_t - jnp.max(logits_t, axis=0, keepdims=True))
    afft_ref[0] = pt / jnp.sum(pt, axis=0, keepdims=True)


def _merge_call(o_f, o_b, p, groups, h_lru, x, mod_l, gn, wa, wb, wo, ln_g, ln_b, w_router, ctx_row, name):
    bsz, t, d = x.shape
    tm = min(t, 512)
    e = N_EXPERTS
    tile = lambda g: pl.BlockSpec((1, 1, tm, d), lambda b, i: (g, b, i, 0))
    tok = pl.BlockSpec((1, tm, d), lambda b, i: (b, i, 0))
    full = lambda shape: pl.BlockSpec(shape, lambda b, i: tuple(0 for _ in shape))
    return pl.pallas_call(
        functools.partial(_merge_kernel, ctx_row=ctx_row),
        grid=(bsz, t // tm),
        in_specs=[tile(0), tile(0), tile(groups["og"]), tok, tile(groups["ly"]), tile(groups["ma"]),
                  tile(groups["mb"]), tok, full((8, 6 * d)), full((1, HEAD_W)),
                  full((d, d)), full((d, d)), full((d, d)), full((2, d)), full((2, d)),
                  full((d, e)), full((e, d))],
        out_specs=[tok, tok, pl.BlockSpec((1, tm, e), lambda b, i: (b, i, 0)),
                   pl.BlockSpec((1, e, tm), lambda b, i: (b, 0, i))],
        out_shape=[jax.ShapeDtypeStruct((bsz, t, d), F32), jax.ShapeDtypeStruct((bsz, t, d), BF16),
                   jax.ShapeDtypeStruct((bsz, t, e), F32), jax.ShapeDtypeStruct((bsz, e, t), F32)],
        compiler_params=_params("arbitrary", "arbitrary"),
        name=name,
    )(o_f, o_b, p, h_lru, p, p, p, x, mod_l, gn.reshape(1, HEAD_W), wa, wb, wo, ln_g, ln_b,
      w_router, w_router.T)


def _route_kernel(a_ref, posm_ref, off_ref, *, cap, t, tt):
    e = N_EXPERTS
    u = lax.bitcast_convert_type(a_ref[0], I32)
    thr = jnp.zeros((e, 1), I32)
    for bit in range(30, -1, -1):
        cand = thr | (1 << bit)
        cnt = jnp.sum((u >= cand).astype(F32), axis=1, keepdims=True)
        thr = jnp.where(cnt >= cap, cand, thr)
    gt = u > thr
    eq = u == thr
    need = cap - jnp.sum(gt.astype(F32), axis=1, keepdims=True)
    ri = lax.broadcasted_iota(I32, (tt, tt), 0)
    ci = lax.broadcasted_iota(I32, (tt, tt), 1)
    before = (ri < ci).astype(BF16)
    lane = lax.broadcasted_iota(I32, (e, OFF_STRIDE), 1)
    offs = jnp.zeros((e, OFF_STRIDE), F32)
    n_eq = jnp.zeros((e, 1), F32)
    n_sel = jnp.zeros((e, 1), F32)
    for j in range(t // tt):
        sl = slice(j * tt, (j + 1) * tt)
        eq_j = eq[:, sl]
        rank_eq = jnp.dot(eq_j.astype(BF16), before, preferred_element_type=F32) + n_eq
        sel_j = gt[:, sl] | (eq_j & (rank_eq < need))
        sel_f = sel_j.astype(F32)
        pos = jnp.dot(sel_j.astype(BF16), before, preferred_element_type=F32) + n_sel
        posm_ref[0, :, sl] = jnp.where(sel_j, pos.astype(I32), -1)
        offs = jnp.where(lane == j, n_sel, offs)
        n_eq = n_eq + jnp.sum(eq_j.astype(F32), axis=1, keepdims=True)
        n_sel = n_sel + jnp.sum(sel_f, axis=1, keepdims=True)
    offs = jnp.where(lane == t // tt, n_sel, offs)
    off_ref[0] = offs.astype(I32)


def _route_call(aff_t, cap, tt, name):
    bsz, e, t = aff_t.shape
    return pl.pallas_call(
        functools.partial(_route_kernel, cap=cap, t=t, tt=tt),
        grid=(bsz,),
        in_specs=[pl.BlockSpec((1, e, t), lambda b: (b, 0, 0))],
        out_specs=[pl.BlockSpec((1, e, t), lambda b: (b, 0, 0)),
                   pl.BlockSpec((1, e, OFF_STRIDE), lambda b: (b, 0, 0))],
        out_shape=[jax.ShapeDtypeStruct((bsz, e, t), I32), jax.ShapeDtypeStruct((bsz, e, OFF_STRIDE), I32)],
        compiler_params=_params("arbitrary"),
        name=name,
    )(aff_t)


def _expert_kernel(off_ref, posm_ref, h_ref, wg_ref, wu_ref, wd_ref, y_ref, x_ref, *, cap, rt, tt, nt):
    b = pl.program_id(0)
    e = pl.program_id(1)
    base = (b * N_EXPERTS + e) * OFF_STRIDE
    x_ref[...] = jnp.zeros_like(x_ref)
    rows = lax.broadcasted_iota(I32, (rt, tt), 0)
    for i in range(cap // rt):
        def body(j, carry, i=i):
            lo = off_ref[base + j]
            hi = off_ref[base + j + 1]

            @pl.when((lo < (i + 1) * rt) & (hi > i * rt))
            def _():
                tok = pl.ds(pl.multiple_of(j * tt, tt), tt)
                onehot = ((posm_ref[0, 0, :, tok] - i * rt) == rows).astype(BF16)
                x_ref[i * rt:(i + 1) * rt, :] += jnp.dot(onehot, h_ref[0, tok, :],
                                                         preferred_element_type=F32)
            return carry
        lax.fori_loop(0, nt, body, 0)
    ft = min(cap, 256)
    for r in range(cap // ft):
        xb = x_ref[r * ft:(r + 1) * ft, :].astype(BF16)
        g = jnp.dot(xb, wg_ref[0], preferred_element_type=F32)
        u = jnp.dot(xb, wu_ref[0], preferred_element_type=F32)
        hid = (g * _sigmoid(g) * u).astype(BF16)
        y_ref[0, 0, r * ft:(r + 1) * ft, :] = jnp.dot(hid, wd_ref[0], preferred_element_type=F32).astype(BF16)


def _expert_call(off_flat, posm, h2, wg, wu, wd, cap, rt, tt, name):
    bsz, t, d = h2.shape
    e, _, f = wg.shape
    nt = t // tt
    grid_spec = pltpu.PrefetchScalarGridSpec(
        num_scalar_prefetch=1,
        grid=(bsz, e),
        in_specs=[
            pl.BlockSpec((1, 1, 1, t), lambda b, k, off: (b, k, 0, 0)),
            pl.BlockSpec((1, t, d), lambda b, k, off: (b, 0, 0), pipeline_mode=pl.Buffered(1)),
            pl.BlockSpec((1, d, f), lambda b, k, off: (k, 0, 0)),
            pl.BlockSpec((1, d, f), lambda b, k, off: (k, 0, 0)),
            pl.BlockSpec((1, f, d), lambda b, k, off: (k, 0, 0)),
        ],
        out_specs=pl.BlockSpec((1, 1, cap, d), lambda b, k, off: (b, k, 0, 0)),
        scratch_shapes=[pltpu.VMEM((cap, d), F32)],
    )
    return pl.pallas_call(
        functools.partial(_expert_kernel, cap=cap, rt=rt, tt=tt, nt=nt),
        grid_spec=grid_spec,
        out_shape=jax.ShapeDtypeStruct((bsz, e, cap, d), BF16),
        compiler_params=_params("arbitrary", "arbitrary"),
        name=name,
    )(off_flat, posm.reshape(bsz, e, 1, t), h2, wg, wu, wd)


def _combine_kernel(off_ref, posm_ref, aff_ref, x1_ref, y_hbm, mod_ref, lng_ref, lnb_ref, o_ref,
                    ybuf, sem, acc_ref, *, rt, tt, nw, ctx_row):
    d = D_MODEL
    b = pl.program_id(0)
    j = pl.program_id(1)
    row = b if ctx_row is None else ctx_row
    shift = rt.bit_length() - 1

    def window(e):
        base = (b * N_EXPERTS + e) * OFF_STRIDE
        lo = off_ref[base + j]
        hi = off_ref[base + j + 1]
        t0 = lax.shift_right_logical(lo, shift)
        n = jnp.where(hi > lo, lax.shift_right_logical(hi - 1, shift) - t0 + 1, 0)
        return t0, n

    def copy(e, wdx, t0):
        src = y_hbm.at[b, e, pl.ds(pl.multiple_of((t0 + wdx) * rt, rt), rt), :]
        return pltpu.make_async_copy(src, ybuf.at[e, wdx], sem.at[e, wdx])

    for e in range(N_EXPERTS):
        t0, n = window(e)
        for wdx in range(nw):
            @pl.when(wdx < n)
            def _(e=e, wdx=wdx, t0=t0):
                copy(e, wdx, t0).start()

    acc_ref[...] = jnp.zeros_like(acc_ref)
    rows = lax.broadcasted_iota(I32, (rt, tt), 0)
    for e in range(N_EXPERTS):
        t0, n = window(e)
        for wdx in range(nw):
            @pl.when(wdx < n)
            def _(e=e, wdx=wdx, t0=t0):
                copy(e, wdx, t0).wait()
                onehot = ((posm_ref[0, e] - (t0 + wdx) * rt) == rows).astype(BF16)
                got = lax.dot_general(onehot, ybuf[e, wdx], (((0,), (0,)), ((), ())),
                                      preferred_element_type=F32)
                acc_ref[...] += aff_ref[0, :, e:e + 1] * got

    g2 = mod_ref[pl.ds(row, 1), 5 * d:6 * d]
    o_ref[0] = _layer_norm(ALPHA * x1_ref[0] + g2 * acc_ref[...], lng_ref[1:2, :], lnb_ref[1:2, :])


def _combine_call(off_flat, posm, aff, x1, y, mod_l, ln_g, ln_b, rt, tt, ctx_row, name):
    bsz, t, d = x1.shape
    e = N_EXPERTS
    cap = y.shape[2]
    nw = min(cap // rt, (tt - 1) // rt + 2)
    grid_spec = pltpu.PrefetchScalarGridSpec(
        num_scalar_prefetch=1,
        grid=(bsz, t // tt),
        in_specs=[
            pl.BlockSpec((1, e, 1, tt), lambda b, j, off: (b, 0, 0, j)),
            pl.BlockSpec((1, tt, e), lambda b, j, off: (b, j, 0)),
            pl.BlockSpec((1, tt, d), lambda b, j, off: (b, j, 0)),
            pl.BlockSpec(memory_space=pl.ANY),
            pl.BlockSpec((8, 6 * d), lambda b, j, off: (0, 0)),
            pl.BlockSpec((2, d), lambda b, j, off: (0, 0)),
            pl.BlockSpec((2, d), lambda b, j, off: (0, 0)),
        ],
        out_specs=pl.BlockSpec((1, tt, d), lambda b, j, off: (b, j, 0)),
        scratch_shapes=[pltpu.VMEM((e, nw, rt, d), BF16), pltpu.SemaphoreType.DMA((e, nw)),
                        pltpu.VMEM((tt, d), F32)],
    )
    return pl.pallas_call(
        functools.partial(_combine_kernel, rt=rt, tt=tt, nw=nw, ctx_row=ctx_row),
        grid_spec=grid_spec,
        out_shape=jax.ShapeDtypeStruct((bsz, t, d), F32),
        compiler_params=_params("arbitrary", "arbitrary"),
        name=name,
    )(off_flat, posm.reshape(bsz, e, 1, t), aff, x1, y, mod_l, ln_g, ln_b)


def _moe(x1, h2, aff, aff_t, mod_l, wg, wu, wd, ln_g, ln_b, ctx_row, tag):
    bsz, t, d = x1.shape
    cap = EC_FACTOR * t // N_EXPERTS
    tt = min(t, 256)
    rt = min(cap, 128)
    posm, off = _route_call(aff_t, cap, tt, "route_" + tag)
    off_flat = off.reshape(-1)
    y = _expert_call(off_flat, posm, h2, wg, wu, wd, cap, rt, tt, "expert_" + tag)
    return _combine_call(off_flat, posm, aff, x1, y, mod_l, ln_g, ln_b, rt, tt, ctx_row, "combine_" + tag)


def kernel(x, c, ctx, c_ctx, w_mod, b_mod, w_in, hgrn_lb_logits, hgrn_norm_g, conv_w, conv_b, lru_wa,
           lru_ba, lru_wx, lru_bx, lru_lambda, w_branch_a, w_branch_b, w_out, ln_g, ln_b, w_router,
           w_gate, w_up, w_down):
    depth = w_in.shape[0]
    bsz = x.shape[0]
    ctx_row = bsz
    assert bsz < 8 and depth == DEPTH

    lb_cum = jnp.cumsum(jax.nn.softmax(hgrn_lb_logits.astype(F32), axis=0), axis=0)
    lbs = lb_cum - lb_cum[0]
    log_lb = jnp.log(lbs)
    log1m_lb = jnp.log1p(-lbs)

    c_all = jnp.zeros((8, x.shape[2]), F32).at[:bsz].set(c).at[ctx_row].set(c_ctx)
    mod = _mod_call(c_all, w_mod, b_mod)

    p_index = P_INDEX
    for l in range(depth):
        need_ctx = l < depth - 1
        mod_l = mod[l]
        w_in_bf = w_in[l].astype(BF16)
        p_lat, lf_lat, kk_lat = _proj_call(x, mod_l, w_in_bf, lbs[l], log_lb[l], log1m_lb[l], None, "proj_lat")
        p_ctx, lf_ctx, kk_ctx = _proj_call(ctx, mod_l, w_in_bf, lbs[l], log_lb[l], log1m_lb[l], ctx_row,
                                           "proj_ctx")

        oc_f, oc_b, ol_f, ol_b = _hgrn_call(p_lat, lf_lat, kk_lat, p_ctx, lf_ctx, kk_ctx)
        h_lat, h_ctx = _lru_call(p_lat, p_ctx, p_index["lx"], p_index["lx"], conv_w[l], conv_b[l],
                                 lru_wa[l], lru_wx[l], lru_ba[l], lru_bx[l], lru_lambda[l])

        wa, wb, wo = (w.astype(BF16) for w in (w_branch_a[l], w_branch_b[l], w_out[l]))
        wg, wu, wd = (w.astype(BF16) for w in (w_gate[l], w_up[l], w_down[l]))
        x1, h2, aff, aff_t = _merge_call(ol_f, ol_b, p_lat, p_index, h_lat, x, mod_l, hgrn_norm_g[l],
                                         wa, wb, wo, ln_g[l], ln_b[l], w_router[l], None, "merge_lat")
        if need_ctx:
            c1, ch2, caff, caff_t = _merge_call(oc_f, oc_b, p_ctx, p_index, h_ctx, ctx, mod_l, hgrn_norm_g[l],
                                                wa, wb, wo, ln_g[l], ln_b[l], w_router[l], ctx_row,
                                                "merge_ctx")
        x = _moe(x1, h2, aff, aff_t, mod_l, wg, wu, wd, ln_g[l], ln_b[l], None, "lat")
        if need_ctx:
            ctx = _moe(c1, ch2, caff, caff_t, mod_l, wg, wu, wd, ln_g[l], ln_b[l], ctx_row, "ctx")
    return x
```

```python
import functools

import jax
import jax.numpy as jnp
from jax import lax
from jax.experimental import pallas as pl
from jax.experimental.pallas import tpu as pltpu

F32 = jnp.float32
BF16 = jnp.bfloat16
I32 = jnp.int32
HIGHEST = lax.Precision.HIGHEST

DEPTH = 2
D_MODEL = 1024
GRID_W = 64
N_HEADS = 8
HEAD_W = 128
CHUNK = 64
N_EXPERTS = 16
EC_FACTOR = 2
LRU_C = 8.0
ALPHA = (2.0 * DEPTH) ** 0.25
LN_EPS = 1e-5
RMS_EPS = 1e-6
EXP_CLAMP = 80.0

V7X_VMEM_LIMIT_BYTES = 56 * 1024 * 1024
OFF_STRIDE = 128
GATHER_WINDOW_TILES = 2
COMBINE_GROUP = 8

G_Q, G_V, G_FF, G_FB, G_OG, G_LX, G_LY, G_MA, G_MB = range(9)


def _params(*sem):
    return pltpu.CompilerParams(dimension_semantics=sem, vmem_limit_bytes=V7X_VMEM_LIMIT_BYTES)


def _sigmoid(x):
    return 1.0 / (1.0 + jnp.exp(-x))


def _layer_norm(x, g, b):
    mu = jnp.mean(x, axis=-1, keepdims=True)
    xc = x - mu
    var = jnp.mean(xc * xc, axis=-1, keepdims=True)
    return xc * lax.rsqrt(var + LN_EPS) * g + b


def _mod_kernel(c_ref, w_ref, b_ref, o_ref):
    c = c_ref[...]
    s = c * _sigmoid(c)
    o_ref[0] = jnp.dot(s, w_ref[0], preferred_element_type=F32, precision=HIGHEST) + b_ref[0]


def _mod_call(c_all, w_mod, b_mod):
    depth, d, n = w_mod.shape
    tn = 1536
    return pl.pallas_call(
        _mod_kernel,
        grid=(depth, n // tn),
        in_specs=[
            pl.BlockSpec((8, d), lambda l, j: (0, 0)),
            pl.BlockSpec((1, d, tn), lambda l, j: (l, 0, j)),
            pl.BlockSpec((1, 1, tn), lambda l, j: (l, 0, j)),
        ],
        out_specs=pl.BlockSpec((1, 8, tn), lambda l, j: (l, 0, j)),
        out_shape=jax.ShapeDtypeStruct((depth, 8, n), F32),
        compiler_params=_params("arbitrary", "arbitrary"),
        name="mod",
    )(c_all, w_mod, b_mod.reshape(depth, 1, n))


def _apply_act(act, p):
    if act == "id":
        return p
    if act == "silu_scale":
        return p * _sigmoid(p) * (HEAD_W ** -0.5)
    if act == "silu":
        return p * _sigmoid(p)
    if act == "sigmoid":
        return _sigmoid(p)
    if act == "gelu":
        return 0.5 * p * (1.0 + jnp.tanh(0.7978845608028654 * (p + 0.044715 * (p * p * p))))
    raise ValueError(act)


P_GROUPS = (G_Q, G_V, G_OG, G_LX, G_LY, G_MA, G_MB)
P_ACTS = ("silu_scale", "id", "silu", "id", "gelu", "sigmoid", "sigmoid")
P_INDEX = {"q": 0, "v": 1, "og": 2, "lx": 3, "ly": 4, "ma": 5, "mb": 6}


def _hgrn_gates(fp, lb, la, l1):
    e = jnp.exp(-jnp.abs(fp))
    r = 1.0 / (1.0 + e)
    log_sig = jnp.minimum(fp, 0.0) - jnp.log(1.0 + e)
    k = (1.0 - lb) * jnp.where(fp >= 0.0, e * r, r)
    b2 = l1 + log_sig
    m = jnp.maximum(la, b2)
    log_f = m + jnp.log(jnp.exp(la - m) + jnp.exp(b2 - m))
    return log_f, k


def _proj_kernel(x_ref, mod_ref, w_ref, lb_ref, la_ref, l1_ref, p_ref, lf_ref, kk_ref, *, ctx_row):
    d = D_MODEL
    b = pl.program_id(0)
    row = b if ctx_row is None else ctx_row
    sh = mod_ref[pl.ds(row, 1), 0:d]
    sc = mod_ref[pl.ds(row, 1), d:2 * d]
    h = (x_ref[0] * (1.0 + sc) + sh).astype(BF16)
    for g, (col, act) in enumerate(zip(P_GROUPS, P_ACTS)):
        p = jnp.dot(h, w_ref[:, col * d:(col + 1) * d], preferred_element_type=F32)
        p_ref[g, 0] = _apply_act(act, p).astype(BF16)
    for dirn, col in enumerate((G_FF, G_FB)):
        fp = jnp.dot(h, w_ref[:, col * d:(col + 1) * d], preferred_element_type=F32)
        log_f, k = _hgrn_gates(fp, lb_ref[dirn:dirn + 1, :], la_ref[dirn:dirn + 1, :],
                               l1_ref[dirn:dirn + 1, :])
        lf_ref[dirn, 0] = log_f
        kk_ref[dirn, 0] = k.astype(BF16)


def _proj_call(x, mod_l, w_in_bf, lb, la, l1, ctx_row, name):
    bsz, t, d = x.shape
    tm = min(t, 512)
    ng = len(P_GROUPS)
    full = lambda shape: pl.BlockSpec(shape, lambda b, i: tuple(0 for _ in shape))
    out = lambda n: pl.BlockSpec((n, 1, tm, d), lambda b, i: (0, b, i, 0))
    return pl.pallas_call(
        functools.partial(_proj_kernel, ctx_row=ctx_row),
        grid=(bsz, t // tm),
        in_specs=[
            pl.BlockSpec((1, tm, d), lambda b, i: (b, i, 0)),
            full((8, 6 * d)),
            pl.BlockSpec(w_in_bf.shape, lambda b, i: (0, 0), pipeline_mode=pl.Buffered(1)),
            full((2, d)), full((2, d)), full((2, d)),
        ],
        out_specs=[out(ng), out(2), out(2)],
        out_shape=[jax.ShapeDtypeStruct((ng, bsz, t, d), BF16), jax.ShapeDtypeStruct((2, bsz, t, d), F32),
                   jax.ShapeDtypeStruct((2, bsz, t, d), BF16)],
        compiler_params=_params("arbitrary", "arbitrary"),
        name=name,
    )(x, mod_l, w_in_bf, lb, la, l1)


def _hgrn_rows(q_ref, v_ref, lf_ref, k_ref, o_ref, row0, n, st_ref, reverse):
    w = HEAD_W
    rows = pl.ds(row0, n * CHUNK)
    log_f = lf_ref[0, 0, rows, :]
    k = k_ref[0, 0, rows, :].astype(F32)
    q = q_ref[0, 0, rows, :]
    v = v_ref[0, 0, rows, :]

    ri = lax.broadcasted_iota(I32, (CHUNK, CHUNK), 0)
    ci = lax.broadcasted_iota(I32, (CHUNK, CHUNK), 1)
    causal = (ci >= ri) if reverse else (ci <= ri)
    tri = causal.astype(BF16)
    lf = jnp.concatenate([log_f[c * CHUNK:(c + 1) * CHUNK, :] for c in range(n)], axis=1)
    hi = lf.astype(BF16)
    rest = lf - hi.astype(F32)
    mid = rest.astype(BF16)
    lo = (rest - mid.astype(F32)).astype(BF16)
    bc = (jnp.dot(tri, hi, preferred_element_type=F32) + jnp.dot(tri, mid, preferred_element_type=F32)
          + jnp.dot(tri, lo, preferred_element_type=F32))

    half = CHUNK // 2
    o_intra, qs_all, upd, decay = [], [], [], []
    for c in range(n):
        sl = slice(c * CHUNK, (c + 1) * CHUNK)
        bcum = bc[:, c * w:(c + 1) * w]
        b_ref = bcum[half:half + 1, :]
        b_end = bcum[0:1, :] if reverse else bcum[CHUNK - 1:CHUNK, :]
        e1 = jnp.exp(jnp.minimum(bcum - b_ref, EXP_CLAMP))
        e2 = jnp.exp(jnp.minimum(b_ref - bcum, EXP_CLAMP))
        qt = q[sl].astype(F32) * e1
        kt = k[sl] * e2
        att = lax.dot_general(qt.astype(BF16), kt.astype(BF16), (((1,), (1,)), ((), ())),
                              preferred_element_type=F32)
        att = jnp.where(causal, att, 0.0).astype(BF16)
        o_intra.append(jnp.dot(att, v[sl], preferred_element_type=F32))
        qs_all.append((qt * jnp.exp(b_ref)).astype(BF16))
        ke = (kt * jnp.exp(b_end - b_ref)).astype(BF16)
        upd.append(lax.dot_general(v[sl], ke, (((0,), (0,)), ((), ())), preferred_element_type=F32))
        decay.append(jnp.exp(b_end))

    st = st_ref[...]
    for c in (range(n - 1, -1, -1) if reverse else range(n)):
        o = o_intra[c] + lax.dot_general(qs_all[c], st.astype(BF16), (((1,), (1,)), ((), ())),
                                         preferred_element_type=F32)
        o_ref[0, 0, pl.ds(row0 + c * CHUNK, CHUNK), :] = o
        st = decay[c] * st + upd[c]
    st_ref[...] = st


HGRN_SUB = 8


def _hgrn_kernel(qc_ref, vc_ref, lcf_ref, lcb_ref, kcf_ref, kcb_ref,
                 qf_ref, vf_ref, lff_ref, kf_ref, qb_ref, vb_ref, lfb_ref, kb_ref,
                 ocf_ref, ocb_ref, of_ref, ob_ref, sf_ref, sb_ref, *, n_ctx_chunks, n_lat_chunks):
    s = pl.program_id(2)

    def run(n_chunks, fwd, bwd):
        sub = min(n_chunks, HGRN_SUB)
        n_sub = n_chunks // sub

        def body(i, carry):
            rf = pl.multiple_of(i * (sub * CHUNK), sub * CHUNK)
            rb = pl.multiple_of((n_sub - 1 - i) * (sub * CHUNK), sub * CHUNK)
            _hgrn_rows(*fwd, rf, sub, sf_ref, False)
            _hgrn_rows(*bwd, rb, sub, sb_ref, True)
            return carry
        lax.fori_loop(0, n_sub, body, 0)

    @pl.when(s == 0)
    def _():
        sf_ref[...] = jnp.zeros_like(sf_ref)
        sb_ref[...] = jnp.zeros_like(sb_ref)
        run(n_ctx_chunks, (qc_ref, vc_ref, lcf_ref, kcf_ref, ocf_ref), (qc_ref, vc_ref, lcb_ref, kcb_ref, ocb_ref))

    run(n_lat_chunks, (qf_ref, vf_ref, lff_ref, kf_ref, of_ref), (qb_ref, vb_ref, lfb_ref, kb_ref, ob_ref))


def _hgrn_call(p_lat, lf_lat, kk_lat, p_ctx, lf_ctx, kk_ctx):
    _, bsz, t, d = p_lat.shape
    tc = p_ctx.shape[2]
    w = HEAD_W
    blk = min(t, 1024)
    ns = t // blk
    lat = lambda g, rev: pl.BlockSpec(
        (1, 1, blk, w), (lambda b, h, s: (g, b, ns - 1 - s, h)) if rev else (lambda b, h, s: (g, b, s, h)))
    ctx = lambda g: pl.BlockSpec((1, 1, tc, w), lambda b, h, s: (g, b, 0, h))
    return pl.pallas_call(
        functools.partial(_hgrn_kernel, n_ctx_chunks=tc // CHUNK, n_lat_chunks=blk // CHUNK),
        grid=(bsz, N_HEADS, ns),
        in_specs=[ctx(0), ctx(1), ctx(0), ctx(1), ctx(0), ctx(1),
                  lat(0, False), lat(1, False), lat(0, False), lat(0, False),
                  lat(0, True), lat(1, True), lat(1, True), lat(1, True)],
        out_specs=[ctx(0), ctx(0), lat(0, False), lat(0, True)],
        out_shape=[jax.ShapeDtypeStruct((1, bsz, tc, d), F32), jax.ShapeDtypeStruct((1, bsz, tc, d), F32),
                   jax.ShapeDtypeStruct((1, bsz, t, d), F32), jax.ShapeDtypeStruct((1, bsz, t, d), F32)],
        scratch_shapes=[pltpu.VMEM((w, w), F32), pltpu.VMEM((w, w), F32)],
        compiler_params=_params("arbitrary", "arbitrary", "arbitrary"),
        name="hgrn",
    )(p_ctx, p_ctx, lf_ctx, lf_ctx, kk_ctx, kk_ctx,
      p_lat, p_lat, lf_lat, kk_lat, p_lat, p_lat, lf_lat, kk_lat)


def _lru_gates(xc, wa_ref, wx_ref, ba_ref, bx_ref, sp_ref, dirn):
    xb = xc.astype(BF16)
    r = _sigmoid(jnp.dot(xb, wa_ref[dirn, 0].astype(BF16), preferred_element_type=F32)
                 + ba_ref[dirn:dirn + 1, :])
    i = _sigmoid(jnp.dot(xb, wx_ref[dirn, 0].astype(BF16), preferred_element_type=F32)
                 + bx_ref[dirn:dirn + 1, :])
    log_a = -LRU_C * r * sp_ref[dirn:dirn + 1, :]
    a = jnp.exp(log_a)
    u = jnp.sqrt(1.0 - a * a) * (i * xc)
    return a, u


def _lru_kernel(xl_ref, xc_ref, cw_ref, cb_ref, wa_ref, wx_ref, ba_ref, bx_ref, lam_ref,
                hl_ref, hc_ref,
                xpad, cpad, a_f, u_f, a_b, u_b, ca_f, cu_f, ca_b, cu_b,
                ends_f, prods_f, carry_f, ends_b, prods_b, carry_b, sp_ref,
                *, t, tc):
    w = HEAD_W
    gw = GRID_W
    rows = t // gw
    lam = lam_ref[...]
    neg = -lam
    sp_ref[...] = jnp.maximum(neg, 0.0) + jnp.log(1.0 + jnp.exp(-jnp.abs(neg)))
    w0, w1, w2, w3 = (cw_ref[k:k + 1, :] for k in range(4))
    cb = cb_ref[...]

    cpad[...] = jnp.zeros_like(cpad)
    cpad[8:8 + tc, :] = xc_ref[0, 0].astype(F32)
    xcc = (w0 * cpad[6:6 + tc, :] + w1 * cpad[7:7 + tc, :] + w2 * cpad[8:8 + tc, :]
           + w3 * cpad[9:9 + tc, :] + cb)
    ca_f[...], cu_f[...] = _lru_gates(xcc, wa_ref, wx_ref, ba_ref, bx_ref, sp_ref, 0)
    ca_b[...], cu_b[...] = _lru_gates(xcc, wa_ref, wx_ref, ba_ref, bx_ref, sp_ref, 1)

    def ctx_step(i, hs):
        hf, hb = hs
        p = tc - 1 - i
        hf = ca_f[pl.ds(i, 1), :] * hf + cu_f[pl.ds(i, 1), :]
        hb = ca_b[pl.ds(p, 1), :] * hb + cu_b[pl.ds(p, 1), :]
        cu_f[pl.ds(i, 1), :] = hf
        cu_b[pl.ds(p, 1), :] = hb
        return hf, hb
    zero_row = jnp.zeros((1, w), F32)
    s0_f, s0_b = lax.fori_loop(0, tc, ctx_step, (zero_row, zero_row))
    hc_ref[0] = (cu_f[...] + cu_b[...]).astype(hc_ref.dtype)

    xpad[2 * gw:2 * gw + t, :] = xl_ref[0, 0].astype(F32)
    col = lax.broadcasted_iota(I32, (gw, w), 0)
    body0 = 2 * gw
    xpad[gw:2 * gw, :] = jnp.where(col == 0, 0.0, xpad[pl.ds(body0 + (rows - 1) * gw - 1, gw), :])
    xpad[0:gw, :] = jnp.where(col == 0, 0.0, xpad[pl.ds(body0 + (rows - 2) * gw - 1, gw), :])
    xpad[body0 + t:body0 + t + gw, :] = jnp.where(col == gw - 1, 0.0, xpad[pl.ds(body0 + 1, gw), :])

    gate_rows = min(t, 512)

    def gate_body(c, carry_):
        base = pl.multiple_of(c * gate_rows, gate_rows)
        xcv = (w0 * xpad[pl.ds(base, gate_rows), :] + w1 * xpad[pl.ds(base + gw, gate_rows), :]
               + w2 * xpad[pl.ds(base + 2 * gw, gate_rows), :]
               + w3 * xpad[pl.ds(base + 3 * gw, gate_rows), :] + cb)
        sl = pl.ds(base, gate_rows)
        a_f[sl, :], u_f[sl, :] = _lru_gates(xcv, wa_ref, wx_ref, ba_ref, bx_ref, sp_ref, 0)
        a_b[sl, :], u_b[sl, :] = _lru_gates(xcv, wa_ref, wx_ref, ba_ref, bx_ref, sp_ref, 1)
        return carry_
    lax.fori_loop(0, t // gate_rows, gate_body, 0)

    def slab(r):
        return pl.ds(pl.multiple_of(r * gw, gw), gw)

    def p1(i, c):
        hf, pf, hb, pb = c
        sf, sb = slab(i), slab(rows - 1 - i)
        af, ab = a_f[sf, :], a_b[sb, :]
        return af * hf + u_f[sf, :], af * pf, ab * hb + u_b[sb, :], ab * pb
    zeros, ones = jnp.zeros((gw, w), F32), jnp.ones((gw, w), F32)
    ends_f[...], prods_f[...], ends_b[...], prods_b[...] = lax.fori_loop(0, rows, p1, (zeros, ones, zeros, ones))

    def chain(i, c):
        cf, cb = c
        jf, jb = pl.ds(i, 1), pl.ds(gw - 1 - i, 1)
        carry_f[jf, :] = cf
        carry_b[jb, :] = cb
        return prods_f[jf, :] * cf + ends_f[jf, :], prods_b[jb, :] * cb + ends_b[jb, :]
    lax.fori_loop(0, gw, chain, (s0_f, s0_b))

    def p2(i, c):
        hf, hb = c
        sf, sb = slab(i), slab(rows - 1 - i)
        hf = a_f[sf, :] * hf + u_f[sf, :]
        hb = a_b[sb, :] * hb + u_b[sb, :]
        u_f[sf, :] = hf
        u_b[sb, :] = hb
        return hf, hb
    lax.fori_loop(0, rows, p2, (carry_f[...], carry_b[...]))
    hl_ref[0] = (u_f[...] + u_b[...]).astype(hl_ref.dtype)


def _lru_call(p_lat, p_ctx, g_lat, g_ctx, conv_w, conv_b, wa, wx, ba, bx, lam):
    _, bsz, t, d = p_lat.shape
    tc = p_ctx.shape[2]
    w = HEAD_W
    nb = d // w
    gw = GRID_W
    vec = lambda n: pl.BlockSpec((n, w), lambda b, k: (0, k))
    mat = pl.BlockSpec((2, 1, w, w), lambda b, k: (0, k, 0, 0))
    return pl.pallas_call(
        functools.partial(_lru_kernel, t=t, tc=tc),
        grid=(bsz, nb),
        in_specs=[
            pl.BlockSpec((1, 1, t, w), lambda b, k: (g_lat, b, 0, k)),
            pl.BlockSpec((1, 1, tc, w), lambda b, k: (g_ctx, b, 0, k)),
            vec(4), vec(1), mat, mat, vec(2), vec(2), vec(2),
        ],
        out_specs=[pl.BlockSpec((1, t, w), lambda b, k: (b, 0, k)),
                   pl.BlockSpec((1, tc, w), lambda b, k: (b, 0, k))],
        out_shape=[jax.ShapeDtypeStruct((bsz, t, d), BF16), jax.ShapeDtypeStruct((bsz, tc, d), BF16)],
        scratch_shapes=[
            pltpu.VMEM((t + 3 * gw, w), F32), pltpu.VMEM((tc + 16, w), F32),
            pltpu.VMEM((t, w), F32), pltpu.VMEM((t, w), F32), pltpu.VMEM((t, w), F32), pltpu.VMEM((t, w), F32),
            pltpu.VMEM((tc, w), F32), pltpu.VMEM((tc, w), F32), pltpu.VMEM((tc, w), F32), pltpu.VMEM((tc, w), F32),
            pltpu.VMEM((gw, w), F32), pltpu.VMEM((gw, w), F32), pltpu.VMEM((gw, w), F32),
            pltpu.VMEM((gw, w), F32), pltpu.VMEM((gw, w), F32), pltpu.VMEM((gw, w), F32),
            pltpu.VMEM((2, w), F32),
        ],
        compiler_params=_params("arbitrary", "arbitrary"),
        name="rglru",
    )(p_lat, p_ctx, conv_w, conv_b.reshape(1, d), wa, wx, ba, bx, lam)


def _merge_kernel(of_ref, ob_ref, og_ref, hl_ref, ly_ref, ma_ref, mb_ref, x_ref, mod_ref, gn_ref,
                  wa_ref, wb_ref, wo_ref, lng_ref, lnb_ref, wrt_ref,
                  x1_ref, h2_ref, afft_ref, *, ctx_row):
    d = D_MODEL
    b = pl.program_id(0)
    row = b if ctx_row is None else ctx_row
    o = of_ref[0, 0] + ob_ref[0, 0]
    gn = gn_ref[...]
    parts = []
    for h in range(N_HEADS):
        oh = o[:, h * HEAD_W:(h + 1) * HEAD_W]
        ms = jnp.mean(oh * oh, axis=-1, keepdims=True)
        parts.append(oh * lax.rsqrt(ms + RMS_EPS) * gn)
    o_a = (jnp.concatenate(parts, axis=1) * og_ref[0, 0].astype(F32)).astype(BF16)
    y_a = jnp.dot(o_a, wa_ref[...], preferred_element_type=F32)
    y_b = jnp.dot((hl_ref[0].astype(F32) * ly_ref[0, 0].astype(F32)).astype(BF16), wb_ref[...],
                  preferred_element_type=F32)
    z = ma_ref[0, 0].astype(F32) * y_a + mb_ref[0, 0].astype(F32) * y_b
    y = jnp.dot(z.astype(BF16), wo_ref[...], preferred_element_type=F32)
    g1 = mod_ref[pl.ds(row, 1), 2 * d:3 * d]
    x1 = _layer_norm(ALPHA * x_ref[0] + g1 * y, lng_ref[0:1, :], lnb_ref[0:1, :])
    x1_ref[0] = x1
    sh2 = mod_ref[pl.ds(row, 1), 3 * d:4 * d]
    sc2 = mod_ref[pl.ds(row, 1), 4 * d:5 * d]
    h2 = x1 * (1.0 + sc2) + sh2
    h2_ref[0] = h2.astype(BF16)
    logits_t = lax.dot_general(wrt_ref[...], h2, (((1,), (1,)), ((), ())),
                               preferred_element_type=F32, precision=HIGHEST)
    pt = jnp.exp(logits_t - jnp.max(logits_t, axis=0, keepdims=True))
    afft_ref[0] = pt / jnp.sum(pt, axis=0, keepdims=True)


def _merge_call(o_f, o_b, p, groups, h_lru, x, mod_l, gn, wa, wb, wo, ln_g, ln_b, w_router, ctx_row, name):
    bsz, t, d = x.shape
    tm = min(t, 512)
    e = N_EXPERTS
    tile = lambda g: pl.BlockSpec((1, 1, tm, d), lambda b, i: (g, b, i, 0))
    tok = pl.BlockSpec((1, tm, d), lambda b, i: (b, i, 0))
    full = lambda shape: pl.BlockSpec(shape, lambda b, i: tuple(0 for _ in shape))
    return pl.pallas_call(
        functools.partial(_merge_kernel, ctx_row=ctx_row),
        grid=(bsz, t // tm),
        in_specs=[tile(0), tile(0), tile(groups["og"]), tok, tile(groups["ly"]), tile(groups["ma"]),
                  tile(groups["mb"]), tok, full((8, 6 * d)), full((1, HEAD_W)),
                  full((d, d)), full((d, d)), full((d, d)), full((2, d)), full((2, d)),
                  full((e, d))],
        out_specs=[tok, tok, pl.BlockSpec((1, e, tm), lambda b, i: (b, 0, i))],
        out_shape=[jax.ShapeDtypeStruct((bsz, t, d), F32), jax.ShapeDtypeStruct((bsz, t, d), BF16),
                   jax.ShapeDtypeStruct((bsz, e, t), F32)],
        compiler_params=_params("arbitrary", "arbitrary"),
        name=name,
    )(o_f, o_b, p, h_lru, p, p, p, x, mod_l, gn.reshape(1, HEAD_W), wa, wb, wo, ln_g, ln_b,
      w_router.T)


def _route_kernel(a_ref, posm_ref, off_ref, *, cap, t, tt):
    e = N_EXPERTS
    u = lax.bitcast_convert_type(a_ref[0], I32)
    thr = jnp.zeros((e, 1), I32)
    for bit in range(30, -1, -1):
        cand = thr | (1 << bit)
        cnt = jnp.sum((u >= cand).astype(F32), axis=1, keepdims=True)
        thr = jnp.where(cnt >= cap, cand, thr)
    gt = u > thr
    eq = u == thr
    need = cap - jnp.sum(gt.astype(F32), axis=1, keepdims=True)
    ri = lax.broadcasted_iota(I32, (tt, tt), 0)
    ci = lax.broadcasted_iota(I32, (tt, tt), 1)
    before = (ri < ci).astype(BF16)
    lane = lax.broadcasted_iota(I32, (e, OFF_STRIDE), 1)
    offs = jnp.zeros((e, OFF_STRIDE), F32)
    n_eq = jnp.zeros((e, 1), F32)
    n_sel = jnp.zeros((e, 1), F32)
    for j in range(t // tt):
        sl = slice(j * tt, (j + 1) * tt)
        eq_j = eq[:, sl]
        rank_eq = jnp.dot(eq_j.astype(BF16), before, preferred_element_type=F32) + n_eq
        sel_j = gt[:, sl] | (eq_j & (rank_eq < need))
        sel_f = sel_j.astype(F32)
        pos = jnp.dot(sel_j.astype(BF16), before, preferred_element_type=F32) + n_sel
        posm_ref[0, :, sl] = jnp.where(sel_j, pos.astype(I32), -1)
        offs = jnp.where(lane == j, n_sel, offs)
        n_eq = n_eq + jnp.sum(eq_j.astype(F32), axis=1, keepdims=True)
        n_sel = n_sel + jnp.sum(sel_f, axis=1, keepdims=True)
    offs = jnp.where(lane == t // tt, n_sel, offs)
    off_ref[0] = offs.astype(I32)


def _route_call(aff_t, cap, tt, name):
    bsz, e, t = aff_t.shape
    return pl.pallas_call(
        functools.partial(_route_kernel, cap=cap, t=t, tt=tt),
        grid=(bsz,),
        in_specs=[pl.BlockSpec((1, e, t), lambda b: (b, 0, 0))],
        out_specs=[pl.BlockSpec((1, e, t), lambda b: (b, 0, 0)),
                   pl.BlockSpec((1, e, OFF_STRIDE), lambda b: (b, 0, 0))],
        out_shape=[jax.ShapeDtypeStruct((bsz, e, t), I32), jax.ShapeDtypeStruct((bsz, e, OFF_STRIDE), I32)],
        compiler_params=_params("arbitrary"),
        name=name,
    )(aff_t)


def _expert_kernel(off_ref, posm_ref, h_ref, wg_ref, wu_ref, wd_ref, y_ref, x_ref, *, cap, rt, tt, nt):
    b = pl.program_id(0)
    e = pl.program_id(1)
    base = (b * N_EXPERTS + e) * OFF_STRIDE
    x_ref[...] = jnp.zeros_like(x_ref)
    kt = min(nt, GATHER_WINDOW_TILES)
    kw = kt * tt
    rows = lax.broadcasted_iota(I32, (rt, kw), 0)
    lane = lax.broadcasted_iota(I32, (rt, kw), 1)
    n_rt = cap // rt

    def spans(j, c):
        lo = off_ref[base + j]
        hi = off_ref[base + j + 1]
        return (tuple(c[i] + (hi <= i * rt).astype(I32) for i in range(n_rt))
                + tuple(c[n_rt + i] + (lo < (i + 1) * rt).astype(I32) for i in range(n_rt)))
    counts = lax.fori_loop(0, nt, spans, (jnp.int32(0),) * (2 * n_rt))

    for i in range(n_rt):
        first = counts[i]
        n_win = (counts[n_rt + i] - 1 - first) // kt + 1

        def body(wdx, carry, i=i, first=first):
            want = first + wdx * kt
            start = jnp.minimum(want, nt - kt)
            tok = pl.ds(pl.multiple_of(start * tt, tt), kw)
            hit = ((posm_ref[0, 0, :, tok] - i * rt) == rows) & (lane >= (want - start) * tt)
            x_ref[i * rt:(i + 1) * rt, :] += jnp.dot(hit.astype(BF16), h_ref[0, tok, :],
                                                     preferred_element_type=F32)
            return carry
        lax.fori_loop(0, n_win, body, 0)
    ft = min(cap, 256)
    for r in range(cap // ft):
        xb = x_ref[r * ft:(r + 1) * ft, :].astype(BF16)
        g = jnp.dot(xb, wg_ref[0], preferred_element_type=F32)
        u = jnp.dot(xb, wu_ref[0], preferred_element_type=F32)
        hid = (g * _sigmoid(g) * u).astype(BF16)
        y_ref[0, 0, r * ft:(r + 1) * ft, :] = jnp.dot(hid, wd_ref[0], preferred_element_type=F32).astype(BF16)


def _expert_call(off_flat, posm, h2, wg, wu, wd, cap, rt, tt, name):
    bsz, t, d = h2.shape
    e, _, f = wg.shape
    nt = t // tt
    grid_spec = pltpu.PrefetchScalarGridSpec(
        num_scalar_prefetch=1,
        grid=(bsz, e),
        in_specs=[
            pl.BlockSpec((1, 1, 1, t), lambda b, k, off: (b, k, 0, 0)),
            pl.BlockSpec((1, t, d), lambda b, k, off: (b, 0, 0), pipeline_mode=pl.Buffered(1)),
            pl.BlockSpec((1, d, f), lambda b, k, off: (k, 0, 0)),
            pl.BlockSpec((1, d, f), lambda b, k, off: (k, 0, 0)),
            pl.BlockSpec((1, f, d), lambda b, k, off: (k, 0, 0)),
        ],
        out_specs=pl.BlockSpec((1, 1, cap, d), lambda b, k, off: (b, k, 0, 0)),
        scratch_shapes=[pltpu.VMEM((cap, d), F32)],
    )
    return pl.pallas_call(
        functools.partial(_expert_kernel, cap=cap, rt=rt, tt=tt, nt=nt),
        grid_spec=grid_spec,
        out_shape=jax.ShapeDtypeStruct((bsz, e, cap, d), BF16),
        compiler_params=_params("arbitrary", "arbitrary"),
        name=name,
    )(off_flat, posm.reshape(bsz, e, 1, t), h2, wg, wu, wd)


def _combine_kernel(off_ref, posm_ref, afft_ref, x1_ref, y_hbm, mod_ref, lng_ref, lnb_ref, o_ref,
                    ybuf, pbuf, sem, acc_ref, *, rt, tt, nw, ctx_row):
    d = D_MODEL
    b = pl.program_id(0)
    j = pl.program_id(1)
    row = b if ctx_row is None else ctx_row
    shift = rt.bit_length() - 1

    @pl.when((b == 0) & (j == 0))
    def _():
        ybuf[...] = jnp.zeros_like(ybuf)

    def window(e):
        base = (b * N_EXPERTS + e) * OFF_STRIDE
        lo = off_ref[base + j]
        hi = off_ref[base + j + 1]
        t0 = lax.shift_right_logical(lo, shift)
        n = jnp.where(hi > lo, lax.shift_right_logical(hi - 1, shift) - t0 + 1, 0)
        return t0, n

    windows, slot0 = [], []
    n_used = jnp.int32(0)
    for e in range(N_EXPERTS):
        windows.append(window(e))
        slot0.append(n_used)
        n_used = n_used + windows[e][1]
    n_groups = lax.shift_right_logical(n_used + (COMBINE_GROUP - 1), COMBINE_GROUP.bit_length() - 1)

    def slot_rows(e, wdx):
        return pl.ds(pl.multiple_of((slot0[e] + wdx) * rt, rt), rt)

    def copy(e, wdx):
        src = y_hbm.at[b, e, pl.ds(pl.multiple_of((windows[e][0] + wdx) * rt, rt), rt), :]
        return pltpu.make_async_copy(src, ybuf.at[slot_rows(e, wdx), :], sem.at[e, wdx])

    def for_used_windows(fn):
        for e in range(N_EXPERTS):
            for wdx in range(nw):
                pl.when(wdx < windows[e][1])(functools.partial(fn, e, wdx))

    for_used_windows(lambda e, wdx: copy(e, wdx).start())

    def clear(s, carry):
        pbuf[pl.ds(pl.multiple_of(s * rt, rt), rt), :] = jnp.zeros((rt, tt), BF16)
        return carry
    lax.fori_loop(n_used, n_groups * COMBINE_GROUP, clear, 0)

    rows = lax.broadcasted_iota(I32, (rt, tt), 0)

    def weights(e, wdx):
        pos = posm_ref[0, e]
        gate = afft_ref[0, e:e + 1, :]
        hit = (pos - (windows[e][0] + wdx) * rt) == rows
        pbuf[slot_rows(e, wdx), :] = jnp.where(hit, gate, 0.0).astype(BF16)
    for_used_windows(weights)
    for_used_windows(lambda e, wdx: copy(e, wdx).wait())

    acc_ref[...] = jnp.zeros_like(acc_ref)
    gk = COMBINE_GROUP * rt
    for g in range(pl.cdiv(N_EXPERTS * nw, COMBINE_GROUP)):
        @pl.when(g < n_groups)
        def _(g=g):
            acc_ref[...] += lax.dot_general(pbuf[g * gk:(g + 1) * gk, :], ybuf[g * gk:(g + 1) * gk, :],
                                            (((0,), (0,)), ((), ())), preferred_element_type=F32)
    g2 = mod_ref[pl.ds(row, 1), 5 * d:6 * d]
    o_ref[0] = _layer_norm(ALPHA * x1_ref[0] + g2 * acc_ref[...], lng_ref[1:2, :], lnb_ref[1:2, :])


def _combine_call(off_flat, posm, aff_t, x1, y, mod_l, ln_g, ln_b, rt, tt, ctx_row, name):
    bsz, t, d = x1.shape
    e = N_EXPERTS
    cap = y.shape[2]
    nw = min(cap // rt, (tt - 1) // rt + 2)
    n_slots = pl.cdiv(e * nw, COMBINE_GROUP) * COMBINE_GROUP
    grid_spec = pltpu.PrefetchScalarGridSpec(
        num_scalar_prefetch=1,
        grid=(bsz, t // tt),
        in_specs=[
            pl.BlockSpec((1, e, 1, tt), lambda b, j, off: (b, 0, 0, j)),
            pl.BlockSpec((1, e, tt), lambda b, j, off: (b, 0, j)),
            pl.BlockSpec((1, tt, d), lambda b, j, off: (b, j, 0)),
            pl.BlockSpec(memory_space=pl.ANY),
            pl.BlockSpec((8, 6 * d), lambda b, j, off: (0, 0)),
            pl.BlockSpec((2, d), lambda b, j, off: (0, 0)),
            pl.BlockSpec((2, d), lambda b, j, off: (0, 0)),
        ],
        out_specs=pl.BlockSpec((1, tt, d), lambda b, j, off: (b, j, 0)),
        scratch_shapes=[pltpu.VMEM((n_slots * rt, d), BF16), pltpu.VMEM((n_slots * rt, tt), BF16),
                        pltpu.SemaphoreType.DMA((e, nw)), pltpu.VMEM((tt, d), F32)],
    )
    return pl.pallas_call(
        functools.partial(_combine_kernel, rt=rt, tt=tt, nw=nw, ctx_row=ctx_row),
        grid_spec=grid_spec,
        out_shape=jax.ShapeDtypeStruct((bsz, t, d), F32),
        compiler_params=_params("arbitrary", "arbitrary"),
        name=name,
    )(off_flat, posm.reshape(bsz, e, 1, t), aff_t, x1, y, mod_l, ln_g, ln_b)


def _moe(x1, h2, aff_t, mod_l, wg, wu, wd, ln_g, ln_b, ctx_row, tag):
    bsz, t, d = x1.shape
    cap = EC_FACTOR * t // N_EXPERTS
    tt = min(t, 256)
    rt = min(cap, 128)
    posm, off = _route_call(aff_t, cap, tt, "route_" + tag)
    off_flat = off.reshape(-1)
    y = _expert_call(off_flat, posm, h2, wg, wu, wd, cap, rt, tt, "expert_" + tag)
    return _combine_call(off_flat, posm, aff_t, x1, y, mod_l, ln_g, ln_b, rt, tt, ctx_row, "combine_" + tag)


def kernel(x, c, ctx, c_ctx, w_mod, b_mod, w_in, hgrn_lb_logits, hgrn_norm_g, conv_w, conv_b, lru_wa,
           lru_ba, lru_wx, lru_bx, lru_lambda, w_branch_a, w_branch_b, w_out, ln_g, ln_b, w_router,
           w_gate, w_up, w_down):
    depth = w_in.shape[0]
    bsz = x.shape[0]
    ctx_row = bsz
    assert bsz < 8 and depth == DEPTH

    lb_cum = jnp.cumsum(jax.nn.softmax(hgrn_lb_logits.astype(F32), axis=0), axis=0)
    lbs = lb_cum - lb_cum[0]
    log_lb = jnp.log(lbs)
    log1m_lb = jnp.log1p(-lbs)

    c_all = jnp.zeros((8, x.shape[2]), F32).at[:bsz].set(c).at[ctx_row].set(c_ctx)
    mod = _mod_call(c_all, w_mod, b_mod)

    p_index = P_INDEX
    for l in range(depth):
        need_ctx = l < depth - 1
        mod_l = mod[l]
        w_in_bf = w_in[l].astype(BF16)
        p_lat, lf_lat, kk_lat = _proj_call(x, mod_l, w_in_bf, lbs[l], log_lb[l], log1m_lb[l], None, "proj_lat")
        p_ctx, lf_ctx, kk_ctx = _proj_call(ctx, mod_l, w_in_bf, lbs[l], log_lb[l], log1m_lb[l], ctx_row,
                                           "proj_ctx")

        oc_f, oc_b, ol_f, ol_b = _hgrn_call(p_lat, lf_lat, kk_lat, p_ctx, lf_ctx, kk_ctx)
        h_lat, h_ctx = _lru_call(p_lat, p_ctx, p_index["lx"], p_index["lx"], conv_w[l], conv_b[l],
                                 lru_wa[l], lru_wx[l], lru_ba[l], lru_bx[l], lru_lambda[l])

        wa, wb, wo = (w.astype(BF16) for w in (w_branch_a[l], w_branch_b[l], w_out[l]))
        wg, wu, wd = (w.astype(BF16) for w in (w_gate[l], w_up[l], w_down[l]))
        x1, h2, aff_t = _merge_call(ol_f, ol_b, p_lat, p_index, h_lat, x, mod_l, hgrn_norm_g[l],
                                         wa, wb, wo, ln_g[l], ln_b[l], w_router[l], None, "merge_lat")
        if need_ctx:
            c1, ch2, caff_t = _merge_call(oc_f, oc_b, p_ctx, p_index, h_ctx, ctx, mod_l, hgrn_norm_g[l],
                                                wa, wb, wo, ln_g[l], ln_b[l], w_router[l], ctx_row,
                                                "merge_ctx")
        x = _moe(x1, h2, aff_t, mod_l, wg, wu, wd, ln_g[l], ln_b[l], None, "lat")
        if need_ctx:
            ctx = _moe(c1, ch2, caff_t, mod_l, wg, wu, wd, ln_g[l], ln_b[l], ctx_row, "ctx")
    return x
```

```python
import functools

import jax
import jax.numpy as jnp
from jax import lax
from jax.experimental import pallas as pl
from jax.experimental.pallas import tpu as pltpu

F32 = jnp.float32
BF16 = jnp.bfloat16
I32 = jnp.int32
HIGHEST = lax.Precision.HIGHEST

DEPTH = 2
D_MODEL = 1024
GRID_W = 64
N_HEADS = 8
HEAD_W = 128
CHUNK = 64
N_EXPERTS = 16
EC_FACTOR = 2
LRU_C = 8.0
ALPHA = (2.0 * DEPTH) ** 0.25
LN_EPS = 1e-5
RMS_EPS = 1e-6
LN2 = 0.6931471805599453
LOG2E = 1.4426950408889634
EXP_CLAMP = 80.0

V7X_VMEM_LIMIT_BYTES = 56 * 1024 * 1024
OFF_STRIDE = 128
GATHER_WINDOW_TILES = 2
COMBINE_GROUP = 8

G_Q, G_V, G_FF, G_FB, G_OG, G_LX, G_LY, G_MA, G_MB = range(9)


def _params(*sem):
    return pltpu.CompilerParams(dimension_semantics=sem, vmem_limit_bytes=V7X_VMEM_LIMIT_BYTES)


def _sigmoid(x):
    return 0.5 * jnp.tanh(0.5 * x) + 0.5


def _layer_norm(x, g, b):
    mu = jnp.mean(x, axis=-1, keepdims=True)
    xc = x - mu
    var = jnp.mean(xc * xc, axis=-1, keepdims=True)
    return xc * lax.rsqrt(var + LN_EPS) * g + b


def _mod_kernel(c_ref, w_ref, b_ref, o_ref):
    c = c_ref[...]
    s = c * _sigmoid(c)
    o_ref[0] = jnp.dot(s, w_ref[0], preferred_element_type=F32, precision=HIGHEST) + b_ref[0]


def _mod_call(c_all, w_mod, b_mod):
    depth, d, n = w_mod.shape
    tn = 1536
    return pl.pallas_call(
        _mod_kernel,
        grid=(depth, n // tn),
        in_specs=[
            pl.BlockSpec((8, d), lambda l, j: (0, 0)),
            pl.BlockSpec((1, d, tn), lambda l, j: (l, 0, j)),
            pl.BlockSpec((1, 1, tn), lambda l, j: (l, 0, j)),
        ],
        out_specs=pl.BlockSpec((1, 8, tn), lambda l, j: (l, 0, j)),
        out_shape=jax.ShapeDtypeStruct((depth, 8, n), F32),
        compiler_params=_params("arbitrary", "arbitrary"),
        name="mod",
    )(c_all, w_mod, b_mod.reshape(depth, 1, n))


def _apply_act(act, p):
    if act == "id":
        return p
    if act == "silu_scale":
        return p * _sigmoid(p) * (HEAD_W ** -0.5)
    if act == "silu":
        return p * _sigmoid(p)
    if act == "sigmoid":
        return _sigmoid(p)
    if act == "gelu":
        return 0.5 * p * (1.0 + jnp.tanh(0.7978845608028654 * (p + 0.044715 * (p * p * p))))
    raise ValueError(act)


P_GROUPS = (G_Q, G_V, G_OG, G_LX, G_LY, G_MA, G_MB)
P_ACTS = ("silu_scale", "id", "silu", "id", "gelu", "sigmoid", "sigmoid")
P_INDEX = {"q": 0, "v": 1, "og": 2, "lx": 3, "ly": 4, "ma": 5, "mb": 6}


def _hgrn_gates(fp, lb, l1):
    e = jnp.exp(-jnp.abs(fp))
    r = 1.0 / (1.0 + e)
    sig = jnp.where(fp >= 0.0, r, e * r)
    k = (1.0 - lb) * jnp.where(fp >= 0.0, e * r, r)
    log_f = jnp.maximum(jnp.log(lb + (1.0 - lb) * sig), l1 + jnp.minimum(fp, 0.0) - LN2)
    return log_f, k


def _proj_kernel(x_ref, mod_ref, w_ref, lb_ref, l1_ref, p_ref, lf_ref, kk_ref, *, ctx_row):
    d = D_MODEL
    b = pl.program_id(0)
    row = b if ctx_row is None else ctx_row
    sh = mod_ref[pl.ds(row, 1), 0:d]
    sc = mod_ref[pl.ds(row, 1), d:2 * d]
    h = (x_ref[0] * (1.0 + sc) + sh).astype(BF16)

    def group(col):
        return jnp.dot(h, w_ref[:, col * d:(col + 1) * d], preferred_element_type=F32)

    def plain(name):
        g = P_INDEX[name]
        p_ref[g, 0] = _apply_act(P_ACTS[g], group(P_GROUPS[g])).astype(BF16)

    def gate(dirn, col):
        log_f, k = _hgrn_gates(group(col), lb_ref[dirn:dirn + 1, :], l1_ref[dirn:dirn + 1, :])
        lf_ref[dirn, 0] = log_f
        kk_ref[dirn, 0] = k.astype(BF16)

    gate(0, G_FF)
    plain("v")
    gate(1, G_FB)
    plain("lx")
    plain("q")
    plain("ly")
    plain("og")
    plain("ma")
    plain("mb")


def _proj_call(x, mod_l, w_in_bf, lb, l1, ctx_row, name):
    bsz, t, d = x.shape
    tm = min(t, 512)
    ng = len(P_GROUPS)
    full = lambda shape: pl.BlockSpec(shape, lambda b, i: tuple(0 for _ in shape))
    out = lambda n: pl.BlockSpec((n, 1, tm, d), lambda b, i: (0, b, i, 0))
    return pl.pallas_call(
        functools.partial(_proj_kernel, ctx_row=ctx_row),
        grid=(bsz, t // tm),
        in_specs=[
            pl.BlockSpec((1, tm, d), lambda b, i: (b, i, 0)),
            full((8, 6 * d)),
            pl.BlockSpec(w_in_bf.shape, lambda b, i: (0, 0), pipeline_mode=pl.Buffered(1)),
            full((2, d)), full((2, d)),
        ],
        out_specs=[out(ng), out(2), out(2)],
        out_shape=[jax.ShapeDtypeStruct((ng, bsz, t, d), BF16), jax.ShapeDtypeStruct((2, bsz, t, d), F32),
                   jax.ShapeDtypeStruct((2, bsz, t, d), BF16)],
        compiler_params=_params("arbitrary", "arbitrary"),
        name=name,
    )(x, mod_l, w_in_bf, lb, l1)


def _hgrn_rows(q_ref, v_ref, lf_ref, k_ref, o_ref, row0, n, st_ref, reverse):
    w = HEAD_W
    rows = pl.ds(row0, n * CHUNK)
    log_f = lf_ref[0, 0, rows, :]
    k = k_ref[0, 0, rows, :].astype(F32)
    q = q_ref[0, 0, rows, :]
    v = v_ref[0, 0, rows, :]

    ri = lax.broadcasted_iota(I32, (CHUNK, CHUNK), 0)
    ci = lax.broadcasted_iota(I32, (CHUNK, CHUNK), 1)
    causal = (ci >= ri) if reverse else (ci <= ri)
    tri = causal.astype(BF16)
    lf = jnp.concatenate([log_f[c * CHUNK:(c + 1) * CHUNK, :] for c in range(n)], axis=1)
    hi = lf.astype(BF16)
    rest = lf - hi.astype(F32)
    mid = rest.astype(BF16)
    lo = (rest - mid.astype(F32)).astype(BF16)
    bc = (jnp.dot(tri, hi, preferred_element_type=F32) + jnp.dot(tri, mid, preferred_element_type=F32)
          + jnp.dot(tri, lo, preferred_element_type=F32))

    half = CHUNK // 2
    o_intra, qs_all, upd, decay = [], [], [], []
    for c in range(n):
        sl = slice(c * CHUNK, (c + 1) * CHUNK)
        bcum = bc[:, c * w:(c + 1) * w]
        b_ref = bcum[half:half + 1, :]
        b_end = bcum[0:1, :] if reverse else bcum[CHUNK - 1:CHUNK, :]
        e1 = jnp.exp(jnp.minimum(bcum - b_ref, EXP_CLAMP))
        e2 = jnp.exp(jnp.minimum(b_ref - bcum, EXP_CLAMP))
        qt = q[sl].astype(F32) * e1
        kt = k[sl] * e2
        att = lax.dot_general(qt.astype(BF16), kt.astype(BF16), (((1,), (1,)), ((), ())),
                              preferred_element_type=F32)
        att = jnp.where(causal, att, 0.0).astype(BF16)
        o_intra.append(jnp.dot(att, v[sl], preferred_element_type=F32))
        qs_all.append((qt * jnp.exp(b_ref)).astype(BF16))
        ke = (kt * jnp.exp(b_end - b_ref)).astype(BF16)
        upd.append(lax.dot_general(v[sl], ke, (((0,), (0,)), ((), ())), preferred_element_type=F32))
        decay.append(jnp.exp(b_end))

    st = st_ref[...]
    for c in (range(n - 1, -1, -1) if reverse else range(n)):
        o = o_intra[c] + lax.dot_general(qs_all[c], st.astype(BF16), (((1,), (1,)), ((), ())),
                                         preferred_element_type=F32)
        o_ref[0, 0, pl.ds(row0 + c * CHUNK, CHUNK), :] = o
        st = decay[c] * st + upd[c]
    st_ref[...] = st


HGRN_SUB = 16


def _hgrn_kernel(qc_ref, vc_ref, lcf_ref, lcb_ref, kcf_ref, kcb_ref,
                 qf_ref, vf_ref, lff_ref, kf_ref, qb_ref, vb_ref, lfb_ref, kb_ref,
                 ocf_ref, ocb_ref, of_ref, ob_ref, sf_ref, sb_ref, *, n_ctx_chunks, n_lat_chunks):
    s = pl.program_id(2)

    def run(n_chunks, fwd, bwd):
        sub = min(n_chunks, HGRN_SUB)
        n_sub = n_chunks // sub

        def body(i, carry):
            rf = pl.multiple_of(i * (sub * CHUNK), sub * CHUNK)
            rb = pl.multiple_of((n_sub - 1 - i) * (sub * CHUNK), sub * CHUNK)
            _hgrn_rows(*fwd, rf, sub, sf_ref, False)
            _hgrn_rows(*bwd, rb, sub, sb_ref, True)
            return carry
        lax.fori_loop(0, n_sub, body, 0)

    @pl.when(s == 0)
    def _():
        sf_ref[...] = jnp.zeros_like(sf_ref)
        sb_ref[...] = jnp.zeros_like(sb_ref)
        run(n_ctx_chunks, (qc_ref, vc_ref, lcf_ref, kcf_ref, ocf_ref), (qc_ref, vc_ref, lcb_ref, kcb_ref, ocb_ref))

    run(n_lat_chunks, (qf_ref, vf_ref, lff_ref, kf_ref, of_ref), (qb_ref, vb_ref, lfb_ref, kb_ref, ob_ref))


def _hgrn_call(p_lat, lf_lat, kk_lat, p_ctx, lf_ctx, kk_ctx):
    _, bsz, t, d = p_lat.shape
    tc = p_ctx.shape[2]
    w = HEAD_W
    blk = min(t, 1024)
    ns = t // blk
    lat = lambda g, rev: pl.BlockSpec(
        (1, 1, blk, w), (lambda b, h, s: (g, b, ns - 1 - s, h)) if rev else (lambda b, h, s: (g, b, s, h)))
    ctx = lambda g: pl.BlockSpec((1, 1, tc, w), lambda b, h, s: (g, b, 0, h))
    return pl.pallas_call(
        functools.partial(_hgrn_kernel, n_ctx_chunks=tc // CHUNK, n_lat_chunks=blk // CHUNK),
        grid=(bsz, N_HEADS, ns),
        in_specs=[ctx(0), ctx(1), ctx(0), ctx(1), ctx(0), ctx(1),
                  lat(0, False), lat(1, False), lat(0, False), lat(0, False),
                  lat(0, True), lat(1, True), lat(1, True), lat(1, True)],
        out_specs=[ctx(0), ctx(0), lat(0, False), lat(0, True)],
        out_shape=[jax.ShapeDtypeStruct((1, bsz, tc, d), F32), jax.ShapeDtypeStruct((1, bsz, tc, d), F32),
                   jax.ShapeDtypeStruct((1, bsz, t, d), F32), jax.ShapeDtypeStruct((1, bsz, t, d), F32)],
        scratch_shapes=[pltpu.VMEM((w, w), F32), pltpu.VMEM((w, w), F32)],
        compiler_params=_params("arbitrary", "arbitrary", "arbitrary"),
        name="hgrn",
    )(p_ctx, p_ctx, lf_ctx, lf_ctx, kk_ctx, kk_ctx,
      p_lat, p_lat, lf_lat, kk_lat, p_lat, p_lat, lf_lat, kk_lat)


def _lru_gates(xc, w_all, hb_all, c2_ref):
    w = HEAD_W
    th = jnp.tanh(jnp.dot(xc.astype(BF16), w_all, preferred_element_type=F32) + hb_all)
    out = []
    for dirn in range(2):
        t_r = th[:, (2 * dirn) * w:(2 * dirn + 1) * w]
        t_i = th[:, (2 * dirn + 1) * w:(2 * dirn + 2) * w]
        c2 = c2_ref[dirn:dirn + 1, :]
        a = jnp.exp2(c2 * t_r + c2)
        u = jnp.sqrt(1.0 - a * a) * ((0.5 * t_i + 0.5) * xc)
        out += [a, u]
    return out


def _lru_kernel(xl_ref, xc_ref, cw_ref, cb_ref, wa_ref, wx_ref, ba_ref, bx_ref, lam_ref,
                hl_ref, hc_ref,
                xpad, cpad, a_f, u_f, a_b, u_b, ca_f, cu_f, ca_b, cu_b,
                ends_f, prods_f, carry_f, ends_b, prods_b, carry_b, c2_ref,
                *, t, tc):
    w = HEAD_W
    gw = GRID_W
    rows = t // gw
    lam = lam_ref[...]
    neg = -lam
    softplus = jnp.maximum(neg, 0.0) + jnp.log(1.0 + jnp.exp(-jnp.abs(neg)))
    c2_ref[...] = (-0.5 * LRU_C * LOG2E) * softplus
    w_all = (0.5 * jnp.concatenate([wa_ref[0, 0], wx_ref[0, 0], wa_ref[1, 0], wx_ref[1, 0]], axis=1)).astype(BF16)
    hb_all = 0.5 * jnp.concatenate([ba_ref[0:1, :], bx_ref[0:1, :], ba_ref[1:2, :], bx_ref[1:2, :]], axis=1)
    w0, w1, w2, w3 = (cw_ref[k:k + 1, :] for k in range(4))
    cb = cb_ref[...]

    cpad[...] = jnp.zeros_like(cpad)
    cpad[8:8 + tc, :] = xc_ref[0, 0].astype(F32)
    xcc = (w0 * cpad[6:6 + tc, :] + w1 * cpad[7:7 + tc, :] + w2 * cpad[8:8 + tc, :]
           + w3 * cpad[9:9 + tc, :] + cb)
    ca_f[...], cu_f[...], ca_b[...], cu_b[...] = _lru_gates(xcc, w_all, hb_all, c2_ref)

    def ctx_step(i, hs):
        hf, hb = hs
        p = tc - 1 - i
        hf = ca_f[pl.ds(i, 1), :] * hf + cu_f[pl.ds(i, 1), :]
        hb = ca_b[pl.ds(p, 1), :] * hb + cu_b[pl.ds(p, 1), :]
        cu_f[pl.ds(i, 1), :] = hf
        cu_b[pl.ds(p, 1), :] = hb
        return hf, hb
    zero_row = jnp.zeros((1, w), F32)
    s0_f, s0_b = lax.fori_loop(0, tc, ctx_step, (zero_row, zero_row))
    hc_ref[0] = (cu_f[...] + cu_b[...]).astype(hc_ref.dtype)

    xpad[2 * gw:2 * gw + t, :] = xl_ref[0, 0].astype(F32)
    col = lax.broadcasted_iota(I32, (gw, w), 0)
    body0 = 2 * gw
    xpad[gw:2 * gw, :] = jnp.where(col == 0, 0.0, xpad[pl.ds(body0 + (rows - 1) * gw - 1, gw), :])
    xpad[0:gw, :] = jnp.where(col == 0, 0.0, xpad[pl.ds(body0 + (rows - 2) * gw - 1, gw), :])
    xpad[body0 + t:body0 + t + gw, :] = jnp.where(col == gw - 1, 0.0, xpad[pl.ds(body0 + 1, gw), :])

    gate_rows = min(t, 512)

    def gate_body(c, carry_):
        base = pl.multiple_of(c * gate_rows, gate_rows)
        xcv = (w0 * xpad[pl.ds(base, gate_rows), :] + w1 * xpad[pl.ds(base + gw, gate_rows), :]
               + w2 * xpad[pl.ds(base + 2 * gw, gate_rows), :]
               + w3 * xpad[pl.ds(base + 3 * gw, gate_rows), :] + cb)
        sl = pl.ds(base, gate_rows)
        a_f[sl, :], u_f[sl, :], a_b[sl, :], u_b[sl, :] = _lru_gates(xcv, w_all, hb_all, c2_ref)
        return carry_
    lax.fori_loop(0, t // gate_rows, gate_body, 0)

    def slab(r):
        return pl.ds(pl.multiple_of(r * gw, gw), gw)

    def p1(i, c):
        hf, pf, hb, pb = c
        sf, sb = slab(i), slab(rows - 1 - i)
        af, ab = a_f[sf, :], a_b[sb, :]
        return af * hf + u_f[sf, :], af * pf, ab * hb + u_b[sb, :], ab * pb
    zeros, ones = jnp.zeros((gw, w), F32), jnp.ones((gw, w), F32)
    ends_f[...], prods_f[...], ends_b[...], prods_b[...] = lax.fori_loop(0, rows, p1, (zeros, ones, zeros, ones))

    def chain(i, c):
        cf, cb = c
        jf, jb = pl.ds(i, 1), pl.ds(gw - 1 - i, 1)
        carry_f[jf, :] = cf
        carry_b[jb, :] = cb
        return prods_f[jf, :] * cf + ends_f[jf, :], prods_b[jb, :] * cb + ends_b[jb, :]
    lax.fori_loop(0, gw, chain, (s0_f, s0_b))

    def p2(i, c):
        hf, hb = c
        sf, sb = slab(i), slab(rows - 1 - i)
        hf = a_f[sf, :] * hf + u_f[sf, :]
        hb = a_b[sb, :] * hb + u_b[sb, :]
        u_f[sf, :] = hf
        u_b[sb, :] = hb
        return hf, hb
    lax.fori_loop(0, rows, p2, (carry_f[...], carry_b[...]))
    hl_ref[0] = (u_f[...] + u_b[...]).astype(hl_ref.dtype)


def _lru_call(p_lat, p_ctx, g_lat, g_ctx, conv_w, conv_b, wa, wx, ba, bx, lam):
    _, bsz, t, d = p_lat.shape
    tc = p_ctx.shape[2]
    w = HEAD_W
    nb = d // w
    gw = GRID_W
    vec = lambda n: pl.BlockSpec((n, w), lambda b, k: (0, k))
    mat = pl.BlockSpec((2, 1, w, w), lambda b, k: (0, k, 0, 0))
    return pl.pallas_call(
        functools.partial(_lru_kernel, t=t, tc=tc),
        grid=(bsz, nb),
        in_specs=[
            pl.BlockSpec((1, 1, t, w), lambda b, k: (g_lat, b, 0, k)),
            pl.BlockSpec((1, 1, tc, w), lambda b, k: (g_ctx, b, 0, k)),
            vec(4), vec(1), mat, mat, vec(2), vec(2), vec(2),
        ],
        out_specs=[pl.BlockSpec((1, t, w), lambda b, k: (b, 0, k)),
                   pl.BlockSpec((1, tc, w), lambda b, k: (b, 0, k))],
        out_shape=[jax.ShapeDtypeStruct((bsz, t, d), BF16), jax.ShapeDtypeStruct((bsz, tc, d), BF16)],
        scratch_shapes=[
            pltpu.VMEM((t + 3 * gw, w), F32), pltpu.VMEM((tc + 16, w), F32),
            pltpu.VMEM((t, w), F32), pltpu.VMEM((t, w), F32), pltpu.VMEM((t, w), F32), pltpu.VMEM((t, w), F32),
            pltpu.VMEM((tc, w), F32), pltpu.VMEM((tc, w), F32), pltpu.VMEM((tc, w), F32), pltpu.VMEM((tc, w), F32),
            pltpu.VMEM((gw, w), F32), pltpu.VMEM((gw, w), F32), pltpu.VMEM((gw, w), F32),
            pltpu.VMEM((gw, w), F32), pltpu.VMEM((gw, w), F32), pltpu.VMEM((gw, w), F32),
            pltpu.VMEM((2, w), F32),
        ],
        compiler_params=_params("arbitrary", "arbitrary"),
        name="rglru",
    )(p_lat, p_ctx, conv_w, conv_b.reshape(1, d), wa, wx, ba, bx, lam)


def _merge_kernel(of_ref, ob_ref, og_ref, hl_ref, ly_ref, ma_ref, mb_ref, x_ref, mod_ref, gn_ref,
                  wa_ref, wb_ref, wo_ref, lng_ref, lnb_ref, wrt_ref,
                  x1_ref, h2_ref, afft_ref, *, ctx_row):
    d = D_MODEL
    b = pl.program_id(0)
    row = b if ctx_row is None else ctx_row
    o = of_ref[0, 0] + ob_ref[0, 0]
    gn = gn_ref[...]
    parts = []
    for h in range(N_HEADS):
        oh = o[:, h * HEAD_W:(h + 1) * HEAD_W]
        ms = jnp.mean(oh * oh, axis=-1, keepdims=True)
        parts.append(oh * lax.rsqrt(ms + RMS_EPS) * gn)
    o_a = (jnp.concatenate(parts, axis=1) * og_ref[0, 0].astype(F32)).astype(BF16)
    y_a = jnp.dot(o_a, wa_ref[...], preferred_element_type=F32)
    y_b = jnp.dot((hl_ref[0].astype(F32) * ly_ref[0, 0].astype(F32)).astype(BF16), wb_ref[...],
                  preferred_element_type=F32)
    z = ma_ref[0, 0].astype(F32) * y_a + mb_ref[0, 0].astype(F32) * y_b
    y = jnp.dot(z.astype(BF16), wo_ref[...], preferred_element_type=F32)
    g1 = mod_ref[pl.ds(row, 1), 2 * d:3 * d]
    x1 = _layer_norm(ALPHA * x_ref[0] + g1 * y, lng_ref[0:1, :], lnb_ref[0:1, :])
    x1_ref[0] = x1
    sh2 = mod_ref[pl.ds(row, 1), 3 * d:4 * d]
    sc2 = mod_ref[pl.ds(row, 1), 4 * d:5 * d]
    h2 = x1 * (1.0 + sc2) + sh2
    h2_ref[0] = h2.astype(BF16)
    logits_t = lax.dot_general(wrt_ref[...], h2, (((1,), (1,)), ((), ())),
                               preferred_element_type=F32, precision=HIGHEST)
    pt = jnp.exp(logits_t - jnp.max(logits_t, axis=0, keepdims=True))
    afft_ref[0] = pt / jnp.sum(pt, axis=0, keepdims=True)


def _merge_call(o_f, o_b, p, groups, h_lru, x, mod_l, gn, wa, wb, wo, ln_g, ln_b, w_router, ctx_row, name):
    bsz, t, d = x.shape
    tm = min(t, 512)
    e = N_EXPERTS
    tile = lambda g: pl.BlockSpec((1, 1, tm, d), lambda b, i: (g, b, i, 0))
    tok = pl.BlockSpec((1, tm, d), lambda b, i: (b, i, 0))
    full = lambda shape: pl.BlockSpec(shape, lambda b, i: tuple(0 for _ in shape))
    return pl.pallas_call(
        functools.partial(_merge_kernel, ctx_row=ctx_row),
        grid=(bsz, t // tm),
        in_specs=[tile(0), tile(0), tile(groups["og"]), tok, tile(groups["ly"]), tile(groups["ma"]),
                  tile(groups["mb"]), tok, full((8, 6 * d)), full((1, HEAD_W)),
                  full((d, d)), full((d, d)), full((d, d)), full((2, d)), full((2, d)),
                  full((e, d))],
        out_specs=[tok, tok, pl.BlockSpec((1, e, tm), lambda b, i: (b, 0, i))],
        out_shape=[jax.ShapeDtypeStruct((bsz, t, d), F32), jax.ShapeDtypeStruct((bsz, t, d), BF16),
                   jax.ShapeDtypeStruct((bsz, e, t), F32)],
        compiler_params=_params("arbitrary", "arbitrary"),
        name=name,
    )(o_f, o_b, p, h_lru, p, p, p, x, mod_l, gn.reshape(1, HEAD_W), wa, wb, wo, ln_g, ln_b,
      w_router.T)


def _route_kernel(a_ref, posm_ref, off_ref, *, cap, t, tt):
    e = N_EXPERTS
    u = lax.bitcast_convert_type(a_ref[0], I32)
    thr = jnp.zeros((e, 1), I32)
    for bit in range(30, -1, -1):
        cand = thr | (1 << bit)
        cnt = jnp.sum((u >= cand).astype(F32), axis=1, keepdims=True)
        thr = jnp.where(cnt >= cap, cand, thr)
    gt = u > thr
    eq = u == thr
    need = cap - jnp.sum(gt.astype(F32), axis=1, keepdims=True)
    ri = lax.broadcasted_iota(I32, (tt, tt), 0)
    ci = lax.broadcasted_iota(I32, (tt, tt), 1)
    before = (ri < ci).astype(BF16)
    lane = lax.broadcasted_iota(I32, (e, OFF_STRIDE), 1)
    offs = jnp.zeros((e, OFF_STRIDE), F32)
    n_eq = jnp.zeros((e, 1), F32)
    n_sel = jnp.zeros((e, 1), F32)
    for j in range(t // tt):
        sl = slice(j * tt, (j + 1) * tt)
        eq_j = eq[:, sl]
        rank_eq = jnp.dot(eq_j.astype(BF16), before, preferred_element_type=F32) + n_eq
        sel_j = gt[:, sl] | (eq_j & (rank_eq < need))
        sel_f = sel_j.astype(F32)
        pos = jnp.dot(sel_j.astype(BF16), before, preferred_element_type=F32) + n_sel
        posm_ref[0, :, sl] = jnp.where(sel_j, pos.astype(I32), -1)
        offs = jnp.where(lane == j, n_sel, offs)
        n_eq = n_eq + jnp.sum(eq_j.astype(F32), axis=1, keepdims=True)
        n_sel = n_sel + jnp.sum(sel_f, axis=1, keepdims=True)
    offs = jnp.where(lane == t // tt, n_sel, offs)
    off_ref[0] = offs.astype(I32)


def _route_call(aff_t, cap, tt, name):
    bsz, e, t = aff_t.shape
    return pl.pallas_call(
        functools.partial(_route_kernel, cap=cap, t=t, tt=tt),
        grid=(bsz,),
        in_specs=[pl.BlockSpec((1, e, t), lambda b: (b, 0, 0))],
        out_specs=[pl.BlockSpec((1, e, t), lambda b: (b, 0, 0)),
                   pl.BlockSpec((1, e, OFF_STRIDE), lambda b: (b, 0, 0))],
        out_shape=[jax.ShapeDtypeStruct((bsz, e, t), I32), jax.ShapeDtypeStruct((bsz, e, OFF_STRIDE), I32)],
        compiler_params=_params("arbitrary"),
        name=name,
    )(aff_t)


def _expert_kernel(off_ref, posm_ref, h_ref, wg_ref, wu_ref, wd_ref, y_ref, x_ref, *, cap, rt, tt, nt):
    b = pl.program_id(0)
    e = pl.program_id(1)
    base = (b * N_EXPERTS + e) * OFF_STRIDE
    x_ref[...] = jnp.zeros_like(x_ref)
    kt = min(nt, GATHER_WINDOW_TILES)
    kw = kt * tt
    rows = lax.broadcasted_iota(I32, (rt, kw), 0)
    lane = lax.broadcasted_iota(I32, (rt, kw), 1)
    n_rt = cap // rt

    def spans(j, c):
        lo = off_ref[base + j]
        hi = off_ref[base + j + 1]
        return (tuple(c[i] + (hi <= i * rt).astype(I32) for i in range(n_rt))
                + tuple(c[n_rt + i] + (lo < (i + 1) * rt).astype(I32) for i in range(n_rt)))
    counts = lax.fori_loop(0, nt, spans, (jnp.int32(0),) * (2 * n_rt))

    for i in range(n_rt):
        first = counts[i]
        n_win = (counts[n_rt + i] - 1 - first) // kt + 1

        def body(wdx, carry, i=i, first=first):
            want = first + wdx * kt
            start = jnp.minimum(want, nt - kt)
            tok = pl.ds(pl.multiple_of(start * tt, tt), kw)
            hit = ((posm_ref[0, 0, :, tok] - i * rt) == rows) & (lane >= (want - start) * tt)
            x_ref[i * rt:(i + 1) * rt, :] += jnp.dot(hit.astype(BF16), h_ref[0, tok, :],
                                                     preferred_element_type=F32)
            return carry
        lax.fori_loop(0, n_win, body, 0)
    ft = min(cap, 256)
    for r in range(cap // ft):
        xb = x_ref[r * ft:(r + 1) * ft, :].astype(BF16)
        g = jnp.dot(xb, wg_ref[0], preferred_element_type=F32)
        u = jnp.dot(xb, wu_ref[0], preferred_element_type=F32)
        hid = (g * _sigmoid(g) * u).astype(BF16)
        y_ref[0, 0, r * ft:(r + 1) * ft, :] = jnp.dot(hid, wd_ref[0], preferred_element_type=F32).astype(BF16)


def _expert_call(off_flat, posm, h2, wg, wu, wd, cap, rt, tt, name):
    bsz, t, d = h2.shape
    e, _, f = wg.shape
    nt = t // tt
    grid_spec = pltpu.PrefetchScalarGridSpec(
        num_scalar_prefetch=1,
        grid=(bsz, e),
        in_specs=[
            pl.BlockSpec((1, 1, 1, t), lambda b, k, off: (b, k, 0, 0)),
            pl.BlockSpec((1, t, d), lambda b, k, off: (b, 0, 0), pipeline_mode=pl.Buffered(1)),
            pl.BlockSpec((1, d, f), lambda b, k, off: (k, 0, 0)),
            pl.BlockSpec((1, d, f), lambda b, k, off: (k, 0, 0)),
            pl.BlockSpec((1, f, d), lambda b, k, off: (k, 0, 0)),
        ],
        out_specs=pl.BlockSpec((1, 1, cap, d), lambda b, k, off: (b, k, 0, 0)),
        scratch_shapes=[pltpu.VMEM((cap, d), F32)],
    )
    return pl.pallas_call(
        functools.partial(_expert_kernel, cap=cap, rt=rt, tt=tt, nt=nt),
        grid_spec=grid_spec,
        out_shape=jax.ShapeDtypeStruct((bsz, e, cap, d), BF16),
        compiler_params=_params("arbitrary", "arbitrary"),
        name=name,
    )(off_flat, posm.reshape(bsz, e, 1, t), h2, wg, wu, wd)


def _combine_kernel(off_ref, posm_ref, afft_ref, x1_ref, y_hbm, mod_ref, lng_ref, lnb_ref, o_ref,
                    ybuf, pbuf, sem, acc_ref, *, rt, tt, nw, ctx_row):
    d = D_MODEL
    b = pl.program_id(0)
    j = pl.program_id(1)
    row = b if ctx_row is None else ctx_row
    shift = rt.bit_length() - 1

    @pl.when((b == 0) & (j == 0))
    def _():
        ybuf[...] = jnp.zeros_like(ybuf)

    def window(e):
        base = (b * N_EXPERTS + e) * OFF_STRIDE
        lo = off_ref[base + j]
        hi = off_ref[base + j + 1]
        t0 = lax.shift_right_logical(lo, shift)
        n = jnp.where(hi > lo, lax.shift_right_logical(hi - 1, shift) - t0 + 1, 0)
        return t0, n

    windows, slot0 = [], []
    n_used = jnp.int32(0)
    for e in range(N_EXPERTS):
        windows.append(window(e))
        slot0.append(n_used)
        n_used = n_used + windows[e][1]
    n_groups = lax.shift_right_logical(n_used + (COMBINE_GROUP - 1), COMBINE_GROUP.bit_length() - 1)

    def slot_rows(e, wdx):
        return pl.ds(pl.multiple_of((slot0[e] + wdx) * rt, rt), rt)

    def copy(e, wdx):
        src = y_hbm.at[b, e, pl.ds(pl.multiple_of((windows[e][0] + wdx) * rt, rt), rt), :]
        return pltpu.make_async_copy(src, ybuf.at[slot_rows(e, wdx), :], sem.at[e, wdx])

    def for_used_windows(fn):
        for e in range(N_EXPERTS):
            for wdx in range(nw):
                pl.when(wdx < windows[e][1])(functools.partial(fn, e, wdx))

    for_used_windows(lambda e, wdx: copy(e, wdx).start())

    def clear(s, carry):
        pbuf[pl.ds(pl.multiple_of(s * rt, rt), rt), :] = jnp.zeros((rt, tt), BF16)
        return carry
    lax.fori_loop(n_used, n_groups * COMBINE_GROUP, clear, 0)

    rows = lax.broadcasted_iota(I32, (rt, tt), 0)

    def weights(e, wdx):
        pos = posm_ref[0, e]
        gate = afft_ref[0, e:e + 1, :]
        hit = (pos - (windows[e][0] + wdx) * rt) == rows
        pbuf[slot_rows(e, wdx), :] = jnp.where(hit, gate, 0.0).astype(BF16)
    for_used_windows(weights)
    for_used_windows(lambda e, wdx: copy(e, wdx).wait())

    acc_ref[...] = jnp.zeros_like(acc_ref)
    gk = COMBINE_GROUP * rt
    for g in range(pl.cdiv(N_EXPERTS * nw, COMBINE_GROUP)):
        @pl.when(g < n_groups)
        def _(g=g):
            acc_ref[...] += lax.dot_general(pbuf[g * gk:(g + 1) * gk, :], ybuf[g * gk:(g + 1) * gk, :],
                                            (((0,), (0,)), ((), ())), preferred_element_type=F32)
    g2 = mod_ref[pl.ds(row, 1), 5 * d:6 * d]
    o_ref[0] = _layer_norm(ALPHA * x1_ref[0] + g2 * acc_ref[...], lng_ref[1:2, :], lnb_ref[1:2, :])


def _combine_call(off_flat, posm, aff_t, x1, y, mod_l, ln_g, ln_b, rt, tt, ctx_row, name):
    bsz, t, d = x1.shape
    e = N_EXPERTS
    cap = y.shape[2]
    nw = min(cap // rt, (tt - 1) // rt + 2)
    n_slots = pl.cdiv(e * nw, COMBINE_GROUP) * COMBINE_GROUP
    grid_spec = pltpu.PrefetchScalarGridSpec(
        num_scalar_prefetch=1,
        grid=(bsz, t // tt),
        in_specs=[
            pl.BlockSpec((1, e, 1, tt), lambda b, j, off: (b, 0, 0, j)),
            pl.BlockSpec((1, e, tt), lambda b, j, off: (b, 0, j)),
            pl.BlockSpec((1, tt, d), lambda b, j, off: (b, j, 0)),
            pl.BlockSpec(memory_space=pl.ANY),
            pl.BlockSpec((8, 6 * d), lambda b, j, off: (0, 0)),
            pl.BlockSpec((2, d), lambda b, j, off: (0, 0)),
            pl.BlockSpec((2, d), lambda b, j, off: (0, 0)),
        ],
        out_specs=pl.BlockSpec((1, tt, d), lambda b, j, off: (b, j, 0)),
        scratch_shapes=[pltpu.VMEM((n_slots * rt, d), BF16), pltpu.VMEM((n_slots * rt, tt), BF16),
                        pltpu.SemaphoreType.DMA((e, nw)), pltpu.VMEM((tt, d), F32)],
    )
    return pl.pallas_call(
        functools.partial(_combine_kernel, rt=rt, tt=tt, nw=nw, ctx_row=ctx_row),
        grid_spec=grid_spec,
        out_shape=jax.ShapeDtypeStruct((bsz, t, d), F32),
        compiler_params=_params("arbitrary", "arbitrary"),
        name=name,
    )(off_flat, posm.reshape(bsz, e, 1, t), aff_t, x1, y, mod_l, ln_g, ln_b)


def _moe(x1, h2, aff_t, mod_l, wg, wu, wd, ln_g, ln_b, ctx_row, tag):
    bsz, t, d = x1.shape
    cap = EC_FACTOR * t // N_EXPERTS
    tt = min(t, 256)
    rt = min(cap, 128)
    posm, off = _route_call(aff_t, cap, tt, "route_" + tag)
    off_flat = off.reshape(-1)
    y = _expert_call(off_flat, posm, h2, wg, wu, wd, cap, rt, tt, "expert_" + tag)
    return _combine_call(off_flat, posm, aff_t, x1, y, mod_l, ln_g, ln_b, rt, tt, ctx_row, "combine_" + tag)


def kernel(x, c, ctx, c_ctx, w_mod, b_mod, w_in, hgrn_lb_logits, hgrn_norm_g, conv_w, conv_b, lru_wa,
           lru_ba, lru_wx, lru_bx, lru_lambda, w_branch_a, w_branch_b, w_out, ln_g, ln_b, w_router,
           w_gate, w_up, w_down):
    depth = w_in.shape[0]
    bsz = x.shape[0]
    ctx_row = bsz
    assert bsz < 8 and depth == DEPTH

    lb_cum = jnp.cumsum(jax.nn.softmax(hgrn_lb_logits.astype(F32), axis=0), axis=0)
    lbs = lb_cum - lb_cum[0]
    log1m_lb = jnp.log1p(-lbs)

    c_all = jnp.zeros((8, x.shape[2]), F32).at[:bsz].set(c).at[ctx_row].set(c_ctx)
    mod = _mod_call(c_all, w_mod, b_mod)

    p_index = P_INDEX
    for l in range(depth):
        need_ctx = l < depth - 1
        mod_l = mod[l]
        w_in_bf = w_in[l].astype(BF16)
        p_lat, lf_lat, kk_lat = _proj_call(x, mod_l, w_in_bf, lbs[l], log1m_lb[l], None, "proj_lat")
        p_ctx, lf_ctx, kk_ctx = _proj_call(ctx, mod_l, w_in_bf, lbs[l], log1m_lb[l], ctx_row,
                                           "proj_ctx")

        oc_f, oc_b, ol_f, ol_b = _hgrn_call(p_lat, lf_lat, kk_lat, p_ctx, lf_ctx, kk_ctx)
        h_lat, h_ctx = _lru_call(p_lat, p_ctx, p_index["lx"], p_index["lx"], conv_w[l], conv_b[l],
                                 lru_wa[l], lru_wx[l], lru_ba[l], lru_bx[l], lru_lambda[l])

        wa, wb, wo = (w.astype(BF16) for w in (w_branch_a[l], w_branch_b[l], w_out[l]))
        wg, wu, wd = (w.astype(BF16) for w in (w_gate[l], w_up[l], w_down[l]))
        x1, h2, aff_t = _merge_call(ol_f, ol_b, p_lat, p_index, h_lat, x, mod_l, hgrn_norm_g[l],
                                         wa, wb, wo, ln_g[l], ln_b[l], w_router[l], None, "merge_lat")
        if need_ctx:
            c1, ch2, caff_t = _merge_call(oc_f, oc_b, p_ctx, p_index, h_ctx, ctx, mod_l, hgrn_norm_g[l],
                                                wa, wb, wo, ln_g[l], ln_b[l], w_router[l], ctx_row,
                                                "merge_ctx")
        x = _moe(x1, h2, aff_t, mod_l, wg, wu, wd, ln_g[l], ln_b[l], None, "lat")
        if need_ctx:
            ctx = _moe(c1, ch2, caff_t, mod_l, wg, wu, wd, ln_g[l], ln_b[l], ctx_row, "ctx")
    return x
```

```python
import functools

import jax
import jax.numpy as jnp
from jax import lax
from jax.experimental import pallas as pl
from jax.experimental.pallas import tpu as pltpu

F32 = jnp.float32
BF16 = jnp.bfloat16
I32 = jnp.int32
HIGHEST = lax.Precision.HIGHEST

DEPTH = 2
D_MODEL = 1024
GRID_W = 64
N_HEADS = 8
HEAD_W = 128
CHUNK = 64
N_EXPERTS = 16
EC_FACTOR = 2
LRU_C = 8.0
ALPHA = (2.0 * DEPTH) ** 0.25
LN_EPS = 1e-5
RMS_EPS = 1e-6
LN2 = 0.6931471805599453
LOG2E = 1.4426950408889634
EXP_CLAMP = 80.0

V7X_VMEM_LIMIT_BYTES = 56 * 1024 * 1024
OFF_STRIDE = 128
GATHER_WINDOW_TILES = 2
COMBINE_GROUP = 8
GATHER_ROWS = 128
SCATTER_ROWS = 64

G_Q, G_V, G_FF, G_FB, G_OG, G_LX, G_LY, G_MA, G_MB = range(9)


def _params(*sem):
    return pltpu.CompilerParams(dimension_semantics=sem, vmem_limit_bytes=V7X_VMEM_LIMIT_BYTES)


def _sigmoid(x):
    return 0.5 * jnp.tanh(0.5 * x) + 0.5


def _layer_norm(x, g, b):
    mu = jnp.mean(x, axis=-1, keepdims=True)
    xc = x - mu
    var = jnp.mean(xc * xc, axis=-1, keepdims=True)
    return xc * lax.rsqrt(var + LN_EPS) * g + b


def _mod_kernel(c_ref, w_ref, b_ref, o_ref):
    c = c_ref[...]
    s = c * _sigmoid(c)
    o_ref[0] = jnp.dot(s, w_ref[0], preferred_element_type=F32, precision=HIGHEST) + b_ref[0]


def _mod_call(c_all, w_mod, b_mod):
    depth, d, n = w_mod.shape
    tn = 1536
    return pl.pallas_call(
        _mod_kernel,
        grid=(depth, n // tn),
        in_specs=[
            pl.BlockSpec((8, d), lambda l, j: (0, 0)),
            pl.BlockSpec((1, d, tn), lambda l, j: (l, 0, j)),
            pl.BlockSpec((1, 1, tn), lambda l, j: (l, 0, j)),
        ],
        out_specs=pl.BlockSpec((1, 8, tn), lambda l, j: (l, 0, j)),
        out_shape=jax.ShapeDtypeStruct((depth, 8, n), F32),
        compiler_params=_params("arbitrary", "arbitrary"),
        name="mod",
    )(c_all, w_mod, b_mod.reshape(depth, 1, n))


def _apply_act(act, p):
    if act == "id":
        return p
    if act == "silu_scale":
        return p * _sigmoid(p) * (HEAD_W ** -0.5)
    if act == "silu":
        return p * _sigmoid(p)
    if act == "sigmoid":
        return _sigmoid(p)
    if act == "gelu":
        return 0.5 * p * (1.0 + jnp.tanh(0.7978845608028654 * (p + 0.044715 * (p * p * p))))
    raise ValueError(act)


P_GROUPS = (G_Q, G_V, G_OG, G_LX, G_LY, G_MA, G_MB)
P_ACTS = ("silu_scale", "id", "silu", "id", "gelu", "sigmoid", "sigmoid")
P_INDEX = {"q": 0, "v": 1, "og": 2, "lx": 3, "ly": 4, "ma": 5, "mb": 6}


def _hgrn_gates(fp, lb, l1):
    e = jnp.exp(-jnp.abs(fp))
    r = 1.0 / (1.0 + e)
    sig = jnp.where(fp >= 0.0, r, e * r)
    k = (1.0 - lb) * jnp.where(fp >= 0.0, e * r, r)
    log_f = jnp.maximum(jnp.log(lb + (1.0 - lb) * sig), l1 + jnp.minimum(fp, 0.0) - LN2)
    return log_f, k


def _proj_kernel(x_ref, mod_ref, w_ref, lb_ref, l1_ref, p_ref, lf_ref, kk_ref, *, ctx_row):
    d = D_MODEL
    b = pl.program_id(0)
    row = b if ctx_row is None else ctx_row
    sh = mod_ref[pl.ds(row, 1), 0:d]
    sc = mod_ref[pl.ds(row, 1), d:2 * d]
    h = (x_ref[0] * (1.0 + sc) + sh).astype(BF16)

    def group(col):
        return jnp.dot(h, w_ref[:, col * d:(col + 1) * d], preferred_element_type=F32)

    def plain(name):
        g = P_INDEX[name]
        p_ref[g, 0] = _apply_act(P_ACTS[g], group(P_GROUPS[g])).astype(BF16)

    def gate(dirn, col):
        log_f, k = _hgrn_gates(group(col), lb_ref[dirn:dirn + 1, :], l1_ref[dirn:dirn + 1, :])
        lf_ref[dirn, 0] = log_f
        kk_ref[dirn, 0] = k.astype(BF16)

    gate(0, G_FF)
    plain("v")
    gate(1, G_FB)
    plain("lx")
    plain("q")
    plain("ly")
    plain("og")
    plain("ma")
    plain("mb")


def _proj_call(x, mod_l, w_in_bf, lb, l1, ctx_row, name):
    bsz, t, d = x.shape
    tm = min(t, 512)
    ng = len(P_GROUPS)
    full = lambda shape: pl.BlockSpec(shape, lambda b, i: tuple(0 for _ in shape))
    out = lambda n: pl.BlockSpec((n, 1, tm, d), lambda b, i: (0, b, i, 0))
    return pl.pallas_call(
        functools.partial(_proj_kernel, ctx_row=ctx_row),
        grid=(bsz, t // tm),
        in_specs=[
            pl.BlockSpec((1, tm, d), lambda b, i: (b, i, 0)),
            full((8, 6 * d)),
            pl.BlockSpec(w_in_bf.shape, lambda b, i: (0, 0), pipeline_mode=pl.Buffered(1)),
            full((2, d)), full((2, d)),
        ],
        out_specs=[out(ng), out(2), out(2)],
        out_shape=[jax.ShapeDtypeStruct((ng, bsz, t, d), BF16), jax.ShapeDtypeStruct((2, bsz, t, d), F32),
                   jax.ShapeDtypeStruct((2, bsz, t, d), BF16)],
        compiler_params=_params("arbitrary", "arbitrary"),
        name=name,
    )(x, mod_l, w_in_bf, lb, l1)


def _hgrn_rows(q_ref, v_ref, lf_ref, k_ref, o_ref, row0, n, st_ref, reverse):
    w = HEAD_W
    rows = pl.ds(row0, n * CHUNK)
    log_f = lf_ref[0, 0, rows, :]
    k = k_ref[0, 0, rows, :].astype(F32)
    q = q_ref[0, 0, rows, :]
    v = v_ref[0, 0, rows, :]

    ri = lax.broadcasted_iota(I32, (CHUNK, CHUNK), 0)
    ci = lax.broadcasted_iota(I32, (CHUNK, CHUNK), 1)
    causal = (ci >= ri) if reverse else (ci <= ri)
    tri = causal.astype(BF16)
    lf = jnp.concatenate([log_f[c * CHUNK:(c + 1) * CHUNK, :] for c in range(n)], axis=1)
    hi = lf.astype(BF16)
    rest = lf - hi.astype(F32)
    mid = rest.astype(BF16)
    lo = (rest - mid.astype(F32)).astype(BF16)
    bc = (jnp.dot(tri, hi, preferred_element_type=F32) + jnp.dot(tri, mid, preferred_element_type=F32)
          + jnp.dot(tri, lo, preferred_element_type=F32))

    half = CHUNK // 2
    o_intra, qs_all, upd, decay = [], [], [], []
    for c in range(n):
        sl = slice(c * CHUNK, (c + 1) * CHUNK)
        bcum = bc[:, c * w:(c + 1) * w]
        b_ref = bcum[half:half + 1, :]
        b_end = bcum[0:1, :] if reverse else bcum[CHUNK - 1:CHUNK, :]
        e1 = jnp.exp(jnp.minimum(bcum - b_ref, EXP_CLAMP))
        e2 = jnp.exp(jnp.minimum(b_ref - bcum, EXP_CLAMP))
        qt = q[sl].astype(F32) * e1
        kt = k[sl] * e2
        att = lax.dot_general(qt.astype(BF16), kt.astype(BF16), (((1,), (1,)), ((), ())),
                              preferred_element_type=F32)
        att = jnp.where(causal, att, 0.0).astype(BF16)
        o_intra.append(jnp.dot(att, v[sl], preferred_element_type=F32))
        qs_all.append((qt * jnp.exp(b_ref)).astype(BF16))
        ke = (kt * jnp.exp(b_end - b_ref)).astype(BF16)
        upd.append(lax.dot_general(v[sl], ke, (((0,), (0,)), ((), ())), preferred_element_type=F32))
        decay.append(jnp.exp(b_end))

    st = st_ref[...]
    for c in (range(n - 1, -1, -1) if reverse else range(n)):
        o = o_intra[c] + lax.dot_general(qs_all[c], st.astype(BF16), (((1,), (1,)), ((), ())),
                                         preferred_element_type=F32)
        o_ref[0, 0, pl.ds(row0 + c * CHUNK, CHUNK), :] = o
        st = decay[c] * st + upd[c]
    st_ref[...] = st


HGRN_SUB = 16


def _hgrn_kernel(qc_ref, vc_ref, lcf_ref, lcb_ref, kcf_ref, kcb_ref,
                 qf_ref, vf_ref, lff_ref, kf_ref, qb_ref, vb_ref, lfb_ref, kb_ref,
                 ocf_ref, ocb_ref, of_ref, ob_ref, sf_ref, sb_ref, *, n_ctx_chunks, n_lat_chunks):
    s = pl.program_id(2)

    def run(n_chunks, fwd, bwd):
        sub = min(n_chunks, HGRN_SUB)
        n_sub = n_chunks // sub

        def body(i, carry):
            rf = pl.multiple_of(i * (sub * CHUNK), sub * CHUNK)
            rb = pl.multiple_of((n_sub - 1 - i) * (sub * CHUNK), sub * CHUNK)
            _hgrn_rows(*fwd, rf, sub, sf_ref, False)
            _hgrn_rows(*bwd, rb, sub, sb_ref, True)
            return carry
        lax.fori_loop(0, n_sub, body, 0)

    @pl.when(s == 0)
    def _():
        sf_ref[...] = jnp.zeros_like(sf_ref)
        sb_ref[...] = jnp.zeros_like(sb_ref)
        run(n_ctx_chunks, (qc_ref, vc_ref, lcf_ref, kcf_ref, ocf_ref), (qc_ref, vc_ref, lcb_ref, kcb_ref, ocb_ref))

    run(n_lat_chunks, (qf_ref, vf_ref, lff_ref, kf_ref, of_ref), (qb_ref, vb_ref, lfb_ref, kb_ref, ob_ref))


def _hgrn_call(p_lat, lf_lat, kk_lat, p_ctx, lf_ctx, kk_ctx):
    _, bsz, t, d = p_lat.shape
    tc = p_ctx.shape[2]
    w = HEAD_W
    blk = min(t, 1024)
    ns = t // blk
    lat = lambda g, rev: pl.BlockSpec(
        (1, 1, blk, w), (lambda b, h, s: (g, b, ns - 1 - s, h)) if rev else (lambda b, h, s: (g, b, s, h)))
    ctx = lambda g: pl.BlockSpec((1, 1, tc, w), lambda b, h, s: (g, b, 0, h))
    return pl.pallas_call(
        functools.partial(_hgrn_kernel, n_ctx_chunks=tc // CHUNK, n_lat_chunks=blk // CHUNK),
        grid=(bsz, N_HEADS, ns),
        in_specs=[ctx(0), ctx(1), ctx(0), ctx(1), ctx(0), ctx(1),
                  lat(0, False), lat(1, False), lat(0, False), lat(0, False),
                  lat(0, True), lat(1, True), lat(1, True), lat(1, True)],
        out_specs=[ctx(0), ctx(0), lat(0, False), lat(0, True)],
        out_shape=[jax.ShapeDtypeStruct((1, bsz, tc, d), F32), jax.ShapeDtypeStruct((1, bsz, tc, d), F32),
                   jax.ShapeDtypeStruct((1, bsz, t, d), F32), jax.ShapeDtypeStruct((1, bsz, t, d), F32)],
        scratch_shapes=[pltpu.VMEM((w, w), F32), pltpu.VMEM((w, w), F32)],
        compiler_params=_params("arbitrary", "arbitrary", "arbitrary"),
        name="hgrn",
    )(p_ctx, p_ctx, lf_ctx, lf_ctx, kk_ctx, kk_ctx,
      p_lat, p_lat, lf_lat, kk_lat, p_lat, p_lat, lf_lat, kk_lat)


def _lru_gates(xc, w_all, hb_all, c2_ref):
    w = HEAD_W
    th = jnp.tanh(jnp.dot(xc.astype(BF16), w_all, preferred_element_type=F32) + hb_all)
    out = []
    for dirn in range(2):
        t_r = th[:, (2 * dirn) * w:(2 * dirn + 1) * w]
        t_i = th[:, (2 * dirn + 1) * w:(2 * dirn + 2) * w]
        c2 = c2_ref[dirn:dirn + 1, :]
        a = jnp.exp2(c2 * t_r + c2)
        u = jnp.sqrt(1.0 - a * a) * ((0.5 * t_i + 0.5) * xc)
        out += [a, u]
    return out


def _lru_kernel(xl_ref, xc_ref, cw_ref, cb_ref, wa_ref, wx_ref, ba_ref, bx_ref, lam_ref,
                hl_ref, hc_ref,
                xpad, cpad, a_f, u_f, a_b, u_b, ca_f, cu_f, ca_b, cu_b,
                ends_f, prods_f, carry_f, ends_b, prods_b, carry_b, c2_ref,
                *, t, tc):
    w = HEAD_W
    gw = GRID_W
    rows = t // gw
    lam = lam_ref[...]
    neg = -lam
    softplus = jnp.maximum(neg, 0.0) + jnp.log(1.0 + jnp.exp(-jnp.abs(neg)))
    c2_ref[...] = (-0.5 * LRU_C * LOG2E) * softplus
    w_all = (0.5 * jnp.concatenate([wa_ref[0, 0], wx_ref[0, 0], wa_ref[1, 0], wx_ref[1, 0]], axis=1)).astype(BF16)
    hb_all = 0.5 * jnp.concatenate([ba_ref[0:1, :], bx_ref[0:1, :], ba_ref[1:2, :], bx_ref[1:2, :]], axis=1)
    w0, w1, w2, w3 = (cw_ref[k:k + 1, :] for k in range(4))
    cb = cb_ref[...]

    cpad[...] = jnp.zeros_like(cpad)
    cpad[8:8 + tc, :] = xc_ref[0, 0].astype(F32)
    xcc = (w0 * cpad[6:6 + tc, :] + w1 * cpad[7:7 + tc, :] + w2 * cpad[8:8 + tc, :]
           + w3 * cpad[9:9 + tc, :] + cb)
    ca_f[...], cu_f[...], ca_b[...], cu_b[...] = _lru_gates(xcc, w_all, hb_all, c2_ref)

    def ctx_step(i, hs):
        hf, hb = hs
        p = tc - 1 - i
        hf = ca_f[pl.ds(i, 1), :] * hf + cu_f[pl.ds(i, 1), :]
        hb = ca_b[pl.ds(p, 1), :] * hb + cu_b[pl.ds(p, 1), :]
        cu_f[pl.ds(i, 1), :] = hf
        cu_b[pl.ds(p, 1), :] = hb
        return hf, hb
    zero_row = jnp.zeros((1, w), F32)
    s0_f, s0_b = lax.fori_loop(0, tc, ctx_step, (zero_row, zero_row))
    hc_ref[0] = (cu_f[...] + cu_b[...]).astype(hc_ref.dtype)

    xpad[2 * gw:2 * gw + t, :] = xl_ref[0, 0].astype(F32)
    col = lax.broadcasted_iota(I32, (gw, w), 0)
    body0 = 2 * gw
    xpad[gw:2 * gw, :] = jnp.where(col == 0, 0.0, xpad[pl.ds(body0 + (rows - 1) * gw - 1, gw), :])
    xpad[0:gw, :] = jnp.where(col == 0, 0.0, xpad[pl.ds(body0 + (rows - 2) * gw - 1, gw), :])
    xpad[body0 + t:body0 + t + gw, :] = jnp.where(col == gw - 1, 0.0, xpad[pl.ds(body0 + 1, gw), :])

    gate_rows = min(t, 512)

    def gate_body(c, carry_):
        base = pl.multiple_of(c * gate_rows, gate_rows)
        xcv = (w0 * xpad[pl.ds(base, gate_rows), :] + w1 * xpad[pl.ds(base + gw, gate_rows), :]
               + w2 * xpad[pl.ds(base + 2 * gw, gate_rows), :]
               + w3 * xpad[pl.ds(base + 3 * gw, gate_rows), :] + cb)
        sl = pl.ds(base, gate_rows)
        a_f[sl, :], u_f[sl, :], a_b[sl, :], u_b[sl, :] = _lru_gates(xcv, w_all, hb_all, c2_ref)
        return carry_
    lax.fori_loop(0, t // gate_rows, gate_body, 0)

    def slab(r):
        return pl.ds(pl.multiple_of(r * gw, gw), gw)

    def p1(i, c):
        hf, pf, hb, pb = c
        sf, sb = slab(i), slab(rows - 1 - i)
        af, ab = a_f[sf, :], a_b[sb, :]
        return af * hf + u_f[sf, :], af * pf, ab * hb + u_b[sb, :], ab * pb
    zeros, ones = jnp.zeros((gw, w), F32), jnp.ones((gw, w), F32)
    ends_f[...], prods_f[...], ends_b[...], prods_b[...] = lax.fori_loop(0, rows, p1, (zeros, ones, zeros, ones))

    def chain(i, c):
        cf, cb = c
        jf, jb = pl.ds(i, 1), pl.ds(gw - 1 - i, 1)
        carry_f[jf, :] = cf
        carry_b[jb, :] = cb
        return prods_f[jf, :] * cf + ends_f[jf, :], prods_b[jb, :] * cb + ends_b[jb, :]
    lax.fori_loop(0, gw, chain, (s0_f, s0_b))

    def p2(i, c):
        hf, hb = c
        sf, sb = slab(i), slab(rows - 1 - i)
        hf = a_f[sf, :] * hf + u_f[sf, :]
        hb = a_b[sb, :] * hb + u_b[sb, :]
        u_f[sf, :] = hf
        u_b[sb, :] = hb
        return hf, hb
    lax.fori_loop(0, rows, p2, (carry_f[...], carry_b[...]))
    hl_ref[0] = (u_f[...] + u_b[...]).astype(hl_ref.dtype)


def _lru_call(p_lat, p_ctx, g_lat, g_ctx, conv_w, conv_b, wa, wx, ba, bx, lam):
    _, bsz, t, d = p_lat.shape
    tc = p_ctx.shape[2]
    w = HEAD_W
    nb = d // w
    gw = GRID_W
    vec = lambda n: pl.BlockSpec((n, w), lambda b, k: (0, k))
    mat = pl.BlockSpec((2, 1, w, w), lambda b, k: (0, k, 0, 0))
    return pl.pallas_call(
        functools.partial(_lru_kernel, t=t, tc=tc),
        grid=(bsz, nb),
        in_specs=[
            pl.BlockSpec((1, 1, t, w), lambda b, k: (g_lat, b, 0, k)),
            pl.BlockSpec((1, 1, tc, w), lambda b, k: (g_ctx, b, 0, k)),
            vec(4), vec(1), mat, mat, vec(2), vec(2), vec(2),
        ],
        out_specs=[pl.BlockSpec((1, t, w), lambda b, k: (b, 0, k)),
                   pl.BlockSpec((1, tc, w), lambda b, k: (b, 0, k))],
        out_shape=[jax.ShapeDtypeStruct((bsz, t, d), BF16), jax.ShapeDtypeStruct((bsz, tc, d), BF16)],
        scratch_shapes=[
            pltpu.VMEM((t + 3 * gw, w), F32), pltpu.VMEM((tc + 16, w), F32),
            pltpu.VMEM((t, w), F32), pltpu.VMEM((t, w), F32), pltpu.VMEM((t, w), F32), pltpu.VMEM((t, w), F32),
            pltpu.VMEM((tc, w), F32), pltpu.VMEM((tc, w), F32), pltpu.VMEM((tc, w), F32), pltpu.VMEM((tc, w), F32),
            pltpu.VMEM((gw, w), F32), pltpu.VMEM((gw, w), F32), pltpu.VMEM((gw, w), F32),
            pltpu.VMEM((gw, w), F32), pltpu.VMEM((gw, w), F32), pltpu.VMEM((gw, w), F32),
            pltpu.VMEM((2, w), F32),
        ],
        compiler_params=_params("arbitrary", "arbitrary"),
        name="rglru",
    )(p_lat, p_ctx, conv_w, conv_b.reshape(1, d), wa, wx, ba, bx, lam)


def _merge_kernel(of_ref, ob_ref, og_ref, hl_ref, ly_ref, ma_ref, mb_ref, x_ref, mod_ref, gn_ref,
                  wa_ref, wb_ref, wo_ref, lng_ref, lnb_ref, wrt_ref,
                  x1_ref, h2_ref, afft_ref, *, ctx_row):
    d = D_MODEL
    b = pl.program_id(0)
    row = b if ctx_row is None else ctx_row
    o = of_ref[0, 0] + ob_ref[0, 0]
    gn = gn_ref[...]
    parts = []
    for h in range(N_HEADS):
        oh = o[:, h * HEAD_W:(h + 1) * HEAD_W]
        ms = jnp.mean(oh * oh, axis=-1, keepdims=True)
        parts.append(oh * lax.rsqrt(ms + RMS_EPS) * gn)
    o_a = (jnp.concatenate(parts, axis=1) * og_ref[0, 0].astype(F32)).astype(BF16)
    y_a = jnp.dot(o_a, wa_ref[...], preferred_element_type=F32)
    y_b = jnp.dot((hl_ref[0].astype(F32) * ly_ref[0, 0].astype(F32)).astype(BF16), wb_ref[...],
                  preferred_element_type=F32)
    z = ma_ref[0, 0].astype(F32) * y_a + mb_ref[0, 0].astype(F32) * y_b
    y = jnp.dot(z.astype(BF16), wo_ref[...], preferred_element_type=F32)
    g1 = mod_ref[pl.ds(row, 1), 2 * d:3 * d]
    x1 = _layer_norm(ALPHA * x_ref[0] + g1 * y, lng_ref[0:1, :], lnb_ref[0:1, :])
    x1_ref[0] = x1
    sh2 = mod_ref[pl.ds(row, 1), 3 * d:4 * d]
    sc2 = mod_ref[pl.ds(row, 1), 4 * d:5 * d]
    h2 = x1 * (1.0 + sc2) + sh2
    h2_ref[0] = h2.astype(BF16)
    logits_t = lax.dot_general(wrt_ref[...], h2, (((1,), (1,)), ((), ())),
                               preferred_element_type=F32, precision=HIGHEST)
    pt = jnp.exp(logits_t - jnp.max(logits_t, axis=0, keepdims=True))
    afft_ref[0] = pt / jnp.sum(pt, axis=0, keepdims=True)


def _merge_call(o_f, o_b, p, groups, h_lru, x, mod_l, gn, wa, wb, wo, ln_g, ln_b, w_router, ctx_row, name):
    bsz, t, d = x.shape
    tm = min(t, 512)
    e = N_EXPERTS
    tile = lambda g: pl.BlockSpec((1, 1, tm, d), lambda b, i: (g, b, i, 0))
    tok = pl.BlockSpec((1, tm, d), lambda b, i: (b, i, 0))
    full = lambda shape: pl.BlockSpec(shape, lambda b, i: tuple(0 for _ in shape))
    return pl.pallas_call(
        functools.partial(_merge_kernel, ctx_row=ctx_row),
        grid=(bsz, t // tm),
        in_specs=[tile(0), tile(0), tile(groups["og"]), tok, tile(groups["ly"]), tile(groups["ma"]),
                  tile(groups["mb"]), tok, full((8, 6 * d)), full((1, HEAD_W)),
                  full((d, d)), full((d, d)), full((d, d)), full((2, d)), full((2, d)),
                  full((e, d))],
        out_specs=[tok, tok, pl.BlockSpec((1, e, tm), lambda b, i: (b, 0, i))],
        out_shape=[jax.ShapeDtypeStruct((bsz, t, d), F32), jax.ShapeDtypeStruct((bsz, t, d), BF16),
                   jax.ShapeDtypeStruct((bsz, e, t), F32)],
        compiler_params=_params("arbitrary", "arbitrary"),
        name=name,
    )(o_f, o_b, p, h_lru, p, p, p, x, mod_l, gn.reshape(1, HEAD_W), wa, wb, wo, ln_g, ln_b,
      w_router.T)


def _route_kernel(a_ref, posm_ref, off_ref, *, cap, t, tt):
    e = N_EXPERTS
    u = lax.bitcast_convert_type(a_ref[0], I32)
    thr = jnp.zeros((e, 1), I32)
    for bit in range(30, -1, -1):
        cand = thr | (1 << bit)
        cnt = jnp.sum((u >= cand).astype(F32), axis=1, keepdims=True)
        thr = jnp.where(cnt >= cap, cand, thr)
    gt = u > thr
    eq = u == thr
    need = cap - jnp.sum(gt.astype(F32), axis=1, keepdims=True)
    ri = lax.broadcasted_iota(I32, (tt, tt), 0)
    ci = lax.broadcasted_iota(I32, (tt, tt), 1)
    before = (ri < ci).astype(BF16)
    lane = lax.broadcasted_iota(I32, (e, OFF_STRIDE), 1)
    offs = jnp.zeros((e, OFF_STRIDE), F32)
    n_eq = jnp.zeros((e, 1), F32)
    n_sel = jnp.zeros((e, 1), F32)
    for j in range(t // tt):
        sl = slice(j * tt, (j + 1) * tt)
        eq_j = eq[:, sl]
        rank_eq = jnp.dot(eq_j.astype(BF16), before, preferred_element_type=F32) + n_eq
        sel_j = gt[:, sl] | (eq_j & (rank_eq < need))
        sel_f = sel_j.astype(F32)
        pos = jnp.dot(sel_j.astype(BF16), before, preferred_element_type=F32) + n_sel
        posm_ref[0, :, sl] = jnp.where(sel_j, pos.astype(I32), -1)
        offs = jnp.where(lane == j, n_sel, offs)
        n_eq = n_eq + jnp.sum(eq_j.astype(F32), axis=1, keepdims=True)
        n_sel = n_sel + jnp.sum(sel_f, axis=1, keepdims=True)
    offs = jnp.where(lane == t // tt, n_sel, offs)
    off_ref[0] = offs.astype(I32)


def _route_call(aff_t, cap, tt, name):
    bsz, e, t = aff_t.shape
    return pl.pallas_call(
        functools.partial(_route_kernel, cap=cap, t=t, tt=tt),
        grid=(bsz,),
        in_specs=[pl.BlockSpec((1, e, t), lambda b: (b, 0, 0))],
        out_specs=[pl.BlockSpec((1, e, t), lambda b: (b, 0, 0)),
                   pl.BlockSpec((1, e, OFF_STRIDE), lambda b: (b, 0, 0))],
        out_shape=[jax.ShapeDtypeStruct((bsz, e, t), I32), jax.ShapeDtypeStruct((bsz, e, OFF_STRIDE), I32)],
        compiler_params=_params("arbitrary"),
        name=name,
    )(aff_t)


def _expert_kernel(off_ref, posm_ref, h_ref, wg_ref, wu_ref, wd_ref, y_ref, x_ref, *, cap, rt, tt, nt,
                   expert_major):
    b = pl.program_id(1 if expert_major else 0)
    e = pl.program_id(0 if expert_major else 1)
    base = (b * N_EXPERTS + e) * OFF_STRIDE
    x_ref[...] = jnp.zeros_like(x_ref)
    kt = min(nt, GATHER_WINDOW_TILES)
    kw = kt * tt
    rows = lax.broadcasted_iota(I32, (rt, kw), 0)
    lane = lax.broadcasted_iota(I32, (rt, kw), 1)
    n_rt = cap // rt

    def spans(j, c):
        lo = off_ref[base + j]
        hi = off_ref[base + j + 1]
        return (tuple(c[i] + (hi <= i * rt).astype(I32) for i in range(n_rt))
                + tuple(c[n_rt + i] + (lo < (i + 1) * rt).astype(I32) for i in range(n_rt)))
    counts = lax.fori_loop(0, nt, spans, (jnp.int32(0),) * (2 * n_rt))

    for i in range(n_rt):
        first = counts[i]
        n_win = (counts[n_rt + i] - 1 - first) // kt + 1

        def body(wdx, carry, i=i, first=first):
            want = first + wdx * kt
            start = jnp.minimum(want, nt - kt)
            tok = pl.ds(pl.multiple_of(start * tt, tt), kw)
            hit = ((posm_ref[0, 0, :, tok] - i * rt) == rows) & (lane >= (want - start) * tt)
            x_ref[i * rt:(i + 1) * rt, :] += jnp.dot(hit.astype(BF16), h_ref[0, tok, :],
                                                     preferred_element_type=F32).astype(BF16)
            return carry
        lax.fori_loop(0, n_win, body, 0)
    wg, wu, wd = wg_ref[0, 0].astype(BF16), wu_ref[0, 0].astype(BF16), wd_ref[0, 0].astype(BF16)
    ft = min(cap, 256)
    for r in range(cap // ft):
        xb = x_ref[r * ft:(r + 1) * ft, :]
        g = jnp.dot(xb, wg, preferred_element_type=F32)
        u = jnp.dot(xb, wu, preferred_element_type=F32)
        hid = (g * _sigmoid(g) * u).astype(BF16)
        y_ref[0, 0, r * ft:(r + 1) * ft, :] = jnp.dot(hid, wd, preferred_element_type=F32).astype(BF16)


def _expert_call(off_flat, posm, h2, w_gate, w_up, w_down, layer, cap, rt, tt, name):
    bsz, t, d = h2.shape
    _, e, _, f = w_gate.shape
    nt = t // tt
    expert_major = t < d
    if expert_major:
        grid = (e, bsz)
        be = lambda k, b: (b, k)
    else:
        grid = (bsz, e)
        be = lambda b, k: (b, k)
    tok_spec = pl.BlockSpec((1, t, d), lambda i, j, off: (be(i, j)[0], 0, 0),
                            pipeline_mode=None if expert_major else pl.Buffered(1))
    weight = lambda r, c: pl.BlockSpec((1, 1, r, c), lambda i, j, off: (layer, be(i, j)[1], 0, 0))
    grid_spec = pltpu.PrefetchScalarGridSpec(
        num_scalar_prefetch=1,
        grid=grid,
        in_specs=[
            pl.BlockSpec((1, 1, 1, t), lambda i, j, off: (*be(i, j), 0, 0)),
            tok_spec, weight(d, f), weight(d, f), weight(f, d),
        ],
        out_specs=pl.BlockSpec((1, 1, cap, d), lambda i, j, off: (*be(i, j), 0, 0)),
        scratch_shapes=[pltpu.VMEM((cap, d), BF16)],
    )
    return pl.pallas_call(
        functools.partial(_expert_kernel, cap=cap, rt=rt, tt=tt, nt=nt, expert_major=expert_major),
        grid_spec=grid_spec,
        out_shape=jax.ShapeDtypeStruct((bsz, e, cap, d), BF16),
        compiler_params=_params("arbitrary", "arbitrary"),
        name=name,
    )(off_flat, posm.reshape(bsz, e, 1, t), h2, w_gate, w_up, w_down)


def _combine_kernel(off_ref, posm_ref, afft_ref, x1_ref, y_hbm, mod_ref, lng_ref, lnb_ref, o_ref,
                    ybuf, pbuf, sem, acc_ref, *, rt, tt, nw, ctx_row):
    d = D_MODEL
    b = pl.program_id(0)
    j = pl.program_id(1)
    row = b if ctx_row is None else ctx_row
    shift = rt.bit_length() - 1

    @pl.when((b == 0) & (j == 0))
    def _():
        ybuf[...] = jnp.zeros_like(ybuf)

    def window(e):
        base = (b * N_EXPERTS + e) * OFF_STRIDE
        lo = off_ref[base + j]
        hi = off_ref[base + j + 1]
        t0 = lax.shift_right_logical(lo, shift)
        n = jnp.where(hi > lo, lax.shift_right_logical(hi - 1, shift) - t0 + 1, 0)
        return t0, n

    windows, slot0 = [], []
    n_used = jnp.int32(0)
    for e in range(N_EXPERTS):
        windows.append(window(e))
        slot0.append(n_used)
        n_used = n_used + windows[e][1]
    n_groups = lax.shift_right_logical(n_used + (COMBINE_GROUP - 1), COMBINE_GROUP.bit_length() - 1)

    def slot_rows(e, wdx):
        return pl.ds(pl.multiple_of((slot0[e] + wdx) * rt, rt), rt)

    def copy(e, wdx):
        src = y_hbm.at[b, e, pl.ds(pl.multiple_of((windows[e][0] + wdx) * rt, rt), rt), :]
        return pltpu.make_async_copy(src, ybuf.at[slot_rows(e, wdx), :], sem.at[e, wdx])

    def for_used_windows(fn):
        for e in range(N_EXPERTS):
            for wdx in range(nw):
                pl.when(wdx < windows[e][1])(functools.partial(fn, e, wdx))

    for_used_windows(lambda e, wdx: copy(e, wdx).start())

    def clear(s, carry):
        pbuf[pl.ds(pl.multiple_of(s * rt, rt), rt), :] = jnp.zeros((rt, tt), BF16)
        return carry
    lax.fori_loop(n_used, n_groups * COMBINE_GROUP, clear, 0)

    rows = lax.broadcasted_iota(I32, (rt, tt), 0)

    def weights(e, wdx):
        pos = posm_ref[0, e]
        gate = afft_ref[0, e:e + 1, :]
        hit = (pos - (windows[e][0] + wdx) * rt) == rows
        pbuf[slot_rows(e, wdx), :] = jnp.where(hit, gate, 0.0).astype(BF16)
    for_used_windows(weights)
    for_used_windows(lambda e, wdx: copy(e, wdx).wait())

    acc_ref[...] = jnp.zeros_like(acc_ref)
    gk = COMBINE_GROUP * rt
    for g in range(pl.cdiv(N_EXPERTS * nw, COMBINE_GROUP)):
        @pl.when(g < n_groups)
        def _(g=g):
            acc_ref[...] += lax.dot_general(pbuf[g * gk:(g + 1) * gk, :], ybuf[g * gk:(g + 1) * gk, :],
                                            (((0,), (0,)), ((), ())), preferred_element_type=F32)
    g2 = mod_ref[pl.ds(row, 1), 5 * d:6 * d]
    o_ref[0] = _layer_norm(ALPHA * x1_ref[0] + g2 * acc_ref[...], lng_ref[1:2, :], lnb_ref[1:2, :])


def _combine_call(off_flat, posm, aff_t, x1, y, mod_l, ln_g, ln_b, rt, tt, ctx_row, name):
    bsz, t, d = x1.shape
    e = N_EXPERTS
    cap = y.shape[2]
    nw = min(cap // rt, (tt - 1) // rt + 2)
    n_slots = pl.cdiv(e * nw, COMBINE_GROUP) * COMBINE_GROUP
    grid_spec = pltpu.PrefetchScalarGridSpec(
        num_scalar_prefetch=1,
        grid=(bsz, t // tt),
        in_specs=[
            pl.BlockSpec((1, e, 1, tt), lambda b, j, off: (b, 0, 0, j)),
            pl.BlockSpec((1, e, tt), lambda b, j, off: (b, 0, j)),
            pl.BlockSpec((1, tt, d), lambda b, j, off: (b, j, 0)),
            pl.BlockSpec(memory_space=pl.ANY),
            pl.BlockSpec((8, 6 * d), lambda b, j, off: (0, 0)),
            pl.BlockSpec((2, d), lambda b, j, off: (0, 0)),
            pl.BlockSpec((2, d), lambda b, j, off: (0, 0)),
        ],
        out_specs=pl.BlockSpec((1, tt, d), lambda b, j, off: (b, j, 0)),
        scratch_shapes=[pltpu.VMEM((n_slots * rt, d), BF16), pltpu.VMEM((n_slots * rt, tt), BF16),
                        pltpu.SemaphoreType.DMA((e, nw)), pltpu.VMEM((tt, d), F32)],
    )
    return pl.pallas_call(
        functools.partial(_combine_kernel, rt=rt, tt=tt, nw=nw, ctx_row=ctx_row),
        grid_spec=grid_spec,
        out_shape=jax.ShapeDtypeStruct((bsz, t, d), F32),
        compiler_params=_params("arbitrary", "arbitrary"),
        name=name,
    )(off_flat, posm.reshape(bsz, e, 1, t), aff_t, x1, y, mod_l, ln_g, ln_b)


def _moe(x1, h2, aff_t, mod_l, w_gate, w_up, w_down, layer, ln_g, ln_b, ctx_row, tag):
    bsz, t, d = x1.shape
    cap = EC_FACTOR * t // N_EXPERTS
    tt = min(t, 256)
    posm, off = _route_call(aff_t, cap, tt, "route_" + tag)
    off_flat = off.reshape(-1)
    y = _expert_call(off_flat, posm, h2, w_gate, w_up, w_down, layer, cap, min(cap, GATHER_ROWS), tt,
                     "expert_" + tag)
    return _combine_call(off_flat, posm, aff_t, x1, y, mod_l, ln_g, ln_b, min(cap, SCATTER_ROWS), tt, ctx_row,
                         "combine_" + tag)


def kernel(x, c, ctx, c_ctx, w_mod, b_mod, w_in, hgrn_lb_logits, hgrn_norm_g, conv_w, conv_b, lru_wa,
           lru_ba, lru_wx, lru_bx, lru_lambda, w_branch_a, w_branch_b, w_out, ln_g, ln_b, w_router,
           w_gate, w_up, w_down):
    depth = w_in.shape[0]
    bsz = x.shape[0]
    ctx_row = bsz
    assert bsz < 8 and depth == DEPTH

    lb_cum = jnp.cumsum(jax.nn.softmax(hgrn_lb_logits.astype(F32), axis=0), axis=0)
    lbs = lb_cum - lb_cum[0]
    log1m_lb = jnp.log1p(-lbs)

    c_all = jnp.zeros((8, x.shape[2]), F32).at[:bsz].set(c).at[ctx_row].set(c_ctx)
    mod = _mod_call(c_all, w_mod, b_mod)

    p_index = P_INDEX
    for l in range(depth):
        need_ctx = l < depth - 1
        mod_l = mod[l]
        w_in_bf = w_in[l].astype(BF16)
        p_lat, lf_lat, kk_lat = _proj_call(x, mod_l, w_in_bf, lbs[l], log1m_lb[l], None, "proj_lat")
        p_ctx, lf_ctx, kk_ctx = _proj_call(ctx, mod_l, w_in_bf, lbs[l], log1m_lb[l], ctx_row,
                                           "proj_ctx")

        oc_f, oc_b, ol_f, ol_b = _hgrn_call(p_lat, lf_lat, kk_lat, p_ctx, lf_ctx, kk_ctx)
        h_lat, h_ctx = _lru_call(p_lat, p_ctx, p_index["lx"], p_index["lx"], conv_w[l], conv_b[l],
                                 lru_wa[l], lru_wx[l], lru_ba[l], lru_bx[l], lru_lambda[l])

        wa, wb, wo = (w.astype(BF16) for w in (w_branch_a[l], w_branch_b[l], w_out[l]))
        x1, h2, aff_t = _merge_call(ol_f, ol_b, p_lat, p_index, h_lat, x, mod_l, hgrn_norm_g[l],
                                         wa, wb, wo, ln_g[l], ln_b[l], w_router[l], None, "merge_lat")
        if need_ctx:
            c1, ch2, caff_t = _merge_call(oc_f, oc_b, p_ctx, p_index, h_ctx, ctx, mod_l, hgrn_norm_g[l],
                                                wa, wb, wo, ln_g[l], ln_b[l], w_router[l], ctx_row,
                                                "merge_ctx")
        x = _moe(x1, h2, aff_t, mod_l, w_gate, w_up, w_down, l, ln_g[l], ln_b[l], None, "lat")
        if need_ctx:
            ctx = _moe(c1, ch2, caff_t, mod_l, w_gate, w_up, w_down, l, ln_g[l], ln_b[l], ctx_row, "ctx")
    return x
```

```python
import functools

import jax
import jax.numpy as jnp
from jax import lax
from jax.experimental import pallas as pl
from jax.experimental.pallas import tpu as pltpu

F32 = jnp.float32
BF16 = jnp.bfloat16
I32 = jnp.int32
HIGHEST = lax.Precision.HIGHEST

DEPTH = 2
D_MODEL = 1024
GRID_W = 64
N_HEADS = 8
HEAD_W = 128
CHUNK = 64
N_EXPERTS = 16
EC_FACTOR = 2
LRU_C = 8.0
ALPHA = (2.0 * DEPTH) ** 0.25
LN_EPS = 1e-5
RMS_EPS = 1e-6
LN2 = 0.6931471805599453
LOG2E = 1.4426950408889634
EXP_CLAMP = 80.0

V7X_VMEM_LIMIT_BYTES = 56 * 1024 * 1024
OFF_STRIDE = 128
GATHER_WINDOW_TILES = 6
COMBINE_GROUP = 8
GATHER_ROWS = 128
SCATTER_ROWS = 64

G_Q, G_V, G_FF, G_FB, G_OG, G_LX, G_LY, G_MA, G_MB = range(9)


def _params(*sem):
    return pltpu.CompilerParams(dimension_semantics=sem, vmem_limit_bytes=V7X_VMEM_LIMIT_BYTES)


def _sigmoid(x):
    return 0.5 * jnp.tanh(0.5 * x) + 0.5


def _layer_norm(x, g, b):
    mu = jnp.mean(x, axis=-1, keepdims=True)
    xc = x - mu
    var = jnp.mean(xc * xc, axis=-1, keepdims=True)
    return xc * lax.rsqrt(var + LN_EPS) * g + b


def _mod_kernel(c_ref, w_ref, b_ref, o_ref):
    c = c_ref[...]
    s = c * _sigmoid(c)
    o_ref[0] = jnp.dot(s, w_ref[0], preferred_element_type=F32, precision=HIGHEST) + b_ref[0]


def _mod_call(c_all, w_mod, b_mod):
    depth, d, n = w_mod.shape
    tn = 1536
    return pl.pallas_call(
        _mod_kernel,
        grid=(depth, n // tn),
        in_specs=[
            pl.BlockSpec((8, d), lambda l, j: (0, 0)),
            pl.BlockSpec((1, d, tn), lambda l, j: (l, 0, j)),
            pl.BlockSpec((1, 1, tn), lambda l, j: (l, 0, j)),
        ],
        out_specs=pl.BlockSpec((1, 8, tn), lambda l, j: (l, 0, j)),
        out_shape=jax.ShapeDtypeStruct((depth, 8, n), F32),
        compiler_params=_params("arbitrary", "arbitrary"),
        name="mod",
    )(c_all, w_mod, b_mod.reshape(depth, 1, n))


def _apply_act(act, p):
    if act == "id":
        return p
    if act == "silu_scale":
        return p * _sigmoid(p) * (HEAD_W ** -0.5)
    if act == "silu":
        return p * _sigmoid(p)
    if act == "sigmoid":
        return _sigmoid(p)
    if act == "gelu":
        return 0.5 * p * (1.0 + jnp.tanh(0.7978845608028654 * (p + 0.044715 * (p * p * p))))
    raise ValueError(act)


P_GROUPS = (G_Q, G_V, G_OG, G_LX, G_LY, G_MA, G_MB)
P_ACTS = ("silu_scale", "id", "silu", "id", "gelu", "sigmoid", "sigmoid")
P_INDEX = {"q": 0, "v": 1, "og": 2, "lx": 3, "ly": 4, "ma": 5, "mb": 6}


def _hgrn_gates(fp, lb, l1):
    e = jnp.exp(-jnp.abs(fp))
    r = 1.0 / (1.0 + e)
    sig = jnp.where(fp >= 0.0, r, e * r)
    k = (1.0 - lb) * jnp.where(fp >= 0.0, e * r, r)
    log_f = jnp.maximum(jnp.log(lb + (1.0 - lb) * sig), l1 + jnp.minimum(fp, 0.0) - LN2)
    return log_f, k


def _proj_kernel(x_ref, mod_ref, w_ref, lb_ref, l1_ref, p_ref, lf_ref, kk_ref, *, ctx_row):
    d = D_MODEL
    b = pl.program_id(0)
    row = b if ctx_row is None else ctx_row
    sh = mod_ref[pl.ds(row, 1), 0:d]
    sc = mod_ref[pl.ds(row, 1), d:2 * d]
    h = (x_ref[0] * (1.0 + sc) + sh).astype(BF16)

    def group(col):
        return jnp.dot(h, w_ref[:, col * d:(col + 1) * d], preferred_element_type=F32)

    def plain(name):
        g = P_INDEX[name]
        p_ref[g, 0] = _apply_act(P_ACTS[g], group(P_GROUPS[g])).astype(BF16)

    def gate(dirn, col):
        log_f, k = _hgrn_gates(group(col), lb_ref[dirn:dirn + 1, :], l1_ref[dirn:dirn + 1, :])
        lf_ref[dirn, 0] = log_f
        kk_ref[dirn, 0] = k.astype(BF16)

    gate(0, G_FF)
    plain("v")
    gate(1, G_FB)
    plain("lx")
    plain("q")
    plain("ly")
    plain("og")
    plain("ma")
    plain("mb")


def _proj_call(x, mod_l, w_in_bf, lb, l1, ctx_row, name):
    bsz, t, d = x.shape
    tm = min(t, 512)
    ng = len(P_GROUPS)
    full = lambda shape: pl.BlockSpec(shape, lambda b, i: tuple(0 for _ in shape))
    out = lambda n: pl.BlockSpec((n, 1, tm, d), lambda b, i: (0, b, i, 0))
    return pl.pallas_call(
        functools.partial(_proj_kernel, ctx_row=ctx_row),
        grid=(bsz, t // tm),
        in_specs=[
            pl.BlockSpec((1, tm, d), lambda b, i: (b, i, 0)),
            full((8, 6 * d)),
            pl.BlockSpec(w_in_bf.shape, lambda b, i: (0, 0), pipeline_mode=pl.Buffered(1)),
            full((2, d)), full((2, d)),
        ],
        out_specs=[out(ng), out(2), out(2)],
        out_shape=[jax.ShapeDtypeStruct((ng, bsz, t, d), BF16), jax.ShapeDtypeStruct((2, bsz, t, d), F32),
                   jax.ShapeDtypeStruct((2, bsz, t, d), BF16)],
        compiler_params=_params("arbitrary", "arbitrary"),
        name=name,
    )(x, mod_l, w_in_bf, lb, l1)


def _hgrn_prepare(q_ref, v_ref, lf_ref, k_ref, row0, n, reverse):
    w = HEAD_W
    rows = pl.ds(row0, n * CHUNK)
    log_f = lf_ref[0, 0, rows, :]
    k = k_ref[0, 0, rows, :].astype(F32)
    q = q_ref[0, 0, rows, :]
    v = v_ref[0, 0, rows, :]

    ri = lax.broadcasted_iota(I32, (CHUNK, CHUNK), 0)
    ci = lax.broadcasted_iota(I32, (CHUNK, CHUNK), 1)
    causal = (ci >= ri) if reverse else (ci <= ri)
    tri = causal.astype(BF16)
    lf = jnp.concatenate([log_f[c * CHUNK:(c + 1) * CHUNK, :] for c in range(n)], axis=1)
    hi = lf.astype(BF16)
    rest = lf - hi.astype(F32)
    mid = rest.astype(BF16)
    lo = (rest - mid.astype(F32)).astype(BF16)
    bc = (jnp.dot(tri, hi, preferred_element_type=F32) + jnp.dot(tri, mid, preferred_element_type=F32)
          + jnp.dot(tri, lo, preferred_element_type=F32))

    half = CHUNK // 2
    chunks = []
    for c in range(n):
        sl = slice(c * CHUNK, (c + 1) * CHUNK)
        bcum = bc[:, c * w:(c + 1) * w]
        b_ref = bcum[half:half + 1, :]
        b_end = bcum[0:1, :] if reverse else bcum[CHUNK - 1:CHUNK, :]
        e1 = jnp.exp(jnp.minimum(bcum - b_ref, EXP_CLAMP))
        e2 = jnp.exp(jnp.minimum(b_ref - bcum, EXP_CLAMP))
        qt = q[sl].astype(F32) * e1
        kt = k[sl] * e2
        att = lax.dot_general(qt.astype(BF16), kt.astype(BF16), (((1,), (1,)), ((), ())),
                              preferred_element_type=F32)
        att = jnp.where(causal, att, 0.0).astype(BF16)
        o_intra = jnp.dot(att, v[sl], preferred_element_type=F32)
        qs = (qt * jnp.exp(b_ref)).astype(BF16)
        ke = (kt * jnp.exp(b_end - b_ref)).astype(BF16)
        upd = lax.dot_general(v[sl], ke, (((0,), (0,)), ((), ())), preferred_element_type=F32)
        chunks.append((o_intra, qs, upd, jnp.exp(b_end)))
    return chunks


def _hgrn_state_step(chunk, st, o_ref, row):
    o_intra, qs, upd, decay = chunk
    o_ref[0, 0, pl.ds(row, CHUNK), :] = o_intra + lax.dot_general(
        qs, st.astype(BF16), (((1,), (1,)), ((), ())), preferred_element_type=F32)
    return decay * st + upd


HGRN_SUB = 16


def _hgrn_kernel(qc_ref, vc_ref, lcf_ref, lcb_ref, kcf_ref, kcb_ref,
                 qf_ref, vf_ref, lff_ref, kf_ref, qb_ref, vb_ref, lfb_ref, kb_ref,
                 ocf_ref, ocb_ref, of_ref, ob_ref, sf_ref, sb_ref, *, n_ctx_chunks, n_lat_chunks):
    s = pl.program_id(2)

    def run(n_chunks, fwd, o_f, bwd, o_b):
        sub = min(n_chunks, HGRN_SUB)
        n_sub = n_chunks // sub

        def body(i, carry):
            rf = pl.multiple_of(i * (sub * CHUNK), sub * CHUNK)
            rb = pl.multiple_of((n_sub - 1 - i) * (sub * CHUNK), sub * CHUNK)
            cf = _hgrn_prepare(*fwd, rf, sub, False)
            cb = _hgrn_prepare(*bwd, rb, sub, True)
            st_f, st_b = sf_ref[...], sb_ref[...]
            for c in range(sub):
                st_f = _hgrn_state_step(cf[c], st_f, o_f, rf + c * CHUNK)
                st_b = _hgrn_state_step(cb[sub - 1 - c], st_b, o_b, rb + (sub - 1 - c) * CHUNK)
            sf_ref[...] = st_f
            sb_ref[...] = st_b
            return carry
        lax.fori_loop(0, n_sub, body, 0)

    @pl.when(s == 0)
    def _():
        sf_ref[...] = jnp.zeros_like(sf_ref)
        sb_ref[...] = jnp.zeros_like(sb_ref)
        run(n_ctx_chunks, (qc_ref, vc_ref, lcf_ref, kcf_ref), ocf_ref, (qc_ref, vc_ref, lcb_ref, kcb_ref), ocb_ref)

    run(n_lat_chunks, (qf_ref, vf_ref, lff_ref, kf_ref), of_ref, (qb_ref, vb_ref, lfb_ref, kb_ref), ob_ref)


def _hgrn_call(p_lat, lf_lat, kk_lat, p_ctx, lf_ctx, kk_ctx):
    _, bsz, t, d = p_lat.shape
    tc = p_ctx.shape[2]
    w = HEAD_W
    blk = min(t, 1024)
    ns = t // blk
    lat = lambda g, rev: pl.BlockSpec(
        (1, 1, blk, w), (lambda b, h, s: (g, b, ns - 1 - s, h)) if rev else (lambda b, h, s: (g, b, s, h)))
    ctx = lambda g: pl.BlockSpec((1, 1, tc, w), lambda b, h, s: (g, b, 0, h))
    return pl.pallas_call(
        functools.partial(_hgrn_kernel, n_ctx_chunks=tc // CHUNK, n_lat_chunks=blk // CHUNK),
        grid=(bsz, N_HEADS, ns),
        in_specs=[ctx(0), ctx(1), ctx(0), ctx(1), ctx(0), ctx(1),
                  lat(0, False), lat(1, False), lat(0, False), lat(0, False),
                  lat(0, True), lat(1, True), lat(1, True), lat(1, True)],
        out_specs=[ctx(0), ctx(0), lat(0, False), lat(0, True)],
        out_shape=[jax.ShapeDtypeStruct((1, bsz, tc, d), F32), jax.ShapeDtypeStruct((1, bsz, tc, d), F32),
                   jax.ShapeDtypeStruct((1, bsz, t, d), F32), jax.ShapeDtypeStruct((1, bsz, t, d), F32)],
        scratch_shapes=[pltpu.VMEM((w, w), F32), pltpu.VMEM((w, w), F32)],
        compiler_params=_params("arbitrary", "arbitrary", "arbitrary"),
        name="hgrn",
    )(p_ctx, p_ctx, lf_ctx, lf_ctx, kk_ctx, kk_ctx,
      p_lat, p_lat, lf_lat, kk_lat, p_lat, p_lat, lf_lat, kk_lat)


def _lru_gates(xc, w_all, hb_all, c2_ref):
    w = HEAD_W
    th = jnp.tanh(jnp.dot(xc.astype(BF16), w_all, preferred_element_type=F32) + hb_all)
    out = []
    for dirn in range(2):
        t_r = th[:, (2 * dirn) * w:(2 * dirn + 1) * w]
        t_i = th[:, (2 * dirn + 1) * w:(2 * dirn + 2) * w]
        c2 = c2_ref[dirn:dirn + 1, :]
        a = jnp.exp2(c2 * t_r + c2)
        u = jnp.sqrt(1.0 - a * a) * ((0.5 * t_i + 0.5) * xc)
        out += [a, u]
    return out


def _lru_kernel(xl_ref, xc_ref, cw_ref, cb_ref, wa_ref, wx_ref, ba_ref, bx_ref, lam_ref,
                hl_ref, hc_ref,
                xpad, cpad, a_f, u_f, a_b, u_b, ca_f, cu_f, ca_b, cu_b,
                ends_f, prods_f, carry_f, ends_b, prods_b, carry_b, c2_ref,
                *, t, tc):
    w = HEAD_W
    gw = GRID_W
    rows = t // gw
    lam = lam_ref[...]
    neg = -lam
    softplus = jnp.maximum(neg, 0.0) + jnp.log(1.0 + jnp.exp(-jnp.abs(neg)))
    c2_ref[...] = (-0.5 * LRU_C * LOG2E) * softplus
    w_all = (0.5 * jnp.concatenate([wa_ref[0, 0], wx_ref[0, 0], wa_ref[1, 0], wx_ref[1, 0]], axis=1)).astype(BF16)
    hb_all = 0.5 * jnp.concatenate([ba_ref[0:1, :], bx_ref[0:1, :], ba_ref[1:2, :], bx_ref[1:2, :]], axis=1)
    w0, w1, w2, w3 = (cw_ref[k:k + 1, :] for k in range(4))
    cb = cb_ref[...]

    cpad[...] = jnp.zeros_like(cpad)
    cpad[8:8 + tc, :] = xc_ref[0, 0].astype(F32)
    xcc = (w0 * cpad[6:6 + tc, :] + w1 * cpad[7:7 + tc, :] + w2 * cpad[8:8 + tc, :]
           + w3 * cpad[9:9 + tc, :] + cb)
    ca_f[...], cu_f[...], ca_b[...], cu_b[...] = _lru_gates(xcc, w_all, hb_all, c2_ref)

    def ctx_step(i, hs):
        hf, hb = hs
        p = tc - 1 - i
        hf = ca_f[pl.ds(i, 1), :] * hf + cu_f[pl.ds(i, 1), :]
        hb = ca_b[pl.ds(p, 1), :] * hb + cu_b[pl.ds(p, 1), :]
        cu_f[pl.ds(i, 1), :] = hf
        cu_b[pl.ds(p, 1), :] = hb
        return hf, hb
    zero_row = jnp.zeros((1, w), F32)
    s0_f, s0_b = lax.fori_loop(0, tc, ctx_step, (zero_row, zero_row))
    hc_ref[0] = (cu_f[...] + cu_b[...]).astype(hc_ref.dtype)

    xpad[2 * gw:2 * gw + t, :] = xl_ref[0, 0].astype(F32)
    col = lax.broadcasted_iota(I32, (gw, w), 0)
    body0 = 2 * gw
    xpad[gw:2 * gw, :] = jnp.where(col == 0, 0.0, xpad[pl.ds(body0 + (rows - 1) * gw - 1, gw), :])
    xpad[0:gw, :] = jnp.where(col == 0, 0.0, xpad[pl.ds(body0 + (rows - 2) * gw - 1, gw), :])
    xpad[body0 + t:body0 + t + gw, :] = jnp.where(col == gw - 1, 0.0, xpad[pl.ds(body0 + 1, gw), :])

    gate_rows = min(t, 512)

    def gate_body(c, carry_):
        base = pl.multiple_of(c * gate_rows, gate_rows)
        xcv = (w0 * xpad[pl.ds(base, gate_rows), :] + w1 * xpad[pl.ds(base + gw, gate_rows), :]
               + w2 * xpad[pl.ds(base + 2 * gw, gate_rows), :]
               + w3 * xpad[pl.ds(base + 3 * gw, gate_rows), :] + cb)
        sl = pl.ds(base, gate_rows)
        a_f[sl, :], u_f[sl, :], a_b[sl, :], u_b[sl, :] = _lru_gates(xcv, w_all, hb_all, c2_ref)
        return carry_
    lax.fori_loop(0, t // gate_rows, gate_body, 0)

    def slab(r):
        return pl.ds(pl.multiple_of(r * gw, gw), gw)

    def p1(i, c):
        hf, pf, hb, pb = c
        sf, sb = slab(i), slab(rows - 1 - i)
        af, ab = a_f[sf, :], a_b[sb, :]
        return af * hf + u_f[sf, :], af * pf, ab * hb + u_b[sb, :], ab * pb
    zeros, ones = jnp.zeros((gw, w), F32), jnp.ones((gw, w), F32)
    ends_f[...], prods_f[...], ends_b[...], prods_b[...] = lax.fori_loop(0, rows, p1, (zeros, ones, zeros, ones))

    def chain(i, c):
        cf, cb = c
        jf, jb = pl.ds(i, 1), pl.ds(gw - 1 - i, 1)
        carry_f[jf, :] = cf
        carry_b[jb, :] = cb
        return prods_f[jf, :] * cf + ends_f[jf, :], prods_b[jb, :] * cb + ends_b[jb, :]
    lax.fori_loop(0, gw, chain, (s0_f, s0_b))

    def p2(i, c):
        hf, hb = c
        sf, sb = slab(i), slab(rows - 1 - i)
        hf = a_f[sf, :] * hf + u_f[sf, :]
        hb = a_b[sb, :] * hb + u_b[sb, :]
        u_f[sf, :] = hf
        u_b[sb, :] = hb
        return hf, hb
    lax.fori_loop(0, rows, p2, (carry_f[...], carry_b[...]))
    hl_ref[0] = (u_f[...] + u_b[...]).astype(hl_ref.dtype)


def _lru_call(p_lat, p_ctx, g_lat, g_ctx, conv_w, conv_b, wa, wx, ba, bx, lam):
    _, bsz, t, d = p_lat.shape
    tc = p_ctx.shape[2]
    w = HEAD_W
    nb = d // w
    gw = GRID_W
    vec = lambda n: pl.BlockSpec((n, w), lambda b, k: (0, k))
    mat = pl.BlockSpec((2, 1, w, w), lambda b, k: (0, k, 0, 0))
    return pl.pallas_call(
        functools.partial(_lru_kernel, t=t, tc=tc),
        grid=(bsz, nb),
        in_specs=[
            pl.BlockSpec((1, 1, t, w), lambda b, k: (g_lat, b, 0, k)),
            pl.BlockSpec((1, 1, tc, w), lambda b, k: (g_ctx, b, 0, k)),
            vec(4), vec(1), mat, mat, vec(2), vec(2), vec(2),
        ],
        out_specs=[pl.BlockSpec((1, t, w), lambda b, k: (b, 0, k)),
                   pl.BlockSpec((1, tc, w), lambda b, k: (b, 0, k))],
        out_shape=[jax.ShapeDtypeStruct((bsz, t, d), BF16), jax.ShapeDtypeStruct((bsz, tc, d), BF16)],
        scratch_shapes=[
            pltpu.VMEM((t + 3 * gw, w), F32), pltpu.VMEM((tc + 16, w), F32),
            pltpu.VMEM((t, w), F32), pltpu.VMEM((t, w), F32), pltpu.VMEM((t, w), F32), pltpu.VMEM((t, w), F32),
            pltpu.VMEM((tc, w), F32), pltpu.VMEM((tc, w), F32), pltpu.VMEM((tc, w), F32), pltpu.VMEM((tc, w), F32),
            pltpu.VMEM((gw, w), F32), pltpu.VMEM((gw, w), F32), pltpu.VMEM((gw, w), F32),
            pltpu.VMEM((gw, w), F32), pltpu.VMEM((gw, w), F32), pltpu.VMEM((gw, w), F32),
            pltpu.VMEM((2, w), F32),
        ],
        compiler_params=_params("arbitrary", "arbitrary"),
        name="rglru",
    )(p_lat, p_ctx, conv_w, conv_b.reshape(1, d), wa, wx, ba, bx, lam)


def _merge_kernel(of_ref, ob_ref, og_ref, hl_ref, ly_ref, ma_ref, mb_ref, x_ref, mod_ref, gn_ref,
                  wa_ref, wb_ref, wo_ref, lng_ref, lnb_ref, wrt_ref,
                  x1_ref, h2_ref, afft_ref, *, ctx_row):
    d = D_MODEL
    b = pl.program_id(0)
    row = b if ctx_row is None else ctx_row
    o = of_ref[0, 0] + ob_ref[0, 0]
    gn = gn_ref[...]
    parts = []
    for h in range(N_HEADS):
        oh = o[:, h * HEAD_W:(h + 1) * HEAD_W]
        ms = jnp.mean(oh * oh, axis=-1, keepdims=True)
        parts.append(oh * lax.rsqrt(ms + RMS_EPS) * gn)
    o_a = (jnp.concatenate(parts, axis=1) * og_ref[0, 0].astype(F32)).astype(BF16)
    y_a = jnp.dot(o_a, wa_ref[...], preferred_element_type=F32)
    y_b = jnp.dot((hl_ref[0].astype(F32) * ly_ref[0, 0].astype(F32)).astype(BF16), wb_ref[...],
                  preferred_element_type=F32)
    z = ma_ref[0, 0].astype(F32) * y_a + mb_ref[0, 0].astype(F32) * y_b
    y = jnp.dot(z.astype(BF16), wo_ref[...], preferred_element_type=F32)
    g1 = mod_ref[pl.ds(row, 1), 2 * d:3 * d]
    x1 = _layer_norm(ALPHA * x_ref[0] + g1 * y, lng_ref[0:1, :], lnb_ref[0:1, :])
    x1_ref[0] = x1
    sh2 = mod_ref[pl.ds(row, 1), 3 * d:4 * d]
    sc2 = mod_ref[pl.ds(row, 1), 4 * d:5 * d]
    h2 = x1 * (1.0 + sc2) + sh2
    h2_ref[0] = h2.astype(BF16)
    logits_t = lax.dot_general(wrt_ref[...], h2, (((1,), (1,)), ((), ())),
                               preferred_element_type=F32, precision=HIGHEST)
    pt = jnp.exp(logits_t - jnp.max(logits_t, axis=0, keepdims=True))
    afft_ref[0] = pt / jnp.sum(pt, axis=0, keepdims=True)


def _merge_call(o_f, o_b, p, groups, h_lru, x, mod_l, gn, wa, wb, wo, ln_g, ln_b, w_router, ctx_row, name):
    bsz, t, d = x.shape
    tm = min(t, 512)
    e = N_EXPERTS
    tile = lambda g: pl.BlockSpec((1, 1, tm, d), lambda b, i: (g, b, i, 0))
    tok = pl.BlockSpec((1, tm, d), lambda b, i: (b, i, 0))
    full = lambda shape: pl.BlockSpec(shape, lambda b, i: tuple(0 for _ in shape))
    return pl.pallas_call(
        functools.partial(_merge_kernel, ctx_row=ctx_row),
        grid=(bsz, t // tm),
        in_specs=[tile(0), tile(0), tile(groups["og"]), tok, tile(groups["ly"]), tile(groups["ma"]),
                  tile(groups["mb"]), tok, full((8, 6 * d)), full((1, HEAD_W)),
                  full((d, d)), full((d, d)), full((d, d)), full((2, d)), full((2, d)),
                  full((e, d))],
        out_specs=[tok, tok, pl.BlockSpec((1, e, tm), lambda b, i: (b, 0, i))],
        out_shape=[jax.ShapeDtypeStruct((bsz, t, d), F32), jax.ShapeDtypeStruct((bsz, t, d), BF16),
                   jax.ShapeDtypeStruct((bsz, e, t), F32)],
        compiler_params=_params("arbitrary", "arbitrary"),
        name=name,
    )(o_f, o_b, p, h_lru, p, p, p, x, mod_l, gn.reshape(1, HEAD_W), wa, wb, wo, ln_g, ln_b,
      w_router.T)


def _route_kernel(a_ref, posm_ref, off_ref, span_ref, *, cap, t, tt, grt):
    e = N_EXPERTS
    u = lax.bitcast_convert_type(a_ref[0], I32)
    thr = jnp.zeros((e, 1), I32)
    for bit in range(30, -1, -1):
        cand = thr | (1 << bit)
        cnt = jnp.sum((u >= cand).astype(F32), axis=1, keepdims=True)
        thr = jnp.where(cnt >= cap, cand, thr)
    gt = u > thr
    eq = u == thr
    need = cap - jnp.sum(gt.astype(F32), axis=1, keepdims=True)
    ri = lax.broadcasted_iota(I32, (tt, tt), 0)
    ci = lax.broadcasted_iota(I32, (tt, tt), 1)
    before = (ri < ci).astype(BF16)
    lane = lax.broadcasted_iota(I32, (e, OFF_STRIDE), 1)
    offs = jnp.zeros((e, OFF_STRIDE), F32)
    offs_end = jnp.zeros((e, OFF_STRIDE), F32)
    n_eq = jnp.zeros((e, 1), F32)
    n_sel = jnp.zeros((e, 1), F32)
    for j in range(t // tt):
        sl = slice(j * tt, (j + 1) * tt)
        eq_j = eq[:, sl]
        rank_eq = jnp.dot(eq_j.astype(BF16), before, preferred_element_type=F32) + n_eq
        sel_j = gt[:, sl] | (eq_j & (rank_eq < need))
        sel_f = sel_j.astype(F32)
        pos = jnp.dot(sel_j.astype(BF16), before, preferred_element_type=F32) + n_sel
        posm_ref[0, :, sl] = jnp.where(sel_j, pos.astype(I32), -1)
        offs = jnp.where(lane == j, n_sel, offs)
        n_eq = n_eq + jnp.sum(eq_j.astype(F32), axis=1, keepdims=True)
        n_sel = n_sel + jnp.sum(sel_f, axis=1, keepdims=True)
        offs_end = jnp.where(lane == j, n_sel, offs_end)
    nt = t // tt
    offs = jnp.where(lane == nt, n_sel, offs)
    off_ref[0] = offs.astype(I32)
    n_rt = cap // grt
    tile = lane < nt
    spans = jnp.zeros((e, OFF_STRIDE), F32)
    for i in range(n_rt):
        skipped = jnp.sum(jnp.where(tile & (offs_end <= i * grt), 1.0, 0.0), axis=1, keepdims=True)
        started = jnp.sum(jnp.where(tile & (offs < (i + 1) * grt), 1.0, 0.0), axis=1, keepdims=True)
        spans = jnp.where(lane == i, skipped, jnp.where(lane == n_rt + i, started, spans))
    span_ref[0] = spans.astype(I32)


def _route_call(aff_t, cap, tt, grt, name):
    bsz, e, t = aff_t.shape
    small = pl.BlockSpec((1, e, OFF_STRIDE), lambda b: (b, 0, 0))
    return pl.pallas_call(
        functools.partial(_route_kernel, cap=cap, t=t, tt=tt, grt=grt),
        grid=(bsz,),
        in_specs=[pl.BlockSpec((1, e, t), lambda b: (b, 0, 0))],
        out_specs=[pl.BlockSpec((1, e, t), lambda b: (b, 0, 0)), small, small],
        out_shape=[jax.ShapeDtypeStruct((bsz, e, t), I32), jax.ShapeDtypeStruct((bsz, e, OFF_STRIDE), I32),
                   jax.ShapeDtypeStruct((bsz, e, OFF_STRIDE), I32)],
        compiler_params=_params("arbitrary"),
        name=name,
    )(aff_t)


def _expert_kernel(span_ref, posm_ref, h_ref, wg_ref, wu_ref, wd_ref, y_ref, x_ref, *, cap, rt, tt, nt,
                   expert_major):
    b = pl.program_id(1 if expert_major else 0)
    e = pl.program_id(0 if expert_major else 1)
    base = (b * N_EXPERTS + e) * OFF_STRIDE
    x_ref[...] = jnp.zeros_like(x_ref)
    kt = min(nt, GATHER_WINDOW_TILES)
    kw = kt * tt
    rows = lax.broadcasted_iota(I32, (rt, kw), 0)
    lane = lax.broadcasted_iota(I32, (rt, kw), 1)
    n_rt = cap // rt

    for i in range(n_rt):
        first = span_ref[base + i]
        n_win = (span_ref[base + n_rt + i] - 1 - first) // kt + 1

        def body(wdx, carry, i=i, first=first):
            want = first + wdx * kt
            start = jnp.minimum(want, nt - kt)
            tok = pl.ds(pl.multiple_of(start * tt, tt), kw)
            hit = ((posm_ref[0, 0, :, tok] - i * rt) == rows) & (lane >= (want - start) * tt)
            x_ref[i * rt:(i + 1) * rt, :] += jnp.dot(hit.astype(BF16), h_ref[0, tok, :],
                                                     preferred_element_type=F32).astype(BF16)
            return carry
        lax.fori_loop(0, n_win, body, 0)
    wg, wu, wd = wg_ref[0, 0].astype(BF16), wu_ref[0, 0].astype(BF16), wd_ref[0, 0].astype(BF16)
    ft = min(cap, 256)
    for r in range(cap // ft):
        xb = x_ref[r * ft:(r + 1) * ft, :]
        g = jnp.dot(xb, wg, preferred_element_type=F32)
        u = jnp.dot(xb, wu, preferred_element_type=F32)
        hid = (g * _sigmoid(g) * u).astype(BF16)
        y_ref[0, 0, r * ft:(r + 1) * ft, :] = jnp.dot(hid, wd, preferred_element_type=F32).astype(BF16)


def _expert_call(span_flat, posm, h2, w_gate, w_up, w_down, layer, cap, rt, tt, name):
    bsz, t, d = h2.shape
    _, e, _, f = w_gate.shape
    nt = t // tt
    expert_major = t < d
    if expert_major:
        grid = (e, bsz)
        be = lambda k, b: (b, k)
    else:
        grid = (bsz, e)
        be = lambda b, k: (b, k)
    tok_spec = pl.BlockSpec((1, t, d), lambda i, j, off: (be(i, j)[0], 0, 0),
                            pipeline_mode=None if expert_major else pl.Buffered(1))
    weight = lambda r, c: pl.BlockSpec((1, 1, r, c), lambda i, j, off: (layer, be(i, j)[1], 0, 0))
    grid_spec = pltpu.PrefetchScalarGridSpec(
        num_scalar_prefetch=1,
        grid=grid,
        in_specs=[
            pl.BlockSpec((1, 1, 1, t), lambda i, j, off: (*be(i, j), 0, 0)),
            tok_spec, weight(d, f), weight(d, f), weight(f, d),
        ],
        out_specs=pl.BlockSpec((1, 1, cap, d), lambda i, j, off: (*be(i, j), 0, 0)),
        scratch_shapes=[pltpu.VMEM((cap, d), BF16)],
    )
    return pl.pallas_call(
        functools.partial(_expert_kernel, cap=cap, rt=rt, tt=tt, nt=nt, expert_major=expert_major),
        grid_spec=grid_spec,
        out_shape=jax.ShapeDtypeStruct((bsz, e, cap, d), BF16),
        compiler_params=_params("arbitrary", "arbitrary"),
        name=name,
    )(span_flat, posm.reshape(bsz, e, 1, t), h2, w_gate, w_up, w_down)


def _combine_kernel(off_ref, posm_ref, afft_ref, x1_ref, y_hbm, mod_ref, lng_ref, lnb_ref, o_ref,
                    ybuf, pbuf, sem, acc_ref, *, rt, tt, nw, ctx_row):
    d = D_MODEL
    b = pl.program_id(0)
    j = pl.program_id(1)
    row = b if ctx_row is None else ctx_row
    shift = rt.bit_length() - 1

    @pl.when((b == 0) & (j == 0))
    def _():
        ybuf[...] = jnp.zeros_like(ybuf)

    def window(e):
        base = (b * N_EXPERTS + e) * OFF_STRIDE
        lo = off_ref[base + j]
        hi = off_ref[base + j + 1]
        t0 = lax.shift_right_logical(lo, shift)
        n = jnp.where(hi > lo, lax.shift_right_logical(hi - 1, shift) - t0 + 1, 0)
        return t0, n

    windows, slot0 = [], []
    n_used = jnp.int32(0)
    for e in range(N_EXPERTS):
        windows.append(window(e))
        slot0.append(n_used)
        n_used = n_used + windows[e][1]
    n_groups = lax.shift_right_logical(n_used + (COMBINE_GROUP - 1), COMBINE_GROUP.bit_length() - 1)

    def slot_rows(e, wdx):
        return pl.ds(pl.multiple_of((slot0[e] + wdx) * rt, rt), rt)

    def copy(e, wdx):
        src = y_hbm.at[b, e, pl.ds(pl.multiple_of((windows[e][0] + wdx) * rt, rt), rt), :]
        return pltpu.make_async_copy(src, ybuf.at[slot_rows(e, wdx), :], sem.at[e, wdx])

    def for_used_windows(fn):
        for e in range(N_EXPERTS):
            for wdx in range(nw):
                pl.when(wdx < windows[e][1])(functools.partial(fn, e, wdx))

    for_used_windows(lambda e, wdx: copy(e, wdx).start())

    def clear(s, carry):
        pbuf[pl.ds(pl.multiple_of(s * rt, rt), rt), :] = jnp.zeros((rt, tt), BF16)
        return carry
    lax.fori_loop(n_used, n_groups * COMBINE_GROUP, clear, 0)

    rows = lax.broadcasted_iota(I32, (rt, tt), 0)

    def weights(e, wdx):
        pos = posm_ref[0, e]
        gate = afft_ref[0, e:e + 1, :]
        hit = (pos - (windows[e][0] + wdx) * rt) == rows
        pbuf[slot_rows(e, wdx), :] = jnp.where(hit, gate, 0.0).astype(BF16)
    for_used_windows(weights)
    for_used_windows(lambda e, wdx: copy(e, wdx).wait())

    acc_ref[...] = jnp.zeros_like(acc_ref)
    gk = COMBINE_GROUP * rt
    for g in range(pl.cdiv(N_EXPERTS * nw, COMBINE_GROUP)):
        @pl.when(g < n_groups)
        def _(g=g):
            acc_ref[...] += lax.dot_general(pbuf[g * gk:(g + 1) * gk, :], ybuf[g * gk:(g + 1) * gk, :],
                                            (((0,), (0,)), ((), ())), preferred_element_type=F32)
    g2 = mod_ref[pl.ds(row, 1), 5 * d:6 * d]
    o_ref[0] = _layer_norm(ALPHA * x1_ref[0] + g2 * acc_ref[...], lng_ref[1:2, :], lnb_ref[1:2, :])


def _combine_call(off_flat, posm, aff_t, x1, y, mod_l, ln_g, ln_b, rt, tt, ctx_row, name):
    bsz, t, d = x1.shape
    e = N_EXPERTS
    cap = y.shape[2]
    nw = min(cap // rt, (tt - 1) // rt + 2)
    n_slots = pl.cdiv(e * nw, COMBINE_GROUP) * COMBINE_GROUP
    grid_spec = pltpu.PrefetchScalarGridSpec(
        num_scalar_prefetch=1,
        grid=(bsz, t // tt),
        in_specs=[
            pl.BlockSpec((1, e, 1, tt), lambda b, j, off: (b, 0, 0, j)),
            pl.BlockSpec((1, e, tt), lambda b, j, off: (b, 0, j)),
            pl.BlockSpec((1, tt, d), lambda b, j, off: (b, j, 0)),
            pl.BlockSpec(memory_space=pl.ANY),
            pl.BlockSpec((8, 6 * d), lambda b, j, off: (0, 0)),
            pl.BlockSpec((2, d), lambda b, j, off: (0, 0)),
            pl.BlockSpec((2, d), lambda b, j, off: (0, 0)),
        ],
        out_specs=pl.BlockSpec((1, tt, d), lambda b, j, off: (b, j, 0)),
        scratch_shapes=[pltpu.VMEM((n_slots * rt, d), BF16), pltpu.VMEM((n_slots * rt, tt), BF16),
                        pltpu.SemaphoreType.DMA((e, nw)), pltpu.VMEM((tt, d), F32)],
    )
    return pl.pallas_call(
        functools.partial(_combine_kernel, rt=rt, tt=tt, nw=nw, ctx_row=ctx_row),
        grid_spec=grid_spec,
        out_shape=jax.ShapeDtypeStruct((bsz, t, d), F32),
        compiler_params=_params("arbitrary", "arbitrary"),
        name=name,
    )(off_flat, posm.reshape(bsz, e, 1, t), aff_t, x1, y, mod_l, ln_g, ln_b)


def _moe(x1, h2, aff_t, mod_l, w_gate, w_up, w_down, layer, ln_g, ln_b, ctx_row, tag):
    bsz, t, d = x1.shape
    cap = EC_FACTOR * t // N_EXPERTS
    tt = min(t, 256)
    grt = min(cap, GATHER_ROWS)
    posm, off, span = _route_call(aff_t, cap, tt, grt, "route_" + tag)
    y = _expert_call(span.reshape(-1), posm, h2, w_gate, w_up, w_down, layer, cap, grt, tt, "expert_" + tag)
    return _combine_call(off.reshape(-1), posm, aff_t, x1, y, mod_l, ln_g, ln_b, min(cap, SCATTER_ROWS), tt, ctx_row,
                         "combine_" + tag)


def kernel(x, c, ctx, c_ctx, w_mod, b_mod, w_in, hgrn_lb_logits, hgrn_norm_g, conv_w, conv_b, lru_wa,
           lru_ba, lru_wx, lru_bx, lru_lambda, w_branch_a, w_branch_b, w_out, ln_g, ln_b, w_router,
           w_gate, w_up, w_down):
    depth = w_in.shape[0]
    bsz = x.shape[0]
    ctx_row = bsz
    assert bsz < 8 and depth == DEPTH

    lb_cum = jnp.cumsum(jax.nn.softmax(hgrn_lb_logits.astype(F32), axis=0), axis=0)
    lbs = lb_cum - lb_cum[0]
    log1m_lb = jnp.log1p(-lbs)

    c_all = jnp.zeros((8, x.shape[2]), F32).at[:bsz].set(c).at[ctx_row].set(c_ctx)
    mod = _mod_call(c_all, w_mod, b_mod)

    p_index = P_INDEX
    for l in range(depth):
        need_ctx = l < depth - 1
        mod_l = mod[l]
        w_in_bf = w_in[l].astype(BF16)
        p_lat, lf_lat, kk_lat = _proj_call(x, mod_l, w_in_bf, lbs[l], log1m_lb[l], None, "proj_lat")
        p_ctx, lf_ctx, kk_ctx = _proj_call(ctx, mod_l, w_in_bf, lbs[l], log1m_lb[l], ctx_row,
                                           "proj_ctx")

        oc_f, oc_b, ol_f, ol_b = _hgrn_call(p_lat, lf_lat, kk_lat, p_ctx, lf_ctx, kk_ctx)
        h_lat, h_ctx = _lru_call(p_lat, p_ctx, p_index["lx"], p_index["lx"], conv_w[l], conv_b[l],
                                 lru_wa[l], lru_wx[l], lru_ba[l], lru_bx[l], lru_lambda[l])

        wa, wb, wo = (w.astype(BF16) for w in (w_branch_a[l], w_branch_b[l], w_out[l]))
        x1, h2, aff_t = _merge_call(ol_f, ol_b, p_lat, p_index, h_lat, x, mod_l, hgrn_norm_g[l],
                                         wa, wb, wo, ln_g[l], ln_b[l], w_router[l], None, "merge_lat")
        if need_ctx:
            c1, ch2, caff_t = _merge_call(oc_f, oc_b, p_ctx, p_index, h_ctx, ctx, mod_l, hgrn_norm_g[l],
                                                wa, wb, wo, ln_g[l], ln_b[l], w_router[l], ctx_row,
                                                "merge_ctx")
        x = _moe(x1, h2, aff_t, mod_l, w_gate, w_up, w_down, l, ln_g[l], ln_b[l], None, "lat")
        if need_ctx:
            ctx = _moe(c1, ch2, caff_t, mod_l, w_gate, w_up, w_down, l, ln_g[l], ln_b[l], ctx_row, "ctx")
    return x
```

```python
import functools

import jax
import jax.numpy as jnp
from jax import lax
from jax.experimental import pallas as pl
from jax.experimental.pallas import tpu as pltpu

F32 = jnp.float32
BF16 = jnp.bfloat16
I32 = jnp.int32
HIGHEST = lax.Precision.HIGHEST

DEPTH = 2
D_MODEL = 1024
GRID_W = 64
N_HEADS = 8
HEAD_W = 128
CHUNK = 64
N_EXPERTS = 16
EC_FACTOR = 2
LRU_C = 8.0
ALPHA = (2.0 * DEPTH) ** 0.25
LN_EPS = 1e-5
RMS_EPS = 1e-6
LN2 = 0.6931471805599453
LOG2E = 1.4426950408889634
EXP_CLAMP = 80.0

V7X_VMEM_LIMIT_BYTES = 56 * 1024 * 1024
OFF_STRIDE = 128
GATHER_WINDOW_TILES = 6
COMBINE_GROUP = 8
GATHER_ROWS = 128
SCATTER_ROWS = 64

G_Q, G_V, G_FF, G_FB, G_OG, G_LX, G_LY, G_MA, G_MB = range(9)


def _params(*sem):
    return pltpu.CompilerParams(dimension_semantics=sem, vmem_limit_bytes=V7X_VMEM_LIMIT_BYTES)


def _sigmoid(x):
    return 0.5 * jnp.tanh(0.5 * x) + 0.5


def _layer_norm(x, g, b):
    mu = jnp.mean(x, axis=-1, keepdims=True)
    xc = x - mu
    var = jnp.mean(xc * xc, axis=-1, keepdims=True)
    return xc * lax.rsqrt(var + LN_EPS) * g + b


def _mod_kernel(c_ref, w_ref, b_ref, o_ref):
    c = c_ref[...]
    s = c * _sigmoid(c)
    o_ref[0] = jnp.dot(s, w_ref[0], preferred_element_type=F32, precision=HIGHEST) + b_ref[0]


def _mod_call(c_all, w_mod, b_mod):
    depth, d, n = w_mod.shape
    tn = 1536
    return pl.pallas_call(
        _mod_kernel,
        grid=(depth, n // tn),
        in_specs=[
            pl.BlockSpec((8, d), lambda l, j: (0, 0)),
            pl.BlockSpec((1, d, tn), lambda l, j: (l, 0, j)),
            pl.BlockSpec((1, 1, tn), lambda l, j: (l, 0, j)),
        ],
        out_specs=pl.BlockSpec((1, 8, tn), lambda l, j: (l, 0, j)),
        out_shape=jax.ShapeDtypeStruct((depth, 8, n), F32),
        compiler_params=_params("arbitrary", "arbitrary"),
        name="mod",
    )(c_all, w_mod, b_mod.reshape(depth, 1, n))


def _apply_act(act, p):
    if act == "id":
        return p
    if act == "silu_scale":
        return p * _sigmoid(p) * (HEAD_W ** -0.5)
    if act == "silu":
        return p * _sigmoid(p)
    if act == "sigmoid":
        return _sigmoid(p)
    if act == "gelu":
        return 0.5 * p * (1.0 + jnp.tanh(0.7978845608028654 * (p + 0.044715 * (p * p * p))))
    raise ValueError(act)


P_GROUPS = (G_Q, G_V, G_OG, G_LX, G_LY, G_MA, G_MB)
P_ACTS = ("silu_scale", "id", "silu", "id", "gelu", "sigmoid", "sigmoid")
P_INDEX = {"q": 0, "v": 1, "og": 2, "lx": 3, "ly": 4, "ma": 5, "mb": 6}


def _hgrn_gates(fp, lb, l1):
    e = jnp.exp(-jnp.abs(fp))
    r = 1.0 / (1.0 + e)
    sig = jnp.where(fp >= 0.0, r, e * r)
    k = (1.0 - lb) * jnp.where(fp >= 0.0, e * r, r)
    log_f = jnp.maximum(jnp.log(lb + (1.0 - lb) * sig), l1 + jnp.minimum(fp, 0.0) - LN2)
    return log_f, k


def _proj_kernel(x_ref, mod_ref, w_ref, lb_ref, l1_ref, p_ref, lf_ref, kk_ref, *, ctx_row):
    d = D_MODEL
    b = pl.program_id(0)
    row = b if ctx_row is None else ctx_row
    sh = mod_ref[pl.ds(row, 1), 0:d]
    sc = mod_ref[pl.ds(row, 1), d:2 * d]
    h = (x_ref[0] * (1.0 + sc) + sh).astype(BF16)

    def group(col):
        return jnp.dot(h, w_ref[:, col * d:(col + 1) * d], preferred_element_type=F32)

    def plain(name):
        g = P_INDEX[name]
        p_ref[g, 0] = _apply_act(P_ACTS[g], group(P_GROUPS[g])).astype(BF16)

    def gate(dirn, col):
        log_f, k = _hgrn_gates(group(col), lb_ref[dirn:dirn + 1, :], l1_ref[dirn:dirn + 1, :])
        lf_ref[dirn, 0] = log_f
        kk_ref[dirn, 0] = k.astype(BF16)

    gate(0, G_FF)
    plain("v")
    gate(1, G_FB)
    plain("lx")
    plain("q")
    plain("ly")
    plain("og")
    plain("ma")
    plain("mb")


def _proj_call(x, mod_l, w_in_bf, lb, l1, ctx_row, name):
    bsz, t, d = x.shape
    tm = min(t, 512)
    ng = len(P_GROUPS)
    full = lambda shape: pl.BlockSpec(shape, lambda b, i: tuple(0 for _ in shape))
    out = lambda n: pl.BlockSpec((n, 1, tm, d), lambda b, i: (0, b, i, 0))
    return pl.pallas_call(
        functools.partial(_proj_kernel, ctx_row=ctx_row),
        grid=(bsz, t // tm),
        in_specs=[
            pl.BlockSpec((1, tm, d), lambda b, i: (b, i, 0)),
            full((8, 6 * d)),
            pl.BlockSpec(w_in_bf.shape, lambda b, i: (0, 0), pipeline_mode=pl.Buffered(1)),
            full((2, d)), full((2, d)),
        ],
        out_specs=[out(ng), out(2), out(2)],
        out_shape=[jax.ShapeDtypeStruct((ng, bsz, t, d), BF16), jax.ShapeDtypeStruct((2, bsz, t, d), F32),
                   jax.ShapeDtypeStruct((2, bsz, t, d), BF16)],
        compiler_params=_params("arbitrary", "arbitrary"),
        name=name,
    )(x, mod_l, w_in_bf, lb, l1)


def _hgrn_prepare(q_ref, v_ref, lf_ref, k_ref, row0, n, reverse):
    w = HEAD_W
    rows = pl.ds(row0, n * CHUNK)
    log_f = lf_ref[0, 0, rows, :]
    k = k_ref[0, 0, rows, :].astype(F32)
    q = q_ref[0, 0, rows, :]
    v = v_ref[0, 0, rows, :]

    ri = lax.broadcasted_iota(I32, (CHUNK, CHUNK), 0)
    ci = lax.broadcasted_iota(I32, (CHUNK, CHUNK), 1)
    causal = (ci >= ri) if reverse else (ci <= ri)
    tri = causal.astype(BF16)
    lf = jnp.concatenate([log_f[c * CHUNK:(c + 1) * CHUNK, :] for c in range(n)], axis=1)
    hi = lf.astype(BF16)
    rest = lf - hi.astype(F32)
    mid = rest.astype(BF16)
    lo = (rest - mid.astype(F32)).astype(BF16)
    bc = (jnp.dot(tri, hi, preferred_element_type=F32) + jnp.dot(tri, mid, preferred_element_type=F32)
          + jnp.dot(tri, lo, preferred_element_type=F32))

    half = CHUNK // 2
    chunks = []
    for c in range(n):
        sl = slice(c * CHUNK, (c + 1) * CHUNK)
        bcum = bc[:, c * w:(c + 1) * w]
        b_ref = bcum[half:half + 1, :]
        b_end = bcum[0:1, :] if reverse else bcum[CHUNK - 1:CHUNK, :]
        e1 = jnp.exp(jnp.minimum(bcum - b_ref, EXP_CLAMP))
        e2 = jnp.exp(jnp.minimum(b_ref - bcum, EXP_CLAMP))
        qt = q[sl].astype(F32) * e1
        kt = k[sl] * e2
        att = lax.dot_general(qt.astype(BF16), kt.astype(BF16), (((1,), (1,)), ((), ())),
                              preferred_element_type=F32)
        att = jnp.where(causal, att, 0.0).astype(BF16)
        o_intra = jnp.dot(att, v[sl], preferred_element_type=F32)
        qs = (qt * jnp.exp(b_ref)).astype(BF16)
        ke = (kt * jnp.exp(b_end - b_ref)).astype(BF16)
        upd = lax.dot_general(v[sl], ke, (((0,), (0,)), ((), ())), preferred_element_type=F32)
        chunks.append((o_intra, qs, upd, jnp.exp(b_end)))
    return chunks


def _hgrn_state_step(chunk, st, o_ref, row):
    o_intra, qs, upd, decay = chunk
    o_ref[0, 0, pl.ds(row, CHUNK), :] = o_intra + lax.dot_general(
        qs, st.astype(BF16), (((1,), (1,)), ((), ())), preferred_element_type=F32)
    return decay * st + upd


HGRN_SUB = 16


def _hgrn_kernel(qc_ref, vc_ref, lcf_ref, lcb_ref, kcf_ref, kcb_ref,
                 qf_ref, vf_ref, lff_ref, kf_ref, qb_ref, vb_ref, lfb_ref, kb_ref,
                 ocf_ref, ocb_ref, of_ref, ob_ref, sf_ref, sb_ref, *, n_ctx_chunks, n_lat_chunks):
    s = pl.program_id(2)

    def run(n_chunks, fwd, o_f, bwd, o_b):
        sub = min(n_chunks, HGRN_SUB)
        n_sub = n_chunks // sub

        def body(i, carry):
            rf = pl.multiple_of(i * (sub * CHUNK), sub * CHUNK)
            rb = pl.multiple_of((n_sub - 1 - i) * (sub * CHUNK), sub * CHUNK)
            cf = _hgrn_prepare(*fwd, rf, sub, False)
            cb = _hgrn_prepare(*bwd, rb, sub, True)
            st_f, st_b = sf_ref[...], sb_ref[...]
            for c in range(sub):
                st_f = _hgrn_state_step(cf[c], st_f, o_f, rf + c * CHUNK)
                st_b = _hgrn_state_step(cb[sub - 1 - c], st_b, o_b, rb + (sub - 1 - c) * CHUNK)
            sf_ref[...] = st_f
            sb_ref[...] = st_b
            return carry
        lax.fori_loop(0, n_sub, body, 0)

    @pl.when(s == 0)
    def _():
        sf_ref[...] = jnp.zeros_like(sf_ref)
        sb_ref[...] = jnp.zeros_like(sb_ref)
        run(n_ctx_chunks, (qc_ref, vc_ref, lcf_ref, kcf_ref), ocf_ref, (qc_ref, vc_ref, lcb_ref, kcb_ref), ocb_ref)

    run(n_lat_chunks, (qf_ref, vf_ref, lff_ref, kf_ref), of_ref, (qb_ref, vb_ref, lfb_ref, kb_ref), ob_ref)


def _hgrn_call(p_lat, lf_lat, kk_lat, p_ctx, lf_ctx, kk_ctx):
    _, bsz, t, d = p_lat.shape
    tc = p_ctx.shape[2]
    w = HEAD_W
    blk = min(t, 1024)
    ns = t // blk
    lat = lambda g, rev: pl.BlockSpec(
        (1, 1, blk, w), (lambda b, h, s: (g, b, ns - 1 - s, h)) if rev else (lambda b, h, s: (g, b, s, h)))
    ctx = lambda g: pl.BlockSpec((1, 1, tc, w), lambda b, h, s: (g, b, 0, h))
    return pl.pallas_call(
        functools.partial(_hgrn_kernel, n_ctx_chunks=tc // CHUNK, n_lat_chunks=blk // CHUNK),
        grid=(bsz, N_HEADS, ns),
        in_specs=[ctx(0), ctx(1), ctx(0), ctx(1), ctx(0), ctx(1),
                  lat(0, False), lat(1, False), lat(0, False), lat(0, False),
                  lat(0, True), lat(1, True), lat(1, True), lat(1, True)],
        out_specs=[ctx(0), ctx(0), lat(0, False), lat(0, True)],
        out_shape=[jax.ShapeDtypeStruct((1, bsz, tc, d), F32), jax.ShapeDtypeStruct((1, bsz, tc, d), F32),
                   jax.ShapeDtypeStruct((1, bsz, t, d), F32), jax.ShapeDtypeStruct((1, bsz, t, d), F32)],
        scratch_shapes=[pltpu.VMEM((w, w), F32), pltpu.VMEM((w, w), F32)],
        compiler_params=_params("arbitrary", "arbitrary", "arbitrary"),
        name="hgrn",
    )(p_ctx, p_ctx, lf_ctx, lf_ctx, kk_ctx, kk_ctx,
      p_lat, p_lat, lf_lat, kk_lat, p_lat, p_lat, lf_lat, kk_lat)


def _lru_gates(xc, w_all, hb_all, c2_ref):
    w = HEAD_W
    th = jnp.tanh(jnp.dot(xc.astype(BF16), w_all, preferred_element_type=F32) + hb_all)
    out = []
    for dirn in range(2):
        t_r = th[:, (2 * dirn) * w:(2 * dirn + 1) * w]
        t_i = th[:, (2 * dirn + 1) * w:(2 * dirn + 2) * w]
        c2 = c2_ref[dirn:dirn + 1, :]
        a = jnp.exp2(c2 * t_r + c2)
        u = jnp.sqrt(1.0 - a * a) * ((0.5 * t_i + 0.5) * xc)
        out += [a, u]
    return out


def _lru_kernel(xl_ref, xc_ref, cw_ref, cb_ref, wa_ref, wx_ref, ba_ref, bx_ref, lam_ref,
                hl_ref, hc_ref,
                xpad, cpad, a_f, u_f, a_b, u_b, ca_f, cu_f, ca_b, cu_b,
                ends_f, prods_f, carry_f, ends_b, prods_b, carry_b, c2_ref,
                *, t, tc):
    w = HEAD_W
    gw = GRID_W
    rows = t // gw
    lam = lam_ref[...]
    neg = -lam
    softplus = jnp.maximum(neg, 0.0) + jnp.log(1.0 + jnp.exp(-jnp.abs(neg)))
    c2_ref[...] = (-0.5 * LRU_C * LOG2E) * softplus
    w_all = (0.5 * jnp.concatenate([wa_ref[0, 0], wx_ref[0, 0], wa_ref[1, 0], wx_ref[1, 0]], axis=1)).astype(BF16)
    hb_all = 0.5 * jnp.concatenate([ba_ref[0:1, :], bx_ref[0:1, :], ba_ref[1:2, :], bx_ref[1:2, :]], axis=1)
    w0, w1, w2, w3 = (cw_ref[k:k + 1, :] for k in range(4))
    cb = cb_ref[...]

    cpad[...] = jnp.zeros_like(cpad)
    cpad[8:8 + tc, :] = xc_ref[0, 0].astype(F32)
    xcc = (w0 * cpad[6:6 + tc, :] + w1 * cpad[7:7 + tc, :] + w2 * cpad[8:8 + tc, :]
           + w3 * cpad[9:9 + tc, :] + cb)
    ca_f[...], cu_f[...], ca_b[...], cu_b[...] = _lru_gates(xcc, w_all, hb_all, c2_ref)

    def ctx_step(i, hs):
        hf, hb = hs
        p = tc - 1 - i
        hf = ca_f[pl.ds(i, 1), :] * hf + cu_f[pl.ds(i, 1), :]
        hb = ca_b[pl.ds(p, 1), :] * hb + cu_b[pl.ds(p, 1), :]
        cu_f[pl.ds(i, 1), :] = hf
        cu_b[pl.ds(p, 1), :] = hb
        return hf, hb
    zero_row = jnp.zeros((1, w), F32)
    s0_f, s0_b = lax.fori_loop(0, tc, ctx_step, (zero_row, zero_row))
    hc_ref[0] = (cu_f[...] + cu_b[...]).astype(hc_ref.dtype)

    xpad[2 * gw:2 * gw + t, :] = xl_ref[0, 0].astype(F32)
    col = lax.broadcasted_iota(I32, (gw, w), 0)
    body0 = 2 * gw
    xpad[gw:2 * gw, :] = jnp.where(col == 0, 0.0, xpad[pl.ds(body0 + (rows - 1) * gw - 1, gw), :])
    xpad[0:gw, :] = jnp.where(col == 0, 0.0, xpad[pl.ds(body0 + (rows - 2) * gw - 1, gw), :])
    xpad[body0 + t:body0 + t + gw, :] = jnp.where(col == gw - 1, 0.0, xpad[pl.ds(body0 + 1, gw), :])

    gate_rows = min(t, 512)

    def gate_body(c, carry_):
        base = pl.multiple_of(c * gate_rows, gate_rows)
        xcv = (w0 * xpad[pl.ds(base, gate_rows), :] + w1 * xpad[pl.ds(base + gw, gate_rows), :]
               + w2 * xpad[pl.ds(base + 2 * gw, gate_rows), :]
               + w3 * xpad[pl.ds(base + 3 * gw, gate_rows), :] + cb)
        sl = pl.ds(base, gate_rows)
        a_f[sl, :], u_f[sl, :], a_b[sl, :], u_b[sl, :] = _lru_gates(xcv, w_all, hb_all, c2_ref)
        return carry_
    lax.fori_loop(0, t // gate_rows, gate_body, 0)

    def slab(r):
        return pl.ds(pl.multiple_of(r * gw, gw), gw)

    def p1(i, c):
        hf, pf, hb, pb = c
        sf, sb = slab(i), slab(rows - 1 - i)
        af, ab = a_f[sf, :], a_b[sb, :]
        return af * hf + u_f[sf, :], af * pf, ab * hb + u_b[sb, :], ab * pb
    zeros, ones = jnp.zeros((gw, w), F32), jnp.ones((gw, w), F32)
    ends_f[...], prods_f[...], ends_b[...], prods_b[...] = lax.fori_loop(0, rows, p1, (zeros, ones, zeros, ones))

    def chain(i, c):
        cf, cb = c
        jf, jb = pl.ds(i, 1), pl.ds(gw - 1 - i, 1)
        carry_f[jf, :] = cf
        carry_b[jb, :] = cb
        return prods_f[jf, :] * cf + ends_f[jf, :], prods_b[jb, :] * cb + ends_b[jb, :]
    lax.fori_loop(0, gw, chain, (s0_f, s0_b))

    def p2(i, c):
        hf, hb = c
        sf, sb = slab(i), slab(rows - 1 - i)
        hf = a_f[sf, :] * hf + u_f[sf, :]
        hb = a_b[sb, :] * hb + u_b[sb, :]
        u_f[sf, :] = hf
        u_b[sb, :] = hb
        return hf, hb
    lax.fori_loop(0, rows, p2, (carry_f[...], carry_b[...]))
    hl_ref[0] = (u_f[...] + u_b[...]).astype(hl_ref.dtype)


def _lru_call(p_lat, p_ctx, g_lat, g_ctx, conv_w, conv_b, wa, wx, ba, bx, lam):
    _, bsz, t, d = p_lat.shape
    tc = p_ctx.shape[2]
    w = HEAD_W
    nb = d // w
    gw = GRID_W
    vec = lambda n: pl.BlockSpec((n, w), lambda b, k: (0, k))
    mat = pl.BlockSpec((2, 1, w, w), lambda b, k: (0, k, 0, 0))
    return pl.pallas_call(
        functools.partial(_lru_kernel, t=t, tc=tc),
        grid=(bsz, nb),
        in_specs=[
            pl.BlockSpec((1, 1, t, w), lambda b, k: (g_lat, b, 0, k)),
            pl.BlockSpec((1, 1, tc, w), lambda b, k: (g_ctx, b, 0, k)),
            vec(4), vec(1), mat, mat, vec(2), vec(2), vec(2),
        ],
        out_specs=[pl.BlockSpec((1, t, w), lambda b, k: (b, 0, k)),
                   pl.BlockSpec((1, tc, w), lambda b, k: (b, 0, k))],
        out_shape=[jax.ShapeDtypeStruct((bsz, t, d), BF16), jax.ShapeDtypeStruct((bsz, tc, d), BF16)],
        scratch_shapes=[
            pltpu.VMEM((t + 3 * gw, w), F32), pltpu.VMEM((tc + 16, w), F32),
            pltpu.VMEM((t, w), F32), pltpu.VMEM((t, w), F32), pltpu.VMEM((t, w), F32), pltpu.VMEM((t, w), F32),
            pltpu.VMEM((tc, w), F32), pltpu.VMEM((tc, w), F32), pltpu.VMEM((tc, w), F32), pltpu.VMEM((tc, w), F32),
            pltpu.VMEM((gw, w), F32), pltpu.VMEM((gw, w), F32), pltpu.VMEM((gw, w), F32),
            pltpu.VMEM((gw, w), F32), pltpu.VMEM((gw, w), F32), pltpu.VMEM((gw, w), F32),
            pltpu.VMEM((2, w), F32),
        ],
        compiler_params=_params("arbitrary", "arbitrary"),
        name="rglru",
    )(p_lat, p_ctx, conv_w, conv_b.reshape(1, d), wa, wx, ba, bx, lam)


def _merge_kernel(of_ref, ob_ref, og_ref, hl_ref, ly_ref, ma_ref, mb_ref, x_ref, mod_ref, gn_ref,
                  wa_ref, wb_ref, wo_ref, lng_ref, lnb_ref, wrt_ref,
                  x1_ref, h2_ref, afft_ref, *, ctx_row):
    d = D_MODEL
    b = pl.program_id(0)
    row = b if ctx_row is None else ctx_row
    o = of_ref[0, 0] + ob_ref[0, 0]
    gn = gn_ref[...]
    parts = []
    for h in range(N_HEADS):
        oh = o[:, h * HEAD_W:(h + 1) * HEAD_W]
        ms = jnp.mean(oh * oh, axis=-1, keepdims=True)
        parts.append(oh * lax.rsqrt(ms + RMS_EPS) * gn)
    o_a = (jnp.concatenate(parts, axis=1) * og_ref[0, 0].astype(F32)).astype(BF16)
    y_a = jnp.dot(o_a, wa_ref[...], preferred_element_type=F32)
    y_b = jnp.dot((hl_ref[0].astype(F32) * ly_ref[0, 0].astype(F32)).astype(BF16), wb_ref[...],
                  preferred_element_type=F32)
    z = ma_ref[0, 0].astype(F32) * y_a + mb_ref[0, 0].astype(F32) * y_b
    y = jnp.dot(z.astype(BF16), wo_ref[...], preferred_element_type=F32)
    g1 = mod_ref[pl.ds(row, 1), 2 * d:3 * d]
    x1 = _layer_norm(ALPHA * x_ref[0] + g1 * y, lng_ref[0:1, :], lnb_ref[0:1, :])
    x1_ref[0] = x1
    sh2 = mod_ref[pl.ds(row, 1), 3 * d:4 * d]
    sc2 = mod_ref[pl.ds(row, 1), 4 * d:5 * d]
    h2 = x1 * (1.0 + sc2) + sh2
    h2_ref[0] = h2.astype(BF16)
    logits_t = lax.dot_general(wrt_ref[...], h2, (((1,), (1,)), ((), ())),
                               preferred_element_type=F32, precision=HIGHEST)
    pt = jnp.exp(logits_t - jnp.max(logits_t, axis=0, keepdims=True))
    afft_ref[0] = pt / jnp.sum(pt, axis=0, keepdims=True)


def _merge_call(o_f, o_b, p, groups, h_lru, x, mod_l, gn, wa, wb, wo, ln_g, ln_b, w_router, ctx_row, name):
    bsz, t, d = x.shape
    tm = min(t, 512)
    e = N_EXPERTS
    tile = lambda g: pl.BlockSpec((1, 1, tm, d), lambda b, i: (g, b, i, 0))
    tok = pl.BlockSpec((1, tm, d), lambda b, i: (b, i, 0))
    full = lambda shape: pl.BlockSpec(shape, lambda b, i: tuple(0 for _ in shape))
    return pl.pallas_call(
        functools.partial(_merge_kernel, ctx_row=ctx_row),
        grid=(bsz, t // tm),
        in_specs=[tile(0), tile(0), tile(groups["og"]), tok, tile(groups["ly"]), tile(groups["ma"]),
                  tile(groups["mb"]), tok, full((8, 6 * d)), full((1, HEAD_W)),
                  full((d, d)), full((d, d)), full((d, d)), full((2, d)), full((2, d)),
                  full((e, d))],
        out_specs=[tok, tok, pl.BlockSpec((1, e, tm), lambda b, i: (b, 0, i))],
        out_shape=[jax.ShapeDtypeStruct((bsz, t, d), F32), jax.ShapeDtypeStruct((bsz, t, d), BF16),
                   jax.ShapeDtypeStruct((bsz, e, t), F32)],
        compiler_params=_params("arbitrary", "arbitrary"),
        name=name,
    )(o_f, o_b, p, h_lru, p, p, p, x, mod_l, gn.reshape(1, HEAD_W), wa, wb, wo, ln_g, ln_b,
      w_router.T)


def _route_kernel(a_ref, posm_ref, off_ref, span_ref, *, cap, t, tt, grt):
    e = N_EXPERTS
    u = lax.bitcast_convert_type(a_ref[0], I32)
    thr = jnp.zeros((e, 1), I32)
    for bit in range(30, -1, -1):
        cand = thr | (1 << bit)
        cnt = jnp.sum((u >= cand).astype(F32), axis=1, keepdims=True)
        thr = jnp.where(cnt >= cap, cand, thr)
    gt = u > thr
    eq = u == thr
    need = cap - jnp.sum(gt.astype(F32), axis=1, keepdims=True)
    ri = lax.broadcasted_iota(I32, (tt, tt), 0)
    ci = lax.broadcasted_iota(I32, (tt, tt), 1)
    before = (ri < ci).astype(BF16)
    lane = lax.broadcasted_iota(I32, (e, OFF_STRIDE), 1)
    offs = jnp.zeros((e, OFF_STRIDE), F32)
    offs_end = jnp.zeros((e, OFF_STRIDE), F32)
    n_eq = jnp.zeros((e, 1), F32)
    n_sel = jnp.zeros((e, 1), F32)
    for j in range(t // tt):
        sl = slice(j * tt, (j + 1) * tt)
        eq_j = eq[:, sl]
        rank_eq = jnp.dot(eq_j.astype(BF16), before, preferred_element_type=F32) + n_eq
        sel_j = gt[:, sl] | (eq_j & (rank_eq < need))
        sel_f = sel_j.astype(F32)
        pos = jnp.dot(sel_j.astype(BF16), before, preferred_element_type=F32) + n_sel
        posm_ref[0, :, sl] = jnp.where(sel_j, pos.astype(I32), -1)
        offs = jnp.where(lane == j, n_sel, offs)
        n_eq = n_eq + jnp.sum(eq_j.astype(F32), axis=1, keepdims=True)
        n_sel = n_sel + jnp.sum(sel_f, axis=1, keepdims=True)
        offs_end = jnp.where(lane == j, n_sel, offs_end)
    nt = t // tt
    offs = jnp.where(lane == nt, n_sel, offs)
    off_ref[0] = offs.astype(I32)
    n_rt = cap // grt
    tile = lane < nt
    spans = jnp.zeros((e, OFF_STRIDE), F32)
    for i in range(n_rt):
        skipped = jnp.sum(jnp.where(tile & (offs_end <= i * grt), 1.0, 0.0), axis=1, keepdims=True)
        started = jnp.sum(jnp.where(tile & (offs < (i + 1) * grt), 1.0, 0.0), axis=1, keepdims=True)
        spans = jnp.where(lane == i, skipped, jnp.where(lane == n_rt + i, started, spans))
    span_ref[0] = spans.astype(I32)


def _route_call(aff_t, cap, tt, grt, name):
    bsz, e, t = aff_t.shape
    small = pl.BlockSpec((1, e, OFF_STRIDE), lambda b: (b, 0, 0))
    return pl.pallas_call(
        functools.partial(_route_kernel, cap=cap, t=t, tt=tt, grt=grt),
        grid=(bsz,),
        in_specs=[pl.BlockSpec((1, e, t), lambda b: (b, 0, 0))],
        out_specs=[pl.BlockSpec((1, e, t), lambda b: (b, 0, 0)), small, small],
        out_shape=[jax.ShapeDtypeStruct((bsz, e, t), I32), jax.ShapeDtypeStruct((bsz, e, OFF_STRIDE), I32),
                   jax.ShapeDtypeStruct((bsz, e, OFF_STRIDE), I32)],
        compiler_params=_params("arbitrary"),
        name=name,
    )(aff_t)


def _expert_kernel(span_ref, posm_ref, h_ref, wg_ref, wu_ref, wd_ref, y_ref, x_ref, *, cap, rt, tt, nt,
                   expert_major):
    b = pl.program_id(1 if expert_major else 0)
    e = pl.program_id(0 if expert_major else 1)
    base = (b * N_EXPERTS + e) * OFF_STRIDE
    x_ref[...] = jnp.zeros_like(x_ref)
    kt = min(nt, GATHER_WINDOW_TILES)
    kw = kt * tt
    rows = lax.broadcasted_iota(I32, (rt, kw), 0)
    lane = lax.broadcasted_iota(I32, (rt, kw), 1)
    n_rt = cap // rt

    for i in range(n_rt):
        first = span_ref[base + i]
        n_win = (span_ref[base + n_rt + i] - 1 - first) // kt + 1

        def body(wdx, carry, i=i, first=first):
            want = first + wdx * kt
            start = jnp.minimum(want, nt - kt)
            tok = pl.ds(pl.multiple_of(start * tt, tt), kw)
            hit = ((posm_ref[0, 0, :, tok] - i * rt) == rows) & (lane >= (want - start) * tt)
            x_ref[i * rt:(i + 1) * rt, :] += jnp.dot(hit.astype(BF16), h_ref[0, tok, :],
                                                     preferred_element_type=F32).astype(BF16)
            return carry
        lax.fori_loop(0, n_win, body, 0)
    wg, wu, wd = wg_ref[0, 0].astype(BF16), wu_ref[0, 0].astype(BF16), wd_ref[0, 0].astype(BF16)
    ft = min(cap, 256)
    for r in range(cap // ft):
        xb = x_ref[r * ft:(r + 1) * ft, :]
        g = jnp.dot(xb, wg, preferred_element_type=F32)
        u = jnp.dot(xb, wu, preferred_element_type=F32)
        hid = (g * _sigmoid(g) * u).astype(BF16)
        y_ref[0, 0, r * ft:(r + 1) * ft, :] = jnp.dot(hid, wd, preferred_element_type=F32).astype(BF16)


def _expert_call(span_flat, posm, h2, w_gate, w_up, w_down, layer, cap, rt, tt, name):
    bsz, t, d = h2.shape
    _, e, _, f = w_gate.shape
    nt = t // tt
    expert_major = t < d
    if expert_major:
        grid = (e, bsz)
        be = lambda k, b: (b, k)
    else:
        grid = (bsz, e)
        be = lambda b, k: (b, k)
    tok_spec = pl.BlockSpec((1, t, d), lambda i, j, off: (be(i, j)[0], 0, 0),
                            pipeline_mode=None if expert_major else pl.Buffered(1))
    weight = lambda r, c: pl.BlockSpec((1, 1, r, c), lambda i, j, off: (layer, be(i, j)[1], 0, 0))
    grid_spec = pltpu.PrefetchScalarGridSpec(
        num_scalar_prefetch=1,
        grid=grid,
        in_specs=[
            pl.BlockSpec((1, 1, 1, t), lambda i, j, off: (*be(i, j), 0, 0)),
            tok_spec, weight(d, f), weight(d, f), weight(f, d),
        ],
        out_specs=pl.BlockSpec((1, 1, cap, d), lambda i, j, off: (*be(i, j), 0, 0)),
        scratch_shapes=[pltpu.VMEM((cap, d), BF16)],
    )
    return pl.pallas_call(
        functools.partial(_expert_kernel, cap=cap, rt=rt, tt=tt, nt=nt, expert_major=expert_major),
        grid_spec=grid_spec,
        out_shape=jax.ShapeDtypeStruct((bsz, e, cap, d), BF16),
        compiler_params=_params("arbitrary", "arbitrary"),
        name=name,
    )(span_flat, posm.reshape(bsz, e, 1, t), h2, w_gate, w_up, w_down)


def _combine_kernel(off_ref, posm_ref, afft_ref, x1_ref, y_hbm, mod_ref, lng_ref, lnb_ref, o_ref,
                    ybuf, pbuf, sem, acc_ref, *, rt, tt, nw, ctx_row, n_tiles):
    d = D_MODEL
    b = pl.program_id(0)
    j = pl.program_id(1)
    row = b if ctx_row is None else ctx_row
    shift = rt.bit_length() - 1
    step = b * n_tiles + j
    cur = lax.rem(step, 2)
    is_last = step == pl.num_programs(0) * n_tiles - 1

    def plan(bb, jj):
        out, n_used = [], jnp.int32(0)
        for e in range(N_EXPERTS):
            base = (bb * N_EXPERTS + e) * OFF_STRIDE
            lo = off_ref[base + jj]
            hi = off_ref[base + jj + 1]
            t0 = lax.shift_right_logical(lo, shift)
            n = jnp.where(hi > lo, lax.shift_right_logical(hi - 1, shift) - t0 + 1, 0)
            out.append((t0, n, n_used))
            n_used = n_used + n
        return out, n_used

    def for_used_windows(windows, fn):
        for e, (t0, n, s0) in enumerate(windows):
            for wdx in range(nw):
                pl.when(wdx < n)(functools.partial(fn, e, t0 + wdx, s0 + wdx))

    def window_copy(bb, e, window, slot, half):
        src = y_hbm.at[bb, e, pl.ds(pl.multiple_of(window * rt, rt), rt), :]
        dst = ybuf.at[half, pl.ds(pl.multiple_of(slot * rt, rt), rt), :]
        return pltpu.make_async_copy(src, dst, sem.at[half])

    def fetch(bb, windows, half):
        for_used_windows(windows, lambda e, window, slot: window_copy(bb, e, window, slot, half).start())

    windows, n_used = plan(b, j)
    n_groups = lax.shift_right_logical(n_used + (COMBINE_GROUP - 1), COMBINE_GROUP.bit_length() - 1)

    @pl.when(step == 0)
    def _():
        ybuf[...] = jnp.zeros_like(ybuf)
        fetch(b, windows, 0)

    @pl.when(jnp.logical_not(is_last))
    def _():
        wrap = j == n_tiles - 1
        b_next = jnp.where(wrap, b + 1, b)
        fetch(b_next, plan(b_next, jnp.where(wrap, 0, j + 1))[0], 1 - cur)

    def clear(s, carry):
        pbuf[pl.ds(pl.multiple_of(s * rt, rt), rt), :] = jnp.zeros((rt, tt), BF16)
        return carry
    lax.fori_loop(n_used, n_groups * COMBINE_GROUP, clear, 0)

    rows = lax.broadcasted_iota(I32, (rt, tt), 0)

    def weights(e, window, slot):
        pos = posm_ref[0, e]
        gate = afft_ref[0, e:e + 1, :]
        pbuf[pl.ds(pl.multiple_of(slot * rt, rt), rt), :] = jnp.where(
            (pos - window * rt) == rows, gate, 0.0).astype(BF16)
    for_used_windows(windows, weights)

    def drain(i, carry):
        window_copy(0, 0, 0, 0, cur).wait()
        return carry
    lax.fori_loop(0, n_used, drain, 0)

    acc_ref[...] = jnp.zeros_like(acc_ref)
    gk = COMBINE_GROUP * rt
    for g in range(pl.cdiv(N_EXPERTS * nw, COMBINE_GROUP)):
        @pl.when(g < n_groups)
        def _(g=g):
            acc_ref[...] += lax.dot_general(pbuf[g * gk:(g + 1) * gk, :], ybuf[cur, g * gk:(g + 1) * gk, :],
                                            (((0,), (0,)), ((), ())), preferred_element_type=F32)
    g2 = mod_ref[pl.ds(row, 1), 5 * d:6 * d]
    o_ref[0] = _layer_norm(ALPHA * x1_ref[0] + g2 * acc_ref[...], lng_ref[1:2, :], lnb_ref[1:2, :])


def _combine_call(off_flat, posm, aff_t, x1, y, mod_l, ln_g, ln_b, rt, tt, ctx_row, name):
    bsz, t, d = x1.shape
    e = N_EXPERTS
    cap = y.shape[2]
    nw = min(cap // rt, (tt - 1) // rt + 2)
    n_slots = pl.cdiv(e * nw, COMBINE_GROUP) * COMBINE_GROUP
    grid_spec = pltpu.PrefetchScalarGridSpec(
        num_scalar_prefetch=1,
        grid=(bsz, t // tt),
        in_specs=[
            pl.BlockSpec((1, e, 1, tt), lambda b, j, off: (b, 0, 0, j)),
            pl.BlockSpec((1, e, tt), lambda b, j, off: (b, 0, j)),
            pl.BlockSpec((1, tt, d), lambda b, j, off: (b, j, 0)),
            pl.BlockSpec(memory_space=pl.ANY),
            pl.BlockSpec((8, 6 * d), lambda b, j, off: (0, 0)),
            pl.BlockSpec((2, d), lambda b, j, off: (0, 0)),
            pl.BlockSpec((2, d), lambda b, j, off: (0, 0)),
        ],
        out_specs=pl.BlockSpec((1, tt, d), lambda b, j, off: (b, j, 0)),
        scratch_shapes=[pltpu.VMEM((2, n_slots * rt, d), BF16), pltpu.VMEM((n_slots * rt, tt), BF16),
                        pltpu.SemaphoreType.DMA((2,)), pltpu.VMEM((tt, d), F32)],
    )
    return pl.pallas_call(
        functools.partial(_combine_kernel, rt=rt, tt=tt, nw=nw, ctx_row=ctx_row, n_tiles=t // tt),
        grid_spec=grid_spec,
        out_shape=jax.ShapeDtypeStruct((bsz, t, d), F32),
        compiler_params=_params("arbitrary", "arbitrary"),
        name=name,
    )(off_flat, posm.reshape(bsz, e, 1, t), aff_t, x1, y, mod_l, ln_g, ln_b)


def _moe(x1, h2, aff_t, mod_l, w_gate, w_up, w_down, layer, ln_g, ln_b, ctx_row, tag):
    bsz, t, d = x1.shape
    cap = EC_FACTOR * t // N_EXPERTS
    tt = min(t, 256)
    grt = min(cap, GATHER_ROWS)
    posm, off, span = _route_call(aff_t, cap, tt, grt, "route_" + tag)
    y = _expert_call(span.reshape(-1), posm, h2, w_gate, w_up, w_down, layer, cap, grt, tt, "expert_" + tag)
    return _combine_call(off.reshape(-1), posm, aff_t, x1, y, mod_l, ln_g, ln_b, min(cap, SCATTER_ROWS), tt, ctx_row,
                         "combine_" + tag)


def kernel(x, c, ctx, c_ctx, w_mod, b_mod, w_in, hgrn_lb_logits, hgrn_norm_g, conv_w, conv_b, lru_wa,
           lru_ba, lru_wx, lru_bx, lru_lambda, w_branch_a, w_branch_b, w_out, ln_g, ln_b, w_router,
           w_gate, w_up, w_down):
    depth = w_in.shape[0]
    bsz = x.shape[0]
    ctx_row = bsz
    assert bsz < 8 and depth == DEPTH

    lb_cum = jnp.cumsum(jax.nn.softmax(hgrn_lb_logits.astype(F32), axis=0), axis=0)
    lbs = lb_cum - lb_cum[0]
    log1m_lb = jnp.log1p(-lbs)

    c_all = jnp.zeros((8, x.shape[2]), F32).at[:bsz].set(c).at[ctx_row].set(c_ctx)
    mod = _mod_call(c_all, w_mod, b_mod)

    p_index = P_INDEX
    for l in range(depth):
        need_ctx = l < depth - 1
        mod_l = mod[l]
        w_in_bf = w_in[l].astype(BF16)
        p_lat, lf_lat, kk_lat = _proj_call(x, mod_l, w_in_bf, lbs[l], log1m_lb[l], None, "proj_lat")
        p_ctx, lf_ctx, kk_ctx = _proj_call(ctx, mod_l, w_in_bf, lbs[l], log1m_lb[l], ctx_row,
                                           "proj_ctx")

        oc_f, oc_b, ol_f, ol_b = _hgrn_call(p_lat, lf_lat, kk_lat, p_ctx, lf_ctx, kk_ctx)
        h_lat, h_ctx = _lru_call(p_lat, p_ctx, p_index["lx"], p_index["lx"], conv_w[l], conv_b[l],
                                 lru_wa[l], lru_wx[l], lru_ba[l], lru_bx[l], lru_lambda[l])

        wa, wb, wo = (w.astype(BF16) for w in (w_branch_a[l], w_branch_b[l], w_out[l]))
        x1, h2, aff_t = _merge_call(ol_f, ol_b, p_lat, p_index, h_lat, x, mod_l, hgrn_norm_g[l],
                                         wa, wb, wo, ln_g[l], ln_b[l], w_router[l], None, "merge_lat")
        if need_ctx:
            c1, ch2, caff_t = _merge_call(oc_f, oc_b, p_ctx, p_index, h_ctx, ctx, mod_l, hgrn_norm_g[l],
                                                wa, wb, wo, ln_g[l], ln_b[l], w_router[l], ctx_row,
                                                "merge_ctx")
        x = _moe(x1, h2, aff_t, mod_l, w_gate, w_up, w_down, l, ln_g[l], ln_b[l], None, "lat")
        if need_ctx:
            ctx = _moe(c1, ch2, caff_t, mod_l, w_gate, w_up, w_down, l, ln_g[l], ln_b[l], ctx_row, "ctx")
    return x
```

```python
import functools

import jax
import jax.numpy as jnp
from jax import lax
from jax.experimental import pallas as pl
from jax.experimental.pallas import tpu as pltpu

F32 = jnp.float32
BF16 = jnp.bfloat16
I32 = jnp.int32
HIGHEST = lax.Precision.HIGHEST

DEPTH = 2
D_MODEL = 1024
GRID_W = 64
N_HEADS = 8
HEAD_W = 128
CHUNK = 64
N_EXPERTS = 16
EC_FACTOR = 2
LRU_C = 8.0
ALPHA = (2.0 * DEPTH) ** 0.25
LN_EPS = 1e-5
RMS_EPS = 1e-6
LN2 = 0.6931471805599453
LOG2E = 1.4426950408889634
SQRT_2_OVER_PI = 0.7978845608028654
EXP_CLAMP = 80.0

V7X_VMEM_LIMIT_BYTES = 56 * 1024 * 1024
OFF_STRIDE = 128
GATHER_WINDOW_TILES = 6
COMBINE_GROUP = 8
GATHER_ROWS = 128
SCATTER_ROWS = 64

G_Q, G_V, G_FF, G_FB, G_OG, G_LX, G_LY, G_MA, G_MB = range(9)


def _params(*sem):
    return pltpu.CompilerParams(dimension_semantics=sem, vmem_limit_bytes=V7X_VMEM_LIMIT_BYTES)


def _sigmoid(x):
    return 0.5 * jnp.tanh(0.5 * x) + 0.5


def _silu(x):
    h = 0.5 * x
    return h * jnp.tanh(h) + h


def _layer_norm(x, g, b):
    mu = jnp.mean(x, axis=-1, keepdims=True)
    xc = x - mu
    var = jnp.mean(xc * xc, axis=-1, keepdims=True)
    return xc * lax.rsqrt(var + LN_EPS) * g + b


def _mod_kernel(c_ref, w_ref, b_ref, o_ref):
    c = c_ref[...]
    s = _silu(c)
    o_ref[0] = jnp.dot(s, w_ref[0], preferred_element_type=F32, precision=HIGHEST) + b_ref[0]


def _mod_call(c_all, w_mod, b_mod):
    depth, d, n = w_mod.shape
    tn = 1536
    return pl.pallas_call(
        _mod_kernel,
        grid=(depth, n // tn),
        in_specs=[
            pl.BlockSpec((8, d), lambda l, j: (0, 0)),
            pl.BlockSpec((1, d, tn), lambda l, j: (l, 0, j)),
            pl.BlockSpec((1, 1, tn), lambda l, j: (l, 0, j)),
        ],
        out_specs=pl.BlockSpec((1, 8, tn), lambda l, j: (l, 0, j)),
        out_shape=jax.ShapeDtypeStruct((depth, 8, n), F32),
        compiler_params=_params("arbitrary", "arbitrary"),
        name="mod",
    )(c_all, w_mod, b_mod.reshape(depth, 1, n))


def _apply_act(act, p):
    if act == "id":
        return p
    if act == "silu_scale":
        return _silu(p) * (HEAD_W ** -0.5)
    if act == "silu":
        return _silu(p)
    if act == "sigmoid":
        return _sigmoid(p)
    if act == "gelu":
        ph = 0.5 * p
        return ph * jnp.tanh(p * ((0.044715 * SQRT_2_OVER_PI) * (p * p) + SQRT_2_OVER_PI)) + ph
    raise ValueError(act)


P_GROUPS = (G_Q, G_V, G_OG, G_LX, G_LY, G_MA, G_MB)
P_ACTS = ("silu_scale", "id", "silu", "id", "gelu", "sigmoid", "sigmoid")
P_INDEX = {"q": 0, "v": 1, "og": 2, "lx": 3, "ly": 4, "ma": 5, "mb": 6}


def _hgrn_gates(fp, lb, l1):
    kh = 0.5 * (1.0 - lb)
    p = kh * jnp.tanh(0.5 * fp)
    log_f = jnp.maximum(jnp.log((lb + kh) + p), jnp.minimum(fp, 0.0) + (l1 - LN2))
    return log_f, kh - p


def _proj_kernel(x_ref, mod_ref, w_ref, lb_ref, l1_ref, p_ref, lf_ref, kk_ref, *, ctx_row):
    d = D_MODEL
    b = pl.program_id(0)
    row = b if ctx_row is None else ctx_row
    sh = mod_ref[pl.ds(row, 1), 0:d]
    sc = mod_ref[pl.ds(row, 1), d:2 * d]
    h = (x_ref[0] * (1.0 + sc) + sh).astype(BF16)

    def group(col):
        return jnp.dot(h, w_ref[:, col * d:(col + 1) * d], preferred_element_type=F32)

    def plain(name):
        g = P_INDEX[name]
        p_ref[g, 0] = _apply_act(P_ACTS[g], group(P_GROUPS[g])).astype(BF16)

    def gate(dirn, col):
        log_f, k = _hgrn_gates(group(col), lb_ref[dirn:dirn + 1, :], l1_ref[dirn:dirn + 1, :])
        lf_ref[dirn, 0] = log_f
        kk_ref[dirn, 0] = k.astype(BF16)

    gate(0, G_FF)
    plain("v")
    gate(1, G_FB)
    plain("lx")
    plain("q")
    plain("ly")
    plain("og")
    plain("ma")
    plain("mb")


def _proj_call(x, mod_l, w_in_bf, lb, l1, ctx_row, name):
    bsz, t, d = x.shape
    tm = min(t, 512)
    ng = len(P_GROUPS)
    full = lambda shape: pl.BlockSpec(shape, lambda b, i: tuple(0 for _ in shape))
    out = lambda n: pl.BlockSpec((n, 1, tm, d), lambda b, i: (0, b, i, 0))
    return pl.pallas_call(
        functools.partial(_proj_kernel, ctx_row=ctx_row),
        grid=(bsz, t // tm),
        in_specs=[
            pl.BlockSpec((1, tm, d), lambda b, i: (b, i, 0)),
            full((8, 6 * d)),
            pl.BlockSpec(w_in_bf.shape, lambda b, i: (0, 0), pipeline_mode=pl.Buffered(1)),
            full((2, d)), full((2, d)),
        ],
        out_specs=[out(ng), out(2), out(2)],
        out_shape=[jax.ShapeDtypeStruct((ng, bsz, t, d), BF16), jax.ShapeDtypeStruct((2, bsz, t, d), F32),
                   jax.ShapeDtypeStruct((2, bsz, t, d), BF16)],
        compiler_params=_params("arbitrary", "arbitrary"),
        name=name,
    )(x, mod_l, w_in_bf, lb, l1)


def _hgrn_prepare(q_ref, v_ref, lf_ref, k_ref, row0, n, reverse):
    w = HEAD_W
    rows = pl.ds(row0, n * CHUNK)
    log_f = lf_ref[0, 0, rows, :]
    k = k_ref[0, 0, rows, :].astype(F32)
    q = q_ref[0, 0, rows, :]
    v = v_ref[0, 0, rows, :]

    ri = lax.broadcasted_iota(I32, (CHUNK, CHUNK), 0)
    ci = lax.broadcasted_iota(I32, (CHUNK, CHUNK), 1)
    causal = (ci >= ri) if reverse else (ci <= ri)
    tri = causal.astype(BF16)
    lf = jnp.concatenate([log_f[c * CHUNK:(c + 1) * CHUNK, :] for c in range(n)], axis=1)
    hi = lf.astype(BF16)
    rest = lf - hi.astype(F32)
    mid = rest.astype(BF16)
    lo = (rest - mid.astype(F32)).astype(BF16)
    bc = (jnp.dot(tri, hi, preferred_element_type=F32) + jnp.dot(tri, mid, preferred_element_type=F32)
          + jnp.dot(tri, lo, preferred_element_type=F32))

    half = CHUNK // 2
    chunks = []
    for c in range(n):
        sl = slice(c * CHUNK, (c + 1) * CHUNK)
        bcum = bc[:, c * w:(c + 1) * w]
        b_ref = bcum[half:half + 1, :]
        b_end = bcum[0:1, :] if reverse else bcum[CHUNK - 1:CHUNK, :]
        e1 = jnp.exp(jnp.minimum(bcum - b_ref, EXP_CLAMP))
        e2 = jnp.exp(jnp.minimum(b_ref - bcum, EXP_CLAMP))
        qt = q[sl].astype(F32) * e1
        kt = k[sl] * e2
        att = lax.dot_general(qt.astype(BF16), kt.astype(BF16), (((1,), (1,)), ((), ())),
                              preferred_element_type=F32)
        att = jnp.where(causal, att, 0.0).astype(BF16)
        o_intra = jnp.dot(att, v[sl], preferred_element_type=F32)
        qs = (qt * jnp.exp(b_ref)).astype(BF16)
        ke = (kt * jnp.exp(b_end - b_ref)).astype(BF16)
        upd = lax.dot_general(v[sl], ke, (((0,), (0,)), ((), ())), preferred_element_type=F32)
        chunks.append((o_intra, qs, upd, jnp.exp(b_end)))
    return chunks


def _hgrn_state_step(chunk, st, o_ref, row):
    o_intra, qs, upd, decay = chunk
    o_ref[0, 0, pl.ds(row, CHUNK), :] = o_intra + lax.dot_general(
        qs, st.astype(BF16), (((1,), (1,)), ((), ())), preferred_element_type=F32)
    return decay * st + upd


HGRN_SUB = 16


def _hgrn_kernel(qc_ref, vc_ref, lcf_ref, lcb_ref, kcf_ref, kcb_ref,
                 qf_ref, vf_ref, lff_ref, kf_ref, qb_ref, vb_ref, lfb_ref, kb_ref,
                 ocf_ref, ocb_ref, of_ref, ob_ref, sf_ref, sb_ref, *, n_ctx_chunks, n_lat_chunks):
    s = pl.program_id(2)

    def run(n_chunks, fwd, o_f, bwd, o_b):
        sub = min(n_chunks, HGRN_SUB)
        n_sub = n_chunks // sub

        def body(i, carry):
            rf = pl.multiple_of(i * (sub * CHUNK), sub * CHUNK)
            rb = pl.multiple_of((n_sub - 1 - i) * (sub * CHUNK), sub * CHUNK)
            cf = _hgrn_prepare(*fwd, rf, sub, False)
            cb = _hgrn_prepare(*bwd, rb, sub, True)
            st_f, st_b = sf_ref[...], sb_ref[...]
            for c in range(sub):
                st_f = _hgrn_state_step(cf[c], st_f, o_f, rf + c * CHUNK)
                st_b = _hgrn_state_step(cb[sub - 1 - c], st_b, o_b, rb + (sub - 1 - c) * CHUNK)
            sf_ref[...] = st_f
            sb_ref[...] = st_b
            return carry
        lax.fori_loop(0, n_sub, body, 0)

    @pl.when(s == 0)
    def _():
        sf_ref[...] = jnp.zeros_like(sf_ref)
        sb_ref[...] = jnp.zeros_like(sb_ref)
        run(n_ctx_chunks, (qc_ref, vc_ref, lcf_ref, kcf_ref), ocf_ref, (qc_ref, vc_ref, lcb_ref, kcb_ref), ocb_ref)

    run(n_lat_chunks, (qf_ref, vf_ref, lff_ref, kf_ref), of_ref, (qb_ref, vb_ref, lfb_ref, kb_ref), ob_ref)


def _hgrn_call(p_lat, lf_lat, kk_lat, p_ctx, lf_ctx, kk_ctx):
    _, bsz, t, d = p_lat.shape
    tc = p_ctx.shape[2]
    w = HEAD_W
    blk = min(t, 2048)
    ns = t // blk
    lat = lambda g, rev: pl.BlockSpec(
        (1, 1, blk, w), (lambda b, h, s: (g, b, ns - 1 - s, h)) if rev else (lambda b, h, s: (g, b, s, h)))
    ctx = lambda g: pl.BlockSpec((1, 1, tc, w), lambda b, h, s: (g, b, 0, h))
    return pl.pallas_call(
        functools.partial(_hgrn_kernel, n_ctx_chunks=tc // CHUNK, n_lat_chunks=blk // CHUNK),
        grid=(bsz, N_HEADS, ns),
        in_specs=[ctx(0), ctx(1), ctx(0), ctx(1), ctx(0), ctx(1),
                  lat(0, False), lat(1, False), lat(0, False), lat(0, False),
                  lat(0, True), lat(1, True), lat(1, True), lat(1, True)],
        out_specs=[ctx(0), ctx(0), lat(0, False), lat(0, True)],
        out_shape=[jax.ShapeDtypeStruct((1, bsz, tc, d), F32), jax.ShapeDtypeStruct((1, bsz, tc, d), F32),
                   jax.ShapeDtypeStruct((1, bsz, t, d), F32), jax.ShapeDtypeStruct((1, bsz, t, d), F32)],
        scratch_shapes=[pltpu.VMEM((w, w), F32), pltpu.VMEM((w, w), F32)],
        compiler_params=_params("arbitrary", "arbitrary", "arbitrary"),
        name="hgrn",
    )(p_ctx, p_ctx, lf_ctx, lf_ctx, kk_ctx, kk_ctx,
      p_lat, p_lat, lf_lat, kk_lat, p_lat, p_lat, lf_lat, kk_lat)


def _lru_gates(xc, w_all, hb_all, c2_ref):
    w = HEAD_W
    th = jnp.tanh(jnp.dot(xc.astype(BF16), w_all, preferred_element_type=F32) + hb_all)
    out = []
    for dirn in range(2):
        t_r = th[:, (2 * dirn) * w:(2 * dirn + 1) * w]
        t_i = th[:, (2 * dirn + 1) * w:(2 * dirn + 2) * w]
        c2 = c2_ref[dirn:dirn + 1, :]
        a = jnp.exp2(c2 * t_r + c2)
        u = jnp.sqrt(1.0 - a * a) * ((0.5 * t_i + 0.5) * xc)
        out += [a, u]
    return out


def _lru_kernel(xl_ref, xc_ref, cw_ref, cb_ref, wa_ref, wx_ref, ba_ref, bx_ref, lam_ref,
                hl_ref, hc_ref,
                xpad, cpad, a_f, u_f, a_b, u_b, ca_f, cu_f, ca_b, cu_b,
                ends_f, prods_f, carry_f, ends_b, prods_b, carry_b, c2_ref,
                *, t, tc):
    w = HEAD_W
    gw = GRID_W
    rows = t // gw
    lam = lam_ref[...]
    neg = -lam
    softplus = jnp.maximum(neg, 0.0) + jnp.log(1.0 + jnp.exp(-jnp.abs(neg)))
    c2_ref[...] = (-0.5 * LRU_C * LOG2E) * softplus
    w_all = (0.5 * jnp.concatenate([wa_ref[0, 0], wx_ref[0, 0], wa_ref[1, 0], wx_ref[1, 0]], axis=1)).astype(BF16)
    hb_all = 0.5 * jnp.concatenate([ba_ref[0:1, :], bx_ref[0:1, :], ba_ref[1:2, :], bx_ref[1:2, :]], axis=1)
    w0, w1, w2, w3 = (cw_ref[k:k + 1, :] for k in range(4))
    cb = cb_ref[...]

    cpad[...] = jnp.zeros_like(cpad)
    cpad[8:8 + tc, :] = xc_ref[0, 0].astype(F32)
    xcc = (w0 * cpad[6:6 + tc, :] + w1 * cpad[7:7 + tc, :] + w2 * cpad[8:8 + tc, :]
           + w3 * cpad[9:9 + tc, :] + cb)
    ca_f[...], cu_f[...], ca_b[...], cu_b[...] = _lru_gates(xcc, w_all, hb_all, c2_ref)

    def ctx_step(i, hs):
        hf, hb = hs
        p = tc - 1 - i
        hf = ca_f[pl.ds(i, 1), :] * hf + cu_f[pl.ds(i, 1), :]
        hb = ca_b[pl.ds(p, 1), :] * hb + cu_b[pl.ds(p, 1), :]
        cu_f[pl.ds(i, 1), :] = hf
        cu_b[pl.ds(p, 1), :] = hb
        return hf, hb
    zero_row = jnp.zeros((1, w), F32)
    s0_f, s0_b = lax.fori_loop(0, tc, ctx_step, (zero_row, zero_row))
    hc_ref[0] = (cu_f[...] + cu_b[...]).astype(hc_ref.dtype)

    xpad[2 * gw:2 * gw + t, :] = xl_ref[0, 0].astype(F32)
    col = lax.broadcasted_iota(I32, (gw, w), 0)
    body0 = 2 * gw
    xpad[gw:2 * gw, :] = jnp.where(col == 0, 0.0, xpad[pl.ds(body0 + (rows - 1) * gw - 1, gw), :])
    xpad[0:gw, :] = jnp.where(col == 0, 0.0, xpad[pl.ds(body0 + (rows - 2) * gw - 1, gw), :])
    xpad[body0 + t:body0 + t + gw, :] = jnp.where(col == gw - 1, 0.0, xpad[pl.ds(body0 + 1, gw), :])

    gate_rows = min(t, 512)

    def gate_body(c, carry_):
        base = pl.multiple_of(c * gate_rows, gate_rows)
        xcv = (w0 * xpad[pl.ds(base, gate_rows), :] + w1 * xpad[pl.ds(base + gw, gate_rows), :]
               + w2 * xpad[pl.ds(base + 2 * gw, gate_rows), :]
               + w3 * xpad[pl.ds(base + 3 * gw, gate_rows), :] + cb)
        sl = pl.ds(base, gate_rows)
        a_f[sl, :], u_f[sl, :], a_b[sl, :], u_b[sl, :] = _lru_gates(xcv, w_all, hb_all, c2_ref)
        return carry_
    lax.fori_loop(0, t // gate_rows, gate_body, 0)

    def slab(r):
        return pl.ds(pl.multiple_of(r * gw, gw), gw)

    def p1(i, c):
        hf, pf, hb, pb = c
        sf, sb = slab(i), slab(rows - 1 - i)
        af, ab = a_f[sf, :], a_b[sb, :]
        return af * hf + u_f[sf, :], af * pf, ab * hb + u_b[sb, :], ab * pb
    zeros, ones = jnp.zeros((gw, w), F32), jnp.ones((gw, w), F32)
    ends_f[...], prods_f[...], ends_b[...], prods_b[...] = lax.fori_loop(0, rows, p1, (zeros, ones, zeros, ones))

    def chain(i, c):
        cf, cb = c
        jf, jb = pl.ds(i, 1), pl.ds(gw - 1 - i, 1)
        carry_f[jf, :] = cf
        carry_b[jb, :] = cb
        return prods_f[jf, :] * cf + ends_f[jf, :], prods_b[jb, :] * cb + ends_b[jb, :]
    lax.fori_loop(0, gw, chain, (s0_f, s0_b))

    def p2(i, c):
        hf, hb = c
        sf, sb = slab(i), slab(rows - 1 - i)
        hf = a_f[sf, :] * hf + u_f[sf, :]
        hb = a_b[sb, :] * hb + u_b[sb, :]
        u_f[sf, :] = hf
        u_b[sb, :] = hb
        return hf, hb
    lax.fori_loop(0, rows, p2, (carry_f[...], carry_b[...]))
    hl_ref[0] = (u_f[...] + u_b[...]).astype(hl_ref.dtype)


def _lru_call(p_lat, p_ctx, g_lat, g_ctx, conv_w, conv_b, wa, wx, ba, bx, lam):
    _, bsz, t, d = p_lat.shape
    tc = p_ctx.shape[2]
    w = HEAD_W
    nb = d // w
    gw = GRID_W
    vec = lambda n: pl.BlockSpec((n, w), lambda b, k: (0, k))
    mat = pl.BlockSpec((2, 1, w, w), lambda b, k: (0, k, 0, 0))
    return pl.pallas_call(
        functools.partial(_lru_kernel, t=t, tc=tc),
        grid=(bsz, nb),
        in_specs=[
            pl.BlockSpec((1, 1, t, w), lambda b, k: (g_lat, b, 0, k)),
            pl.BlockSpec((1, 1, tc, w), lambda b, k: (g_ctx, b, 0, k)),
            vec(4), vec(1), mat, mat, vec(2), vec(2), vec(2),
        ],
        out_specs=[pl.BlockSpec((1, t, w), lambda b, k: (b, 0, k)),
                   pl.BlockSpec((1, tc, w), lambda b, k: (b, 0, k))],
        out_shape=[jax.ShapeDtypeStruct((bsz, t, d), BF16), jax.ShapeDtypeStruct((bsz, tc, d), BF16)],
        scratch_shapes=[
            pltpu.VMEM((t + 3 * gw, w), F32), pltpu.VMEM((tc + 16, w), F32),
            pltpu.VMEM((t, w), F32), pltpu.VMEM((t, w), F32), pltpu.VMEM((t, w), F32), pltpu.VMEM((t, w), F32),
            pltpu.VMEM((tc, w), F32), pltpu.VMEM((tc, w), F32), pltpu.VMEM((tc, w), F32), pltpu.VMEM((tc, w), F32),
            pltpu.VMEM((gw, w), F32), pltpu.VMEM((gw, w), F32), pltpu.VMEM((gw, w), F32),
            pltpu.VMEM((gw, w), F32), pltpu.VMEM((gw, w), F32), pltpu.VMEM((gw, w), F32),
            pltpu.VMEM((2, w), F32),
        ],
        compiler_params=_params("arbitrary", "arbitrary"),
        name="rglru",
    )(p_lat, p_ctx, conv_w, conv_b.reshape(1, d), wa, wx, ba, bx, lam)


def _merge_kernel(of_ref, ob_ref, og_ref, hl_ref, ly_ref, ma_ref, mb_ref, x_ref, mod_ref, gn_ref,
                  wa_ref, wb_ref, wo_ref, lng_ref, lnb_ref, wrt_ref,
                  x1_ref, h2_ref, afft_ref, *, ctx_row):
    d = D_MODEL
    b = pl.program_id(0)
    row = b if ctx_row is None else ctx_row
    o = of_ref[0, 0] + ob_ref[0, 0]
    gn = gn_ref[...]
    parts = []
    for h in range(N_HEADS):
        oh = o[:, h * HEAD_W:(h + 1) * HEAD_W]
        ms = jnp.mean(oh * oh, axis=-1, keepdims=True)
        parts.append(oh * lax.rsqrt(ms + RMS_EPS) * gn)
    o_a = (jnp.concatenate(parts, axis=1) * og_ref[0, 0].astype(F32)).astype(BF16)
    y_a = jnp.dot(o_a, wa_ref[...], preferred_element_type=F32)
    y_b = jnp.dot((hl_ref[0].astype(F32) * ly_ref[0, 0].astype(F32)).astype(BF16), wb_ref[...],
                  preferred_element_type=F32)
    z = ma_ref[0, 0].astype(F32) * y_a + mb_ref[0, 0].astype(F32) * y_b
    y = jnp.dot(z.astype(BF16), wo_ref[...], preferred_element_type=F32)
    g1 = mod_ref[pl.ds(row, 1), 2 * d:3 * d]
    x1 = _layer_norm(ALPHA * x_ref[0] + g1 * y, lng_ref[0:1, :], lnb_ref[0:1, :])
    x1_ref[0] = x1
    sh2 = mod_ref[pl.ds(row, 1), 3 * d:4 * d]
    sc2 = mod_ref[pl.ds(row, 1), 4 * d:5 * d]
    h2 = x1 * (1.0 + sc2) + sh2
    h2_ref[0] = h2.astype(BF16)
    logits_t = lax.dot_general(wrt_ref[...], h2, (((1,), (1,)), ((), ())),
                               preferred_element_type=F32, precision=HIGHEST)
    pt = jnp.exp(logits_t - jnp.max(logits_t, axis=0, keepdims=True))
    afft_ref[0] = pt / jnp.sum(pt, axis=0, keepdims=True)


def _merge_call(o_f, o_b, p, groups, h_lru, x, mod_l, gn, wa, wb, wo, ln_g, ln_b, w_router, ctx_row, name):
    bsz, t, d = x.shape
    tm = min(t, 512)
    e = N_EXPERTS
    tile = lambda g: pl.BlockSpec((1, 1, tm, d), lambda b, i: (g, b, i, 0))
    tok = pl.BlockSpec((1, tm, d), lambda b, i: (b, i, 0))
    full = lambda shape: pl.BlockSpec(shape, lambda b, i: tuple(0 for _ in shape))
    return pl.pallas_call(
        functools.partial(_merge_kernel, ctx_row=ctx_row),
        grid=(bsz, t // tm),
        in_specs=[tile(0), tile(0), tile(groups["og"]), tok, tile(groups["ly"]), tile(groups["ma"]),
                  tile(groups["mb"]), tok, full((8, 6 * d)), full((1, HEAD_W)),
                  full((d, d)), full((d, d)), full((d, d)), full((2, d)), full((2, d)),
                  full((e, d))],
        out_specs=[tok, tok, pl.BlockSpec((1, e, tm), lambda b, i: (b, 0, i))],
        out_shape=[jax.ShapeDtypeStruct((bsz, t, d), F32), jax.ShapeDtypeStruct((bsz, t, d), BF16),
                   jax.ShapeDtypeStruct((bsz, e, t), F32)],
        compiler_params=_params("arbitrary", "arbitrary"),
        name=name,
    )(o_f, o_b, p, h_lru, p, p, p, x, mod_l, gn.reshape(1, HEAD_W), wa, wb, wo, ln_g, ln_b,
      w_router.T)


def _route_kernel(a_ref, posm_ref, off_ref, span_ref, *, cap, t, tt, grt):
    e = N_EXPERTS
    u = lax.bitcast_convert_type(a_ref[0], I32)
    thr = jnp.zeros((e, 1), I32)
    for bit in range(30, -1, -1):
        cand = thr | (1 << bit)
        cnt = jnp.sum((u >= cand).astype(F32), axis=1, keepdims=True)
        thr = jnp.where(cnt >= cap, cand, thr)
    gt = u > thr
    eq = u == thr
    need = cap - jnp.sum(gt.astype(F32), axis=1, keepdims=True)
    ri = lax.broadcasted_iota(I32, (tt, tt), 0)
    ci = lax.broadcasted_iota(I32, (tt, tt), 1)
    before = (ri < ci).astype(BF16)
    lane = lax.broadcasted_iota(I32, (e, OFF_STRIDE), 1)
    offs = jnp.zeros((e, OFF_STRIDE), F32)
    offs_end = jnp.zeros((e, OFF_STRIDE), F32)
    n_eq = jnp.zeros((e, 1), F32)
    n_sel = jnp.zeros((e, 1), F32)
    for j in range(t // tt):
        sl = slice(j * tt, (j + 1) * tt)
        eq_j = eq[:, sl]
        rank_eq = jnp.dot(eq_j.astype(BF16), before, preferred_element_type=F32) + n_eq
        sel_j = gt[:, sl] | (eq_j & (rank_eq < need))
        sel_f = sel_j.astype(F32)
        pos = jnp.dot(sel_j.astype(BF16), before, preferred_element_type=F32) + n_sel
        posm_ref[0, :, sl] = jnp.where(sel_j, pos.astype(I32), -1)
        offs = jnp.where(lane == j, n_sel, offs)
        n_eq = n_eq + jnp.sum(eq_j.astype(F32), axis=1, keepdims=True)
        n_sel = n_sel + jnp.sum(sel_f, axis=1, keepdims=True)
        offs_end = jnp.where(lane == j, n_sel, offs_end)
    nt = t // tt
    offs = jnp.where(lane == nt, n_sel, offs)
    off_ref[0] = offs.astype(I32)
    n_rt = cap // grt
    tile = lane < nt
    spans = jnp.zeros((e, OFF_STRIDE), F32)
    for i in range(n_rt):
        skipped = jnp.sum(jnp.where(tile & (offs_end <= i * grt), 1.0, 0.0), axis=1, keepdims=True)
        started = jnp.sum(jnp.where(tile & (offs < (i + 1) * grt), 1.0, 0.0), axis=1, keepdims=True)
        spans = jnp.where(lane == i, skipped, jnp.where(lane == n_rt + i, started, spans))
    span_ref[0] = spans.astype(I32)


def _route_call(aff_t, cap, tt, grt, name):
    bsz, e, t = aff_t.shape
    small = pl.BlockSpec((1, e, OFF_STRIDE), lambda b: (b, 0, 0))
    return pl.pallas_call(
        functools.partial(_route_kernel, cap=cap, t=t, tt=tt, grt=grt),
        grid=(bsz,),
        in_specs=[pl.BlockSpec((1, e, t), lambda b: (b, 0, 0))],
        out_specs=[pl.BlockSpec((1, e, t), lambda b: (b, 0, 0)), small, small],
        out_shape=[jax.ShapeDtypeStruct((bsz, e, t), I32), jax.ShapeDtypeStruct((bsz, e, OFF_STRIDE), I32),
                   jax.ShapeDtypeStruct((bsz, e, OFF_STRIDE), I32)],
        compiler_params=_params("arbitrary"),
        name=name,
    )(aff_t)


def _expert_kernel(span_ref, posm_ref, h_ref, wg_ref, wu_ref, wd_ref, y_ref, x_ref, *, cap, rt, tt, nt,
                   expert_major):
    b = pl.program_id(1 if expert_major else 0)
    e = pl.program_id(0 if expert_major else 1)
    base = (b * N_EXPERTS + e) * OFF_STRIDE
    x_ref[...] = jnp.zeros_like(x_ref)
    kt = min(nt, GATHER_WINDOW_TILES)
    kw = kt * tt
    rows = lax.broadcasted_iota(I32, (rt, kw), 0)
    lane = lax.broadcasted_iota(I32, (rt, kw), 1)
    n_rt = cap // rt

    for i in range(n_rt):
        first = span_ref[base + i]
        n_win = (span_ref[base + n_rt + i] - 1 - first) // kt + 1

        def body(wdx, carry, i=i, first=first):
            want = first + wdx * kt
            start = jnp.minimum(want, nt - kt)
            tok = pl.ds(pl.multiple_of(start * tt, tt), kw)
            hit = ((posm_ref[0, 0, :, tok] - i * rt) == rows) & (lane >= (want - start) * tt)
            x_ref[i * rt:(i + 1) * rt, :] += jnp.dot(hit.astype(BF16), h_ref[0, tok, :],
                                                     preferred_element_type=F32).astype(BF16)
            return carry
        lax.fori_loop(0, n_win, body, 0)
    wg, wu, wd = wg_ref[0, 0].astype(BF16), wu_ref[0, 0].astype(BF16), wd_ref[0, 0].astype(BF16)
    ft = min(cap, 256)
    for r in range(cap // ft):
        xb = x_ref[r * ft:(r + 1) * ft, :]
        g = jnp.dot(xb, wg, preferred_element_type=F32)
        u = jnp.dot(xb, wu, preferred_element_type=F32)
        hid = (_silu(g) * u).astype(BF16)
        y_ref[0, 0, r * ft:(r + 1) * ft, :] = jnp.dot(hid, wd, preferred_element_type=F32).astype(BF16)


def _expert_call(span_flat, posm, h2, w_gate, w_up, w_down, layer, cap, rt, tt, name):
    bsz, t, d = h2.shape
    _, e, _, f = w_gate.shape
    nt = t // tt
    expert_major = t < d
    if expert_major:
        grid = (e, bsz)
        be = lambda k, b: (b, k)
    else:
        grid = (bsz, e)
        be = lambda b, k: (b, k)
    tok_spec = pl.BlockSpec((1, t, d), lambda i, j, off: (be(i, j)[0], 0, 0),
                            pipeline_mode=None if expert_major else pl.Buffered(1))
    weight = lambda r, c: pl.BlockSpec((1, 1, r, c), lambda i, j, off: (layer, be(i, j)[1], 0, 0))
    grid_spec = pltpu.PrefetchScalarGridSpec(
        num_scalar_prefetch=1,
        grid=grid,
        in_specs=[
            pl.BlockSpec((1, 1, 1, t), lambda i, j, off: (*be(i, j), 0, 0)),
            tok_spec, weight(d, f), weight(d, f), weight(f, d),
        ],
        out_specs=pl.BlockSpec((1, 1, cap, d), lambda i, j, off: (*be(i, j), 0, 0)),
        scratch_shapes=[pltpu.VMEM((cap, d), BF16)],
    )
    return pl.pallas_call(
        functools.partial(_expert_kernel, cap=cap, rt=rt, tt=tt, nt=nt, expert_major=expert_major),
        grid_spec=grid_spec,
        out_shape=jax.ShapeDtypeStruct((bsz, e, cap, d), BF16),
        compiler_params=_params("arbitrary", "arbitrary"),
        name=name,
    )(span_flat, posm.reshape(bsz, e, 1, t), h2, w_gate, w_up, w_down)


def _combine_kernel(off_ref, posm_ref, afft_ref, x1_ref, y_hbm, mod_ref, lng_ref, lnb_ref, o_ref,
                    ybuf, pbuf, sem, acc_ref, *, rt, tt, nw, ctx_row, n_tiles):
    d = D_MODEL
    b = pl.program_id(0)
    j = pl.program_id(1)
    row = b if ctx_row is None else ctx_row
    shift = rt.bit_length() - 1
    step = b * n_tiles + j
    cur = lax.rem(step, 2)
    is_last = step == pl.num_programs(0) * n_tiles - 1

    def plan(bb, jj):
        out, n_used = [], jnp.int32(0)
        for e in range(N_EXPERTS):
            base = (bb * N_EXPERTS + e) * OFF_STRIDE
            lo = off_ref[base + jj]
            hi = off_ref[base + jj + 1]
            t0 = lax.shift_right_logical(lo, shift)
            n = jnp.where(hi > lo, lax.shift_right_logical(hi - 1, shift) - t0 + 1, 0)
            out.append((t0, n, n_used))
            n_used = n_used + n
        return out, n_used

    def for_used_windows(windows, fn):
        for e, (t0, n, s0) in enumerate(windows):
            for wdx in range(nw):
                pl.when(wdx < n)(functools.partial(fn, e, t0 + wdx, s0 + wdx))

    def window_copy(bb, e, window, slot, half):
        src = y_hbm.at[bb, e, pl.ds(pl.multiple_of(window * rt, rt), rt), :]
        dst = ybuf.at[half, pl.ds(pl.multiple_of(slot * rt, rt), rt), :]
        return pltpu.make_async_copy(src, dst, sem.at[half])

    def fetch(bb, windows, half):
        for_used_windows(windows, lambda e, window, slot: window_copy(bb, e, window, slot, half).start())

    windows, n_used = plan(b, j)
    n_groups = lax.shift_right_logical(n_used + (COMBINE_GROUP - 1), COMBINE_GROUP.bit_length() - 1)

    @pl.when(step == 0)
    def _():
        ybuf[...] = jnp.zeros_like(ybuf)
        fetch(b, windows, 0)

    @pl.when(jnp.logical_not(is_last))
    def _():
        wrap = j == n_tiles - 1
        b_next = jnp.where(wrap, b + 1, b)
        fetch(b_next, plan(b_next, jnp.where(wrap, 0, j + 1))[0], 1 - cur)

    def clear(s, carry):
        pbuf[pl.ds(pl.multiple_of(s * rt, rt), rt), :] = jnp.zeros((rt, tt), BF16)
        return carry
    lax.fori_loop(n_used, n_groups * COMBINE_GROUP, clear, 0)

    rows = lax.broadcasted_iota(I32, (rt, tt), 0)

    def weights(e, window, slot):
        pos = posm_ref[0, e]
        gate = afft_ref[0, e:e + 1, :]
        pbuf[pl.ds(pl.multiple_of(slot * rt, rt), rt), :] = jnp.where(
            (pos - window * rt) == rows, gate, 0.0).astype(BF16)
    for_used_windows(windows, weights)

    def drain(i, carry):
        window_copy(0, 0, 0, 0, cur).wait()
        return carry
    lax.fori_loop(0, n_used, drain, 0)

    acc_ref[...] = jnp.zeros_like(acc_ref)
    gk = COMBINE_GROUP * rt
    for g in range(pl.cdiv(N_EXPERTS * nw, COMBINE_GROUP)):
        @pl.when(g < n_groups)
        def _(g=g):
            acc_ref[...] += lax.dot_general(pbuf[g * gk:(g + 1) * gk, :], ybuf[cur, g * gk:(g + 1) * gk, :],
                                            (((0,), (0,)), ((), ())), preferred_element_type=F32)
    g2 = mod_ref[pl.ds(row, 1), 5 * d:6 * d]
    o_ref[0] = _layer_norm(ALPHA * x1_ref[0] + g2 * acc_ref[...], lng_ref[1:2, :], lnb_ref[1:2, :])


def _combine_call(off_flat, posm, aff_t, x1, y, mod_l, ln_g, ln_b, rt, tt, ctx_row, name):
    bsz, t, d = x1.shape
    e = N_EXPERTS
    cap = y.shape[2]
    nw = min(cap // rt, (tt - 1) // rt + 2)
    n_slots = pl.cdiv(e * nw, COMBINE_GROUP) * COMBINE_GROUP
    grid_spec = pltpu.PrefetchScalarGridSpec(
        num_scalar_prefetch=1,
        grid=(bsz, t // tt),
        in_specs=[
            pl.BlockSpec((1, e, 1, tt), lambda b, j, off: (b, 0, 0, j)),
            pl.BlockSpec((1, e, tt), lambda b, j, off: (b, 0, j)),
            pl.BlockSpec((1, tt, d), lambda b, j, off: (b, j, 0)),
            pl.BlockSpec(memory_space=pl.ANY),
            pl.BlockSpec((8, 6 * d), lambda b, j, off: (0, 0)),
            pl.BlockSpec((2, d), lambda b, j, off: (0, 0)),
            pl.BlockSpec((2, d), lambda b, j, off: (0, 0)),
        ],
        out_specs=pl.BlockSpec((1, tt, d), lambda b, j, off: (b, j, 0)),
        scratch_shapes=[pltpu.VMEM((2, n_slots * rt, d), BF16), pltpu.VMEM((n_slots * rt, tt), BF16),
                        pltpu.SemaphoreType.DMA((2,)), pltpu.VMEM((tt, d), F32)],
    )
    return pl.pallas_call(
        functools.partial(_combine_kernel, rt=rt, tt=tt, nw=nw, ctx_row=ctx_row, n_tiles=t // tt),
        grid_spec=grid_spec,
        out_shape=jax.ShapeDtypeStruct((bsz, t, d), F32),
        compiler_params=_params("arbitrary", "arbitrary"),
        name=name,
    )(off_flat, posm.reshape(bsz, e, 1, t), aff_t, x1, y, mod_l, ln_g, ln_b)


def _moe(x1, h2, aff_t, mod_l, w_gate, w_up, w_down, layer, ln_g, ln_b, ctx_row, tag):
    bsz, t, d = x1.shape
    cap = EC_FACTOR * t // N_EXPERTS
    tt = min(t, 256)
    grt = min(cap, GATHER_ROWS)
    posm, off, span = _route_call(aff_t, cap, tt, grt, "route_" + tag)
    y = _expert_call(span.reshape(-1), posm, h2, w_gate, w_up, w_down, layer, cap, grt, tt, "expert_" + tag)
    return _combine_call(off.reshape(-1), posm, aff_t, x1, y, mod_l, ln_g, ln_b, min(cap, SCATTER_ROWS), tt, ctx_row,
                         "combine_" + tag)


def kernel(x, c, ctx, c_ctx, w_mod, b_mod, w_in, hgrn_lb_logits, hgrn_norm_g, conv_w, conv_b, lru_wa,
           lru_ba, lru_wx, lru_bx, lru_lambda, w_branch_a, w_branch_b, w_out, ln_g, ln_b, w_router,
           w_gate, w_up, w_down):
    depth = w_in.shape[0]
    bsz = x.shape[0]
    ctx_row = bsz
    assert bsz < 8 and depth == DEPTH

    lb_cum = jnp.cumsum(jax.nn.softmax(hgrn_lb_logits.astype(F32), axis=0), axis=0)
    lbs = lb_cum - lb_cum[0]
    log1m_lb = jnp.log1p(-lbs)

    c_all = jnp.zeros((8, x.shape[2]), F32).at[:bsz].set(c).at[ctx_row].set(c_ctx)
    mod = _mod_call(c_all, w_mod, b_mod)

    p_index = P_INDEX
    for l in range(depth):
        need_ctx = l < depth - 1
        mod_l = mod[l]
        w_in_bf = w_in[l].astype(BF16)
        p_lat, lf_lat, kk_lat = _proj_call(x, mod_l, w_in_bf, lbs[l], log1m_lb[l], None, "proj_lat")
        p_ctx, lf_ctx, kk_ctx = _proj_call(ctx, mod_l, w_in_bf, lbs[l], log1m_lb[l], ctx_row,
                                           "proj_ctx")

        oc_f, oc_b, ol_f, ol_b = _hgrn_call(p_lat, lf_lat, kk_lat, p_ctx, lf_ctx, kk_ctx)
        h_lat, h_ctx = _lru_call(p_lat, p_ctx, p_index["lx"], p_index["lx"], conv_w[l], conv_b[l],
                                 lru_wa[l], lru_wx[l], lru_ba[l], lru_bx[l], lru_lambda[l])

        wa, wb, wo = (w.astype(BF16) for w in (w_branch_a[l], w_branch_b[l], w_out[l]))
        x1, h2, aff_t = _merge_call(ol_f, ol_b, p_lat, p_index, h_lat, x, mod_l, hgrn_norm_g[l],
                                         wa, wb, wo, ln_g[l], ln_b[l], w_router[l], None, "merge_lat")
        if need_ctx:
            c1, ch2, caff_t = _merge_call(oc_f, oc_b, p_ctx, p_index, h_ctx, ctx, mod_l, hgrn_norm_g[l],
                                                wa, wb, wo, ln_g[l], ln_b[l], w_router[l], ctx_row,
                                                "merge_ctx")
        x = _moe(x1, h2, aff_t, mod_l, w_gate, w_up, w_down, l, ln_g[l], ln_b[l], None, "lat")
        if need_ctx:
            ctx = _moe(c1, ch2, caff_t, mod_l, w_gate, w_up, w_down, l, ln_g[l], ln_b[l], ctx_row, "ctx")
    return x
```

```python
import functools

import jax
import jax.numpy as jnp
from jax import lax
from jax.experimental import pallas as pl
from jax.experimental.pallas import tpu as pltpu

F32 = jnp.float32
BF16 = jnp.bfloat16
I32 = jnp.int32
HIGHEST = lax.Precision.HIGHEST

DEPTH = 2
D_MODEL = 1024
GRID_W = 64
N_HEADS = 8
HEAD_W = 128
CHUNK = 64
N_EXPERTS = 16
EC_FACTOR = 2
LRU_C = 8.0
ALPHA = (2.0 * DEPTH) ** 0.25
LN_EPS = 1e-5
RMS_EPS = 1e-6
LN2 = 0.6931471805599453
LOG2E = 1.4426950408889634
SQRT_2_OVER_PI = 0.7978845608028654
EXP_CLAMP = 80.0

V7X_VMEM_LIMIT_BYTES = 56 * 1024 * 1024
OFF_STRIDE = 128
GATHER_WINDOW_TILES = 6
COMBINE_GROUP = 8
GATHER_ROWS = 128
SCATTER_ROWS = 64

G_Q, G_V, G_FF, G_FB, G_OG, G_LX, G_LY, G_MA, G_MB = range(9)


def _params(*sem):
    return pltpu.CompilerParams(dimension_semantics=sem, vmem_limit_bytes=V7X_VMEM_LIMIT_BYTES)


def _sigmoid(x):
    return 0.5 * jnp.tanh(0.5 * x) + 0.5


def _silu(x):
    h = 0.5 * x
    return h * jnp.tanh(h) + h


def _layer_norm(x, g, b):
    mu = jnp.mean(x, axis=-1, keepdims=True)
    xc = x - mu
    var = jnp.mean(xc * xc, axis=-1, keepdims=True)
    return xc * lax.rsqrt(var + LN_EPS) * g + b


def _mod_kernel(c_ref, w_ref, b_ref, o_ref):
    c = c_ref[...]
    s = _silu(c)
    o_ref[0] = jnp.dot(s, w_ref[0], preferred_element_type=F32, precision=HIGHEST) + b_ref[0]


def _mod_call(c_all, w_mod, b_mod):
    depth, d, n = w_mod.shape
    tn = 1536
    return pl.pallas_call(
        _mod_kernel,
        grid=(depth, n // tn),
        in_specs=[
            pl.BlockSpec((8, d), lambda l, j: (0, 0)),
            pl.BlockSpec((1, d, tn), lambda l, j: (l, 0, j)),
            pl.BlockSpec((1, 1, tn), lambda l, j: (l, 0, j)),
        ],
        out_specs=pl.BlockSpec((1, 8, tn), lambda l, j: (l, 0, j)),
        out_shape=jax.ShapeDtypeStruct((depth, 8, n), F32),
        compiler_params=_params("arbitrary", "arbitrary"),
        name="mod",
    )(c_all, w_mod, b_mod.reshape(depth, 1, n))


def _apply_act(act, p):
    if act == "id":
        return p
    if act == "silu_scale":
        return _silu(p) * (HEAD_W ** -0.5)
    if act == "silu":
        return _silu(p)
    if act == "sigmoid":
        return _sigmoid(p)
    if act == "gelu":
        ph = 0.5 * p
        return ph * jnp.tanh(p * ((0.044715 * SQRT_2_OVER_PI) * (p * p) + SQRT_2_OVER_PI)) + ph
    raise ValueError(act)


P_GROUPS = (G_Q, G_V, G_OG, G_LX, G_LY, G_MA, G_MB)
P_ACTS = ("silu_scale", "id", "silu", "id", "gelu", "sigmoid", "sigmoid")
P_INDEX = {"q": 0, "v": 1, "og": 2, "lx": 3, "ly": 4, "ma": 5, "mb": 6}


def _hgrn_gates(fp, lb, l1):
    kh = 0.5 * (1.0 - lb)
    p = kh * jnp.tanh(0.5 * fp)
    log_f = jnp.maximum(jnp.log((lb + kh) + p), jnp.minimum(fp, 0.0) + (l1 - LN2))
    return log_f, kh - p


def _proj_kernel(x_ref, mod_ref, w_ref, lb_ref, l1_ref, p_ref, lf_ref, kk_ref, *, ctx_row):
    d = D_MODEL
    b = pl.program_id(0)
    row = b if ctx_row is None else ctx_row
    sh = mod_ref[pl.ds(row, 1), 0:d]
    sc = mod_ref[pl.ds(row, 1), d:2 * d]
    h = (x_ref[0] * (1.0 + sc) + sh).astype(BF16)

    def group(col):
        return jnp.dot(h, w_ref[:, col * d:(col + 1) * d], preferred_element_type=F32)

    def plain(name):
        g = P_INDEX[name]
        p_ref[g, 0] = _apply_act(P_ACTS[g], group(P_GROUPS[g])).astype(BF16)

    def gate(dirn, col):
        log_f, k = _hgrn_gates(group(col), lb_ref[dirn:dirn + 1, :], l1_ref[dirn:dirn + 1, :])
        lf_ref[dirn, 0] = log_f
        kk_ref[dirn, 0] = k.astype(BF16)

    gate(0, G_FF)
    plain("v")
    gate(1, G_FB)
    plain("lx")
    plain("q")
    plain("ly")
    plain("og")
    plain("ma")
    plain("mb")


def _proj_call(x, mod_l, w_in_bf, lb, l1, ctx_row, name):
    bsz, t, d = x.shape
    tm = min(t, 512)
    ng = len(P_GROUPS)
    full = lambda shape: pl.BlockSpec(shape, lambda b, i: tuple(0 for _ in shape))
    out = lambda n: pl.BlockSpec((n, 1, tm, d), lambda b, i: (0, b, i, 0))
    return pl.pallas_call(
        functools.partial(_proj_kernel, ctx_row=ctx_row),
        grid=(bsz, t // tm),
        in_specs=[
            pl.BlockSpec((1, tm, d), lambda b, i: (b, i, 0)),
            full((8, 6 * d)),
            pl.BlockSpec(w_in_bf.shape, lambda b, i: (0, 0), pipeline_mode=pl.Buffered(1)),
            full((2, d)), full((2, d)),
        ],
        out_specs=[out(ng), out(2), out(2)],
        out_shape=[jax.ShapeDtypeStruct((ng, bsz, t, d), BF16), jax.ShapeDtypeStruct((2, bsz, t, d), F32),
                   jax.ShapeDtypeStruct((2, bsz, t, d), BF16)],
        compiler_params=_params("arbitrary", "arbitrary"),
        name=name,
    )(x, mod_l, w_in_bf, lb, l1)


def _hgrn_prepare(q_ref, v_ref, lf_ref, k_ref, row0, n, reverse):
    w = HEAD_W
    rows = pl.ds(row0, n * CHUNK)
    log_f = lf_ref[0, 0, rows, :]
    k = k_ref[0, 0, rows, :].astype(F32)
    q = q_ref[0, 0, rows, :]
    v = v_ref[0, 0, rows, :]

    ri = lax.broadcasted_iota(I32, (CHUNK, CHUNK), 0)
    ci = lax.broadcasted_iota(I32, (CHUNK, CHUNK), 1)
    causal = (ci >= ri) if reverse else (ci <= ri)
    tri = causal.astype(BF16)
    lf = jnp.concatenate([log_f[c * CHUNK:(c + 1) * CHUNK, :] for c in range(n)], axis=1)
    hi = lf.astype(BF16)
    rest = lf - hi.astype(F32)
    mid = rest.astype(BF16)
    lo = (rest - mid.astype(F32)).astype(BF16)
    bc = (jnp.dot(tri, hi, preferred_element_type=F32) + jnp.dot(tri, mid, preferred_element_type=F32)
          + jnp.dot(tri, lo, preferred_element_type=F32))

    half = CHUNK // 2
    chunks = []
    for c in range(n):
        sl = slice(c * CHUNK, (c + 1) * CHUNK)
        bcum = bc[:, c * w:(c + 1) * w]
        b_ref = bcum[half:half + 1, :]
        b_end = bcum[0:1, :] if reverse else bcum[CHUNK - 1:CHUNK, :]
        e1 = jnp.exp(jnp.minimum(bcum - b_ref, EXP_CLAMP))
        e2 = jnp.exp(jnp.minimum(b_ref - bcum, EXP_CLAMP))
        qt = q[sl].astype(F32) * e1
        kt = k[sl] * e2
        att = lax.dot_general(qt.astype(BF16), kt.astype(BF16), (((1,), (1,)), ((), ())),
                              preferred_element_type=F32)
        att = jnp.where(causal, att, 0.0).astype(BF16)
        o_intra = jnp.dot(att, v[sl], preferred_element_type=F32)
        qs = (qt * jnp.exp(b_ref)).astype(BF16)
        ke = (kt * jnp.exp(b_end - b_ref)).astype(BF16)
        upd = lax.dot_general(v[sl], ke, (((0,), (0,)), ((), ())), preferred_element_type=F32)
        chunks.append((o_intra, qs, upd, jnp.exp(b_end)))
    return chunks


def _hgrn_state_step(chunk, st, o_ref, row):
    o_intra, qs, upd, decay = chunk
    o_ref[0, 0, pl.ds(row, CHUNK), :] = o_intra + lax.dot_general(
        qs, st.astype(BF16), (((1,), (1,)), ((), ())), preferred_element_type=F32)
    return decay * st + upd


HGRN_SUB = 16


def _hgrn_kernel(qc_ref, vc_ref, lcf_ref, lcb_ref, kcf_ref, kcb_ref,
                 qf_ref, vf_ref, lff_ref, kf_ref, qb_ref, vb_ref, lfb_ref, kb_ref,
                 ocf_ref, ocb_ref, of_ref, ob_ref, sf_ref, sb_ref, *, n_ctx_chunks, n_lat_chunks):
    s = pl.program_id(2)

    def run(n_chunks, fwd, o_f, bwd, o_b):
        sub = min(n_chunks, HGRN_SUB)
        n_sub = n_chunks // sub

        def body(i, carry):
            rf = pl.multiple_of(i * (sub * CHUNK), sub * CHUNK)
            rb = pl.multiple_of((n_sub - 1 - i) * (sub * CHUNK), sub * CHUNK)
            cf = _hgrn_prepare(*fwd, rf, sub, False)
            cb = _hgrn_prepare(*bwd, rb, sub, True)
            st_f, st_b = sf_ref[...], sb_ref[...]
            for c in range(sub):
                st_f = _hgrn_state_step(cf[c], st_f, o_f, rf + c * CHUNK)
                st_b = _hgrn_state_step(cb[sub - 1 - c], st_b, o_b, rb + (sub - 1 - c) * CHUNK)
            sf_ref[...] = st_f
            sb_ref[...] = st_b
            return carry
        lax.fori_loop(0, n_sub, body, 0)

    @pl.when(s == 0)
    def _():
        sf_ref[...] = jnp.zeros_like(sf_ref)
        sb_ref[...] = jnp.zeros_like(sb_ref)
        run(n_ctx_chunks, (qc_ref, vc_ref, lcf_ref, kcf_ref), ocf_ref, (qc_ref, vc_ref, lcb_ref, kcb_ref), ocb_ref)

    run(n_lat_chunks, (qf_ref, vf_ref, lff_ref, kf_ref), of_ref, (qb_ref, vb_ref, lfb_ref, kb_ref), ob_ref)


def _hgrn_call(p_lat, lf_lat, kk_lat, p_ctx, lf_ctx, kk_ctx):
    _, bsz, t, d = p_lat.shape
    tc = p_ctx.shape[2]
    w = HEAD_W
    blk = min(t, 2048)
    ns = t // blk
    lat = lambda g, rev: pl.BlockSpec(
        (1, 1, blk, w), (lambda b, h, s: (g, b, ns - 1 - s, h)) if rev else (lambda b, h, s: (g, b, s, h)))
    ctx = lambda g: pl.BlockSpec((1, 1, tc, w), lambda b, h, s: (g, b, 0, h))
    return pl.pallas_call(
        functools.partial(_hgrn_kernel, n_ctx_chunks=tc // CHUNK, n_lat_chunks=blk // CHUNK),
        grid=(bsz, N_HEADS, ns),
        in_specs=[ctx(0), ctx(1), ctx(0), ctx(1), ctx(0), ctx(1),
                  lat(0, False), lat(1, False), lat(0, False), lat(0, False),
                  lat(0, True), lat(1, True), lat(1, True), lat(1, True)],
        out_specs=[ctx(0), ctx(0), lat(0, False), lat(0, True)],
        out_shape=[jax.ShapeDtypeStruct((1, bsz, tc, d), F32), jax.ShapeDtypeStruct((1, bsz, tc, d), F32),
                   jax.ShapeDtypeStruct((1, bsz, t, d), F32), jax.ShapeDtypeStruct((1, bsz, t, d), F32)],
        scratch_shapes=[pltpu.VMEM((w, w), F32), pltpu.VMEM((w, w), F32)],
        compiler_params=_params("arbitrary", "arbitrary", "arbitrary"),
        name="hgrn",
    )(p_ctx, p_ctx, lf_ctx, lf_ctx, kk_ctx, kk_ctx,
      p_lat, p_lat, lf_lat, kk_lat, p_lat, p_lat, lf_lat, kk_lat)


def _lru_gates(xc, w_all, hb_all, c2_ref):
    w = HEAD_W
    th = jnp.tanh(jnp.dot(xc.astype(BF16), w_all, preferred_element_type=F32) + hb_all)
    out = []
    for dirn in range(2):
        t_r = th[:, (2 * dirn) * w:(2 * dirn + 1) * w]
        t_i = th[:, (2 * dirn + 1) * w:(2 * dirn + 2) * w]
        c2 = c2_ref[dirn:dirn + 1, :]
        a = jnp.exp2(c2 * t_r + c2)
        y = 1.0 - a * a
        u = jnp.where(y > 0.0, y * lax.rsqrt(y), 0.0) * ((0.5 * t_i + 0.5) * xc)
        out += [a, u]
    return out


def _lru_kernel(xl_ref, xc_ref, cw_ref, cb_ref, wa_ref, wx_ref, ba_ref, bx_ref, lam_ref,
                hl_ref, hc_ref,
                xpad, cpad, a_f, u_f, a_b, u_b, ca_f, cu_f, ca_b, cu_b,
                ends_f, prods_f, carry_f, ends_b, prods_b, carry_b, c2_ref,
                *, t, tc):
    w = HEAD_W
    gw = GRID_W
    rows = t // gw
    lam = lam_ref[...]
    neg = -lam
    softplus = jnp.maximum(neg, 0.0) + jnp.log(1.0 + jnp.exp(-jnp.abs(neg)))
    c2_ref[...] = (-0.5 * LRU_C * LOG2E) * softplus
    w_all = (0.5 * jnp.concatenate([wa_ref[0, 0], wx_ref[0, 0], wa_ref[1, 0], wx_ref[1, 0]], axis=1)).astype(BF16)
    hb_all = 0.5 * jnp.concatenate([ba_ref[0:1, :], bx_ref[0:1, :], ba_ref[1:2, :], bx_ref[1:2, :]], axis=1)
    w0, w1, w2, w3 = (cw_ref[k:k + 1, :] for k in range(4))
    cb = cb_ref[...]

    cpad[...] = jnp.zeros_like(cpad)
    cpad[8:8 + tc, :] = xc_ref[0, 0].astype(F32)
    xcc = (w0 * cpad[6:6 + tc, :] + w1 * cpad[7:7 + tc, :] + w2 * cpad[8:8 + tc, :]
           + w3 * cpad[9:9 + tc, :] + cb)
    ca_f[...], cu_f[...], ca_b[...], cu_b[...] = _lru_gates(xcc, w_all, hb_all, c2_ref)

    def ctx_step(i, hs):
        hf, hb = hs
        p = tc - 1 - i
        hf = ca_f[pl.ds(i, 1), :] * hf + cu_f[pl.ds(i, 1), :]
        hb = ca_b[pl.ds(p, 1), :] * hb + cu_b[pl.ds(p, 1), :]
        cu_f[pl.ds(i, 1), :] = hf
        cu_b[pl.ds(p, 1), :] = hb
        return hf, hb
    zero_row = jnp.zeros((1, w), F32)
    s0_f, s0_b = lax.fori_loop(0, tc, ctx_step, (zero_row, zero_row), unroll=8)
    hc_ref[0] = (cu_f[...] + cu_b[...]).astype(hc_ref.dtype)

    xpad[2 * gw:2 * gw + t, :] = xl_ref[0, 0].astype(F32)
    col = lax.broadcasted_iota(I32, (gw, w), 0)
    body0 = 2 * gw
    xpad[gw:2 * gw, :] = jnp.where(col == 0, 0.0, xpad[pl.ds(body0 + (rows - 1) * gw - 1, gw), :])
    xpad[0:gw, :] = jnp.where(col == 0, 0.0, xpad[pl.ds(body0 + (rows - 2) * gw - 1, gw), :])
    xpad[body0 + t:body0 + t + gw, :] = jnp.where(col == gw - 1, 0.0, xpad[pl.ds(body0 + 1, gw), :])

    gate_rows = min(t, 512)

    def gate_body(c, carry_):
        base = pl.multiple_of(c * gate_rows, gate_rows)
        xcv = (w0 * xpad[pl.ds(base, gate_rows), :] + w1 * xpad[pl.ds(base + gw, gate_rows), :]
               + w2 * xpad[pl.ds(base + 2 * gw, gate_rows), :]
               + w3 * xpad[pl.ds(base + 3 * gw, gate_rows), :] + cb)
        sl = pl.ds(base, gate_rows)
        a_f[sl, :], u_f[sl, :], a_b[sl, :], u_b[sl, :] = _lru_gates(xcv, w_all, hb_all, c2_ref)
        return carry_
    lax.fori_loop(0, t // gate_rows, gate_body, 0)

    def slab(r):
        return pl.ds(pl.multiple_of(r * gw, gw), gw)

    def p1(i, c):
        hf, pf, hb, pb = c
        sf, sb = slab(i), slab(rows - 1 - i)
        af, ab = a_f[sf, :], a_b[sb, :]
        return af * hf + u_f[sf, :], af * pf, ab * hb + u_b[sb, :], ab * pb
    zeros, ones = jnp.zeros((gw, w), F32), jnp.ones((gw, w), F32)
    ends_f[...], prods_f[...], ends_b[...], prods_b[...] = lax.fori_loop(
        0, rows, p1, (zeros, ones, zeros, ones), unroll=4)

    def chain(i, c):
        cf, cb = c
        jf, jb = pl.ds(i, 1), pl.ds(gw - 1 - i, 1)
        carry_f[jf, :] = cf
        carry_b[jb, :] = cb
        return prods_f[jf, :] * cf + ends_f[jf, :], prods_b[jb, :] * cb + ends_b[jb, :]
    lax.fori_loop(0, gw, chain, (s0_f, s0_b), unroll=8)

    def p2(i, c):
        hf, hb = c
        sf, sb = slab(i), slab(rows - 1 - i)
        hf = a_f[sf, :] * hf + u_f[sf, :]
        hb = a_b[sb, :] * hb + u_b[sb, :]
        u_f[sf, :] = hf
        u_b[sb, :] = hb
        return hf, hb
    lax.fori_loop(0, rows, p2, (carry_f[...], carry_b[...]), unroll=4)
    hl_ref[0] = (u_f[...] + u_b[...]).astype(hl_ref.dtype)


def _lru_call(p_lat, p_ctx, g_lat, g_ctx, conv_w, conv_b, wa, wx, ba, bx, lam):
    _, bsz, t, d = p_lat.shape
    tc = p_ctx.shape[2]
    w = HEAD_W
    nb = d // w
    gw = GRID_W
    vec = lambda n: pl.BlockSpec((n, w), lambda b, k: (0, k))
    mat = pl.BlockSpec((2, 1, w, w), lambda b, k: (0, k, 0, 0))
    return pl.pallas_call(
        functools.partial(_lru_kernel, t=t, tc=tc),
        grid=(bsz, nb),
        in_specs=[
            pl.BlockSpec((1, 1, t, w), lambda b, k: (g_lat, b, 0, k)),
            pl.BlockSpec((1, 1, tc, w), lambda b, k: (g_ctx, b, 0, k)),
            vec(4), vec(1), mat, mat, vec(2), vec(2), vec(2),
        ],
        out_specs=[pl.BlockSpec((1, t, w), lambda b, k: (b, 0, k)),
                   pl.BlockSpec((1, tc, w), lambda b, k: (b, 0, k))],
        out_shape=[jax.ShapeDtypeStruct((bsz, t, d), BF16), jax.ShapeDtypeStruct((bsz, tc, d), BF16)],
        scratch_shapes=[
            pltpu.VMEM((t + 3 * gw, w), F32), pltpu.VMEM((tc + 16, w), F32),
            pltpu.VMEM((t, w), F32), pltpu.VMEM((t, w), F32), pltpu.VMEM((t, w), F32), pltpu.VMEM((t, w), F32),
            pltpu.VMEM((tc, w), F32), pltpu.VMEM((tc, w), F32), pltpu.VMEM((tc, w), F32), pltpu.VMEM((tc, w), F32),
            pltpu.VMEM((gw, w), F32), pltpu.VMEM((gw, w), F32), pltpu.VMEM((gw, w), F32),
            pltpu.VMEM((gw, w), F32), pltpu.VMEM((gw, w), F32), pltpu.VMEM((gw, w), F32),
            pltpu.VMEM((2, w), F32),
        ],
        compiler_params=_params("arbitrary", "arbitrary"),
        name="rglru",
    )(p_lat, p_ctx, conv_w, conv_b.reshape(1, d), wa, wx, ba, bx, lam)


def _merge_kernel(of_ref, ob_ref, og_ref, hl_ref, ly_ref, ma_ref, mb_ref, x_ref, mod_ref, gn_ref,
                  wa_ref, wb_ref, wo_ref, lng_ref, lnb_ref, wrt_ref,
                  x1_ref, h2_ref, afft_ref, *, ctx_row):
    d = D_MODEL
    b = pl.program_id(0)
    row = b if ctx_row is None else ctx_row
    o = of_ref[0, 0] + ob_ref[0, 0]
    gn = gn_ref[...]
    parts = []
    for h in range(N_HEADS):
        oh = o[:, h * HEAD_W:(h + 1) * HEAD_W]
        ms = jnp.mean(oh * oh, axis=-1, keepdims=True)
        parts.append(oh * lax.rsqrt(ms + RMS_EPS) * gn)
    o_a = (jnp.concatenate(parts, axis=1) * og_ref[0, 0].astype(F32)).astype(BF16)
    y_a = jnp.dot(o_a, wa_ref[...], preferred_element_type=F32)
    y_b = jnp.dot((hl_ref[0].astype(F32) * ly_ref[0, 0].astype(F32)).astype(BF16), wb_ref[...],
                  preferred_element_type=F32)
    z = ma_ref[0, 0].astype(F32) * y_a + mb_ref[0, 0].astype(F32) * y_b
    y = jnp.dot(z.astype(BF16), wo_ref[...], preferred_element_type=F32)
    g1 = mod_ref[pl.ds(row, 1), 2 * d:3 * d]
    x1 = _layer_norm(ALPHA * x_ref[0] + g1 * y, lng_ref[0:1, :], lnb_ref[0:1, :])
    x1_ref[0] = x1
    sh2 = mod_ref[pl.ds(row, 1), 3 * d:4 * d]
    sc2 = mod_ref[pl.ds(row, 1), 4 * d:5 * d]
    h2 = x1 * (1.0 + sc2) + sh2
    h2_ref[0] = h2.astype(BF16)
    logits_t = lax.dot_general(wrt_ref[...], h2, (((1,), (1,)), ((), ())),
                               preferred_element_type=F32, precision=HIGHEST)
    pt = jnp.exp(logits_t - jnp.max(logits_t, axis=0, keepdims=True))
    afft_ref[0] = pt / jnp.sum(pt, axis=0, keepdims=True)


def _merge_call(o_f, o_b, p, groups, h_lru, x, mod_l, gn, wa, wb, wo, ln_g, ln_b, w_router, ctx_row, name):
    bsz, t, d = x.shape
    tm = min(t, 512)
    e = N_EXPERTS
    tile = lambda g: pl.BlockSpec((1, 1, tm, d), lambda b, i: (g, b, i, 0))
    tok = pl.BlockSpec((1, tm, d), lambda b, i: (b, i, 0))
    full = lambda shape: pl.BlockSpec(shape, lambda b, i: tuple(0 for _ in shape))
    return pl.pallas_call(
        functools.partial(_merge_kernel, ctx_row=ctx_row),
        grid=(bsz, t // tm),
        in_specs=[tile(0), tile(0), tile(groups["og"]), tok, tile(groups["ly"]), tile(groups["ma"]),
                  tile(groups["mb"]), tok, full((8, 6 * d)), full((1, HEAD_W)),
                  full((d, d)), full((d, d)), full((d, d)), full((2, d)), full((2, d)),
                  full((e, d))],
        out_specs=[tok, tok, pl.BlockSpec((1, e, tm), lambda b, i: (b, 0, i))],
        out_shape=[jax.ShapeDtypeStruct((bsz, t, d), F32), jax.ShapeDtypeStruct((bsz, t, d), BF16),
                   jax.ShapeDtypeStruct((bsz, e, t), F32)],
        compiler_params=_params("arbitrary", "arbitrary"),
        name=name,
    )(o_f, o_b, p, h_lru, p, p, p, x, mod_l, gn.reshape(1, HEAD_W), wa, wb, wo, ln_g, ln_b,
      w_router.T)


def _route_kernel(a_ref, posm_ref, off_ref, span_ref, *, cap, t, tt, grt):
    e = N_EXPERTS
    u = lax.bitcast_convert_type(a_ref[0], I32)
    thr = jnp.zeros((e, 1), I32)
    for bit in range(30, -1, -1):
        cand = thr | (1 << bit)
        cnt = jnp.sum((u >= cand).astype(F32), axis=1, keepdims=True)
        thr = jnp.where(cnt >= cap, cand, thr)
    gt = u > thr
    eq = u == thr
    need = cap - jnp.sum(gt.astype(F32), axis=1, keepdims=True)
    ri = lax.broadcasted_iota(I32, (tt, tt), 0)
    ci = lax.broadcasted_iota(I32, (tt, tt), 1)
    before = (ri < ci).astype(BF16)
    lane = lax.broadcasted_iota(I32, (e, OFF_STRIDE), 1)
    offs = jnp.zeros((e, OFF_STRIDE), F32)
    offs_end = jnp.zeros((e, OFF_STRIDE), F32)
    n_eq = jnp.zeros((e, 1), F32)
    n_sel = jnp.zeros((e, 1), F32)
    for j in range(t // tt):
        sl = slice(j * tt, (j + 1) * tt)
        eq_j = eq[:, sl]
        rank_eq = jnp.dot(eq_j.astype(BF16), before, preferred_element_type=F32) + n_eq
        sel_j = gt[:, sl] | (eq_j & (rank_eq < need))
        sel_f = sel_j.astype(F32)
        pos = jnp.dot(sel_j.astype(BF16), before, preferred_element_type=F32) + n_sel
        posm_ref[0, :, sl] = jnp.where(sel_j, pos.astype(I32), -1)
        offs = jnp.where(lane == j, n_sel, offs)
        n_eq = n_eq + jnp.sum(eq_j.astype(F32), axis=1, keepdims=True)
        n_sel = n_sel + jnp.sum(sel_f, axis=1, keepdims=True)
        offs_end = jnp.where(lane == j, n_sel, offs_end)
    nt = t // tt
    offs = jnp.where(lane == nt, n_sel, offs)
    off_ref[0] = offs.astype(I32)
    n_rt = cap // grt
    tile = lane < nt
    spans = jnp.zeros((e, OFF_STRIDE), F32)
    for i in range(n_rt):
        skipped = jnp.sum(jnp.where(tile & (offs_end <= i * grt), 1.0, 0.0), axis=1, keepdims=True)
        started = jnp.sum(jnp.where(tile & (offs < (i + 1) * grt), 1.0, 0.0), axis=1, keepdims=True)
        spans = jnp.where(lane == i, skipped, jnp.where(lane == n_rt + i, started, spans))
    span_ref[0] = spans.astype(I32)


def _route_call(aff_t, cap, tt, grt, name):
    bsz, e, t = aff_t.shape
    small = pl.BlockSpec((1, e, OFF_STRIDE), lambda b: (b, 0, 0))
    return pl.pallas_call(
        functools.partial(_route_kernel, cap=cap, t=t, tt=tt, grt=grt),
        grid=(bsz,),
        in_specs=[pl.BlockSpec((1, e, t), lambda b: (b, 0, 0))],
        out_specs=[pl.BlockSpec((1, e, t), lambda b: (b, 0, 0)), small, small],
        out_shape=[jax.ShapeDtypeStruct((bsz, e, t), I32), jax.ShapeDtypeStruct((bsz, e, OFF_STRIDE), I32),
                   jax.ShapeDtypeStruct((bsz, e, OFF_STRIDE), I32)],
        compiler_params=_params("arbitrary"),
        name=name,
    )(aff_t)


def _expert_kernel(span_ref, posm_ref, h_ref, wg_ref, wu_ref, wd_ref, y_ref, x_ref, *, cap, rt, tt, nt,
                   expert_major):
    b = pl.program_id(1 if expert_major else 0)
    e = pl.program_id(0 if expert_major else 1)
    base = (b * N_EXPERTS + e) * OFF_STRIDE
    x_ref[...] = jnp.zeros_like(x_ref)
    kt = min(nt, GATHER_WINDOW_TILES)
    kw = kt * tt
    rows = lax.broadcasted_iota(I32, (rt, kw), 0)
    lane = lax.broadcasted_iota(I32, (rt, kw), 1)
    n_rt = cap // rt

    for i in range(n_rt):
        first = span_ref[base + i]
        n_win = (span_ref[base + n_rt + i] - 1 - first) // kt + 1

        def body(wdx, carry, i=i, first=first):
            want = first + wdx * kt
            start = jnp.minimum(want, nt - kt)
            tok = pl.ds(pl.multiple_of(start * tt, tt), kw)
            hit = ((posm_ref[0, 0, :, tok] - i * rt) == rows) & (lane >= (want - start) * tt)
            x_ref[i * rt:(i + 1) * rt, :] += jnp.dot(hit.astype(BF16), h_ref[0, tok, :],
                                                     preferred_element_type=F32).astype(BF16)
            return carry
        lax.fori_loop(0, n_win, body, 0)
    wg, wu, wd = wg_ref[0, 0].astype(BF16), wu_ref[0, 0].astype(BF16), wd_ref[0, 0].astype(BF16)
    ft = min(cap, 256)
    for r in range(cap // ft):
        xb = x_ref[r * ft:(r + 1) * ft, :]
        g = jnp.dot(xb, wg, preferred_element_type=F32)
        u = jnp.dot(xb, wu, preferred_element_type=F32)
        hid = (_silu(g) * u).astype(BF16)
        y_ref[0, 0, r * ft:(r + 1) * ft, :] = jnp.dot(hid, wd, preferred_element_type=F32).astype(BF16)


def _expert_call(span_flat, posm, h2, w_gate, w_up, w_down, layer, cap, rt, tt, name):
    bsz, t, d = h2.shape
    _, e, _, f = w_gate.shape
    nt = t // tt
    expert_major = t < d
    if expert_major:
        grid = (e, bsz)
        be = lambda k, b: (b, k)
    else:
        grid = (bsz, e)
        be = lambda b, k: (b, k)
    tok_spec = pl.BlockSpec((1, t, d), lambda i, j, off: (be(i, j)[0], 0, 0),
                            pipeline_mode=None if expert_major else pl.Buffered(1))
    weight = lambda r, c: pl.BlockSpec((1, 1, r, c), lambda i, j, off: (layer, be(i, j)[1], 0, 0))
    grid_spec = pltpu.PrefetchScalarGridSpec(
        num_scalar_prefetch=1,
        grid=grid,
        in_specs=[
            pl.BlockSpec((1, 1, 1, t), lambda i, j, off: (*be(i, j), 0, 0)),
            tok_spec, weight(d, f), weight(d, f), weight(f, d),
        ],
        out_specs=pl.BlockSpec((1, 1, cap, d), lambda i, j, off: (*be(i, j), 0, 0)),
        scratch_shapes=[pltpu.VMEM((cap, d), BF16)],
    )
    return pl.pallas_call(
        functools.partial(_expert_kernel, cap=cap, rt=rt, tt=tt, nt=nt, expert_major=expert_major),
        grid_spec=grid_spec,
        out_shape=jax.ShapeDtypeStruct((bsz, e, cap, d), BF16),
        compiler_params=_params("arbitrary", "arbitrary"),
        name=name,
    )(span_flat, posm.reshape(bsz, e, 1, t), h2, w_gate, w_up, w_down)


def _combine_kernel(off_ref, posm_ref, afft_ref, x1_ref, y_hbm, mod_ref, lng_ref, lnb_ref, o_ref,
                    ybuf, pbuf, sem, acc_ref, *, rt, tt, nw, ctx_row, n_tiles):
    d = D_MODEL
    b = pl.program_id(0)
    j = pl.program_id(1)
    row = b if ctx_row is None else ctx_row
    shift = rt.bit_length() - 1
    step = b * n_tiles + j
    cur = lax.rem(step, 2)
    is_last = step == pl.num_programs(0) * n_tiles - 1

    def plan(bb, jj):
        out, n_used = [], jnp.int32(0)
        for e in range(N_EXPERTS):
            base = (bb * N_EXPERTS + e) * OFF_STRIDE
            lo = off_ref[base + jj]
            hi = off_ref[base + jj + 1]
            t0 = lax.shift_right_logical(lo, shift)
            n = jnp.where(hi > lo, lax.shift_right_logical(hi - 1, shift) - t0 + 1, 0)
            out.append((t0, n, n_used))
            n_used = n_used + n
        return out, n_used

    def for_used_windows(windows, fn):
        for e, (t0, n, s0) in enumerate(windows):
            for wdx in range(nw):
                pl.when(wdx < n)(functools.partial(fn, e, t0 + wdx, s0 + wdx))

    def window_copy(bb, e, window, slot, half):
        src = y_hbm.at[bb, e, pl.ds(pl.multiple_of(window * rt, rt), rt), :]
        dst = ybuf.at[half, pl.ds(pl.multiple_of(slot * rt, rt), rt), :]
        return pltpu.make_async_copy(src, dst, sem.at[half])

    def fetch(bb, windows, half):
        for_used_windows(windows, lambda e, window, slot: window_copy(bb, e, window, slot, half).start())

    windows, n_used = plan(b, j)
    n_groups = lax.shift_right_logical(n_used + (COMBINE_GROUP - 1), COMBINE_GROUP.bit_length() - 1)

    @pl.when(step == 0)
    def _():
        ybuf[...] = jnp.zeros_like(ybuf)
        fetch(b, windows, 0)

    @pl.when(jnp.logical_not(is_last))
    def _():
        wrap = j == n_tiles - 1
        b_next = jnp.where(wrap, b + 1, b)
        fetch(b_next, plan(b_next, jnp.where(wrap, 0, j + 1))[0], 1 - cur)

    def clear(s, carry):
        pbuf[pl.ds(pl.multiple_of(s * rt, rt), rt), :] = jnp.zeros((rt, tt), BF16)
        return carry
    lax.fori_loop(n_used, n_groups * COMBINE_GROUP, clear, 0)

    rows = lax.broadcasted_iota(I32, (rt, tt), 0)

    def weights(e, window, slot):
        pos = posm_ref[0, e]
        gate = afft_ref[0, e:e + 1, :]
        pbuf[pl.ds(pl.multiple_of(slot * rt, rt), rt), :] = jnp.where(
            (pos - window * rt) == rows, gate, 0.0).astype(BF16)
    for_used_windows(windows, weights)

    def drain(i, carry):
        window_copy(0, 0, 0, 0, cur).wait()
        return carry
    lax.fori_loop(0, n_used, drain, 0)

    acc_ref[...] = jnp.zeros_like(acc_ref)
    gk = COMBINE_GROUP * rt
    for g in range(pl.cdiv(N_EXPERTS * nw, COMBINE_GROUP)):
        @pl.when(g < n_groups)
        def _(g=g):
            acc_ref[...] += lax.dot_general(pbuf[g * gk:(g + 1) * gk, :], ybuf[cur, g * gk:(g + 1) * gk, :],
                                            (((0,), (0,)), ((), ())), preferred_element_type=F32)
    g2 = mod_ref[pl.ds(row, 1), 5 * d:6 * d]
    o_ref[0] = _layer_norm(ALPHA * x1_ref[0] + g2 * acc_ref[...], lng_ref[1:2, :], lnb_ref[1:2, :])


def _combine_call(off_flat, posm, aff_t, x1, y, mod_l, ln_g, ln_b, rt, tt, ctx_row, name):
    bsz, t, d = x1.shape
    e = N_EXPERTS
    cap = y.shape[2]
    nw = min(cap // rt, (tt - 1) // rt + 2)
    n_slots = pl.cdiv(e * nw, COMBINE_GROUP) * COMBINE_GROUP
    grid_spec = pltpu.PrefetchScalarGridSpec(
        num_scalar_prefetch=1,
        grid=(bsz, t // tt),
        in_specs=[
            pl.BlockSpec((1, e, 1, tt), lambda b, j, off: (b, 0, 0, j)),
            pl.BlockSpec((1, e, tt), lambda b, j, off: (b, 0, j)),
            pl.BlockSpec((1, tt, d), lambda b, j, off: (b, j, 0)),
            pl.BlockSpec(memory_space=pl.ANY),
            pl.BlockSpec((8, 6 * d), lambda b, j, off: (0, 0)),
            pl.BlockSpec((2, d), lambda b, j, off: (0, 0)),
            pl.BlockSpec((2, d), lambda b, j, off: (0, 0)),
        ],
        out_specs=pl.BlockSpec((1, tt, d), lambda b, j, off: (b, j, 0)),
        scratch_shapes=[pltpu.VMEM((2, n_slots * rt, d), BF16), pltpu.VMEM((n_slots * rt, tt), BF16),
                        pltpu.SemaphoreType.DMA((2,)), pltpu.VMEM((tt, d), F32)],
    )
    return pl.pallas_call(
        functools.partial(_combine_kernel, rt=rt, tt=tt, nw=nw, ctx_row=ctx_row, n_tiles=t // tt),
        grid_spec=grid_spec,
        out_shape=jax.ShapeDtypeStruct((bsz, t, d), F32),
        compiler_params=_params("arbitrary", "arbitrary"),
        name=name,
    )(off_flat, posm.reshape(bsz, e, 1, t), aff_t, x1, y, mod_l, ln_g, ln_b)


def _moe(x1, h2, aff_t, mod_l, w_gate, w_up, w_down, layer, ln_g, ln_b, ctx_row, tag):
    bsz, t, d = x1.shape
    cap = EC_FACTOR * t // N_EXPERTS
    tt = min(t, 256)
    grt = min(cap, GATHER_ROWS)
    posm, off, span = _route_call(aff_t, cap, tt, grt, "route_" + tag)
    y = _expert_call(span.reshape(-1), posm, h2, w_gate, w_up, w_down, layer, cap, grt, tt, "expert_" + tag)
    return _combine_call(off.reshape(-1), posm, aff_t, x1, y, mod_l, ln_g, ln_b, min(cap, SCATTER_ROWS), tt, ctx_row,
                         "combine_" + tag)


def kernel(x, c, ctx, c_ctx, w_mod, b_mod, w_in, hgrn_lb_logits, hgrn_norm_g, conv_w, conv_b, lru_wa,
           lru_ba, lru_wx, lru_bx, lru_lambda, w_branch_a, w_branch_b, w_out, ln_g, ln_b, w_router,
           w_gate, w_up, w_down):
    depth = w_in.shape[0]
    bsz = x.shape[0]
    ctx_row = bsz
    assert bsz < 8 and depth == DEPTH

    lb_cum = jnp.cumsum(jax.nn.softmax(hgrn_lb_logits.astype(F32), axis=0), axis=0)
    lbs = lb_cum - lb_cum[0]
    log1m_lb = jnp.log1p(-lbs)

    c_all = jnp.zeros((8, x.shape[2]), F32).at[:bsz].set(c).at[ctx_row].set(c_ctx)
    mod = _mod_call(c_all, w_mod, b_mod)

    p_index = P_INDEX
    for l in range(depth):
        need_ctx = l < depth - 1
        mod_l = mod[l]
        w_in_bf = w_in[l].astype(BF16)
        p_lat, lf_lat, kk_lat = _proj_call(x, mod_l, w_in_bf, lbs[l], log1m_lb[l], None, "proj_lat")
        p_ctx, lf_ctx, kk_ctx = _proj_call(ctx, mod_l, w_in_bf, lbs[l], log1m_lb[l], ctx_row,
                                           "proj_ctx")

        oc_f, oc_b, ol_f, ol_b = _hgrn_call(p_lat, lf_lat, kk_lat, p_ctx, lf_ctx, kk_ctx)
        h_lat, h_ctx = _lru_call(p_lat, p_ctx, p_index["lx"], p_index["lx"], conv_w[l], conv_b[l],
                                 lru_wa[l], lru_wx[l], lru_ba[l], lru_bx[l], lru_lambda[l])

        wa, wb, wo = (w.astype(BF16) for w in (w_branch_a[l], w_branch_b[l], w_out[l]))
        x1, h2, aff_t = _merge_call(ol_f, ol_b, p_lat, p_index, h_lat, x, mod_l, hgrn_norm_g[l],
                                         wa, wb, wo, ln_g[l], ln_b[l], w_router[l], None, "merge_lat")
        if need_ctx:
            c1, ch2, caff_t = _merge_call(oc_f, oc_b, p_ctx, p_index, h_ctx, ctx, mod_l, hgrn_norm_g[l],
                                                wa, wb, wo, ln_g[l], ln_b[l], w_router[l], ctx_row,
                                                "merge_ctx")
        x = _moe(x1, h2, aff_t, mod_l, w_gate, w_up, w_down, l, ln_g[l], ln_b[l], None, "lat")
        if need_ctx:
            ctx = _moe(c1, ch2, caff_t, mod_l, w_gate, w_up, w_down, l, ln_g[l], ln_b[l], ctx_row, "ctx")
    return x
```

```python
import functools

import jax
import jax.numpy as jnp
from jax import lax
from jax.experimental import pallas as pl
from jax.experimental.pallas import tpu as pltpu

F32 = jnp.float32
BF16 = jnp.bfloat16
I32 = jnp.int32
HIGHEST = lax.Precision.HIGHEST

DEPTH = 2
D_MODEL = 1024
GRID_W = 64
N_HEADS = 8
HEAD_W = 128
CHUNK = 64
N_EXPERTS = 16
EC_FACTOR = 2
LRU_C = 8.0
ALPHA = (2.0 * DEPTH) ** 0.25
LN_EPS = 1e-5
RMS_EPS = 1e-6
LN2 = 0.6931471805599453
LOG2E = 1.4426950408889634
SQRT_2_OVER_PI = 0.7978845608028654
EXP_CLAMP = 80.0

V7X_VMEM_LIMIT_BYTES = 56 * 1024 * 1024
OFF_STRIDE = 128
GATHER_WINDOW_TILES = 6
COMBINE_GROUP = 8
GATHER_ROWS = 128
SCATTER_ROWS = 64

G_Q, G_V, G_FF, G_FB, G_OG, G_LX, G_LY, G_MA, G_MB = range(9)


def _params(*sem):
    return pltpu.CompilerParams(dimension_semantics=sem, vmem_limit_bytes=V7X_VMEM_LIMIT_BYTES)


def _sigmoid(x):
    return 0.5 * jnp.tanh(0.5 * x) + 0.5


def _silu(x):
    h = 0.5 * x
    return h * jnp.tanh(h) + h


def _layer_norm(x, g, b):
    mu = jnp.mean(x, axis=-1, keepdims=True)
    xc = x - mu
    var = jnp.mean(xc * xc, axis=-1, keepdims=True)
    return xc * lax.rsqrt(var + LN_EPS) * g + b


def _mod_kernel(c_ref, w_ref, b_ref, o_ref):
    c = c_ref[...]
    s = _silu(c)
    o_ref[0] = jnp.dot(s, w_ref[0], preferred_element_type=F32, precision=HIGHEST) + b_ref[0]


def _mod_call(c_all, w_mod, b_mod):
    depth, d, n = w_mod.shape
    tn = 1536
    return pl.pallas_call(
        _mod_kernel,
        grid=(depth, n // tn),
        in_specs=[
            pl.BlockSpec((8, d), lambda l, j: (0, 0)),
            pl.BlockSpec((1, d, tn), lambda l, j: (l, 0, j)),
            pl.BlockSpec((1, 1, tn), lambda l, j: (l, 0, j)),
        ],
        out_specs=pl.BlockSpec((1, 8, tn), lambda l, j: (l, 0, j)),
        out_shape=jax.ShapeDtypeStruct((depth, 8, n), F32),
        compiler_params=_params("arbitrary", "arbitrary"),
        name="mod",
    )(c_all, w_mod, b_mod.reshape(depth, 1, n))


def _apply_act(act, p):
    if act == "id":
        return p
    if act == "silu_scale":
        return _silu(p) * (HEAD_W ** -0.5)
    if act == "silu":
        return _silu(p)
    if act == "sigmoid":
        return _sigmoid(p)
    if act == "gelu":
        ph = 0.5 * p
        return ph * jnp.tanh(p * ((0.044715 * SQRT_2_OVER_PI) * (p * p) + SQRT_2_OVER_PI)) + ph
    raise ValueError(act)


P_GROUPS = (G_Q, G_V, G_OG, G_LX, G_LY, G_MA, G_MB)
P_ACTS = ("silu_scale", "id", "silu", "id", "gelu", "sigmoid", "sigmoid")
P_INDEX = {"q": 0, "v": 1, "og": 2, "lx": 3, "ly": 4, "ma": 5, "mb": 6}


def _hgrn_gates(fp, lb, l1):
    kh = 0.5 * (1.0 - lb)
    p = kh * jnp.tanh(0.5 * fp)
    log_f = jnp.maximum(jnp.log((lb + kh) + p), jnp.minimum(fp, 0.0) + (l1 - LN2))
    return log_f, kh - p


def _proj_kernel(x_ref, mod_ref, w_ref, lb_ref, l1_ref, p_ref, lf_ref, kk_ref, *, ctx_row):
    d = D_MODEL
    b = pl.program_id(0)
    row = b if ctx_row is None else ctx_row
    sh = mod_ref[pl.ds(row, 1), 0:d]
    sc = mod_ref[pl.ds(row, 1), d:2 * d]
    h = (x_ref[0] * (1.0 + sc) + sh).astype(BF16)

    def group(col):
        return jnp.dot(h, w_ref[:, col * d:(col + 1) * d], preferred_element_type=F32)

    def plain(name):
        g = P_INDEX[name]
        p_ref[g, 0] = _apply_act(P_ACTS[g], group(P_GROUPS[g])).astype(BF16)

    def gate(dirn, col):
        log_f, k = _hgrn_gates(group(col), lb_ref[dirn:dirn + 1, :], l1_ref[dirn:dirn + 1, :])
        lf_ref[dirn, 0] = log_f
        kk_ref[dirn, 0] = k.astype(BF16)

    gate(0, G_FF)
    plain("v")
    gate(1, G_FB)
    plain("lx")
    plain("q")
    plain("ly")
    plain("og")
    plain("ma")
    plain("mb")


def _proj_call(x, mod_l, w_in_bf, lb, l1, ctx_row, name):
    bsz, t, d = x.shape
    tm = min(t, 512)
    ng = len(P_GROUPS)
    full = lambda shape: pl.BlockSpec(shape, lambda b, i: tuple(0 for _ in shape))
    out = lambda n: pl.BlockSpec((n, 1, tm, d), lambda b, i: (0, b, i, 0))
    return pl.pallas_call(
        functools.partial(_proj_kernel, ctx_row=ctx_row),
        grid=(bsz, t // tm),
        in_specs=[
            pl.BlockSpec((1, tm, d), lambda b, i: (b, i, 0)),
            full((8, 6 * d)),
            pl.BlockSpec(w_in_bf.shape, lambda b, i: (0, 0), pipeline_mode=pl.Buffered(1)),
            full((2, d)), full((2, d)),
        ],
        out_specs=[out(ng), out(2), out(2)],
        out_shape=[jax.ShapeDtypeStruct((ng, bsz, t, d), BF16), jax.ShapeDtypeStruct((2, bsz, t, d), F32),
                   jax.ShapeDtypeStruct((2, bsz, t, d), BF16)],
        compiler_params=_params("arbitrary", "arbitrary"),
        name=name,
    )(x, mod_l, w_in_bf, lb, l1)


def _hgrn_prepare(q_ref, v_ref, lf_ref, k_ref, row0, n, reverse):
    w = HEAD_W
    rows = pl.ds(row0, n * CHUNK)
    log_f = lf_ref[0, 0, rows, :]
    k = k_ref[0, 0, rows, :].astype(F32)
    q = q_ref[0, 0, rows, :]
    v = v_ref[0, 0, rows, :]

    ri = lax.broadcasted_iota(I32, (CHUNK, CHUNK), 0)
    ci = lax.broadcasted_iota(I32, (CHUNK, CHUNK), 1)
    causal = (ci >= ri) if reverse else (ci <= ri)
    tri = causal.astype(BF16)
    lf = jnp.concatenate([log_f[c * CHUNK:(c + 1) * CHUNK, :] for c in range(n)], axis=1)
    hi = lf.astype(BF16)
    rest = lf - hi.astype(F32)
    mid = rest.astype(BF16)
    lo = (rest - mid.astype(F32)).astype(BF16)
    bc = (jnp.dot(tri, hi, preferred_element_type=F32) + jnp.dot(tri, mid, preferred_element_type=F32)
          + jnp.dot(tri, lo, preferred_element_type=F32))

    half = CHUNK // 2
    chunks = []
    for c in range(n):
        sl = slice(c * CHUNK, (c + 1) * CHUNK)
        bcum = bc[:, c * w:(c + 1) * w]
        b_ref = bcum[half:half + 1, :]
        b_end = bcum[0:1, :] if reverse else bcum[CHUNK - 1:CHUNK, :]
        e1 = jnp.exp(jnp.minimum(bcum - b_ref, EXP_CLAMP))
        e2 = jnp.exp(jnp.minimum(b_ref - bcum, EXP_CLAMP))
        qt = q[sl].astype(F32) * e1
        kt = k[sl] * e2
        att = lax.dot_general(qt.astype(BF16), kt.astype(BF16), (((1,), (1,)), ((), ())),
                              preferred_element_type=F32)
        att = jnp.where(causal, att, 0.0).astype(BF16)
        o_intra = jnp.dot(att, v[sl], preferred_element_type=F32)
        qs = (qt * jnp.exp(b_ref)).astype(BF16)
        ke = (kt * jnp.exp(b_end - b_ref)).astype(BF16)
        upd = lax.dot_general(v[sl], ke, (((0,), (0,)), ((), ())), preferred_element_type=F32)
        chunks.append((o_intra, qs, upd, jnp.exp(b_end)))
    return chunks


def _hgrn_state_step(chunk, st, o_ref, row):
    o_intra, qs, upd, decay = chunk
    o_ref[0, 0, pl.ds(row, CHUNK), :] = o_intra + lax.dot_general(
        qs, st.astype(BF16), (((1,), (1,)), ((), ())), preferred_element_type=F32)
    return decay * st + upd


HGRN_SUB = 16


def _hgrn_kernel(qc_ref, vc_ref, lcf_ref, lcb_ref, kcf_ref, kcb_ref,
                 qf_ref, vf_ref, lff_ref, kf_ref, qb_ref, vb_ref, lfb_ref, kb_ref,
                 ocf_ref, ocb_ref, of_ref, ob_ref, sf_ref, sb_ref, *, n_ctx_chunks, n_lat_chunks):
    s = pl.program_id(2)

    def run(n_chunks, fwd, o_f, bwd, o_b):
        sub = min(n_chunks, HGRN_SUB)
        n_sub = n_chunks // sub

        def body(i, carry):
            rf = pl.multiple_of(i * (sub * CHUNK), sub * CHUNK)
            rb = pl.multiple_of((n_sub - 1 - i) * (sub * CHUNK), sub * CHUNK)
            cf = _hgrn_prepare(*fwd, rf, sub, False)
            cb = _hgrn_prepare(*bwd, rb, sub, True)
            st_f, st_b = sf_ref[...], sb_ref[...]
            for c in range(sub):
                st_f = _hgrn_state_step(cf[c], st_f, o_f, rf + c * CHUNK)
                st_b = _hgrn_state_step(cb[sub - 1 - c], st_b, o_b, rb + (sub - 1 - c) * CHUNK)
            sf_ref[...] = st_f
            sb_ref[...] = st_b
            return carry
        lax.fori_loop(0, n_sub, body, 0)

    @pl.when(s == 0)
    def _():
        sf_ref[...] = jnp.zeros_like(sf_ref)
        sb_ref[...] = jnp.zeros_like(sb_ref)
        run(n_ctx_chunks, (qc_ref, vc_ref, lcf_ref, kcf_ref), ocf_ref, (qc_ref, vc_ref, lcb_ref, kcb_ref), ocb_ref)

    run(n_lat_chunks, (qf_ref, vf_ref, lff_ref, kf_ref), of_ref, (qb_ref, vb_ref, lfb_ref, kb_ref), ob_ref)


def _hgrn_call(p_lat, lf_lat, kk_lat, p_ctx, lf_ctx, kk_ctx):
    _, bsz, t, d = p_lat.shape
    tc = p_ctx.shape[2]
    w = HEAD_W
    blk = min(t, 2048)
    ns = t // blk
    lat = lambda g, rev: pl.BlockSpec(
        (1, 1, blk, w), (lambda b, h, s: (g, b, ns - 1 - s, h)) if rev else (lambda b, h, s: (g, b, s, h)))
    ctx = lambda g: pl.BlockSpec((1, 1, tc, w), lambda b, h, s: (g, b, 0, h))
    return pl.pallas_call(
        functools.partial(_hgrn_kernel, n_ctx_chunks=tc // CHUNK, n_lat_chunks=blk // CHUNK),
        grid=(bsz, N_HEADS, ns),
        in_specs=[ctx(0), ctx(1), ctx(0), ctx(1), ctx(0), ctx(1),
                  lat(0, False), lat(1, False), lat(0, False), lat(0, False),
                  lat(0, True), lat(1, True), lat(1, True), lat(1, True)],
        out_specs=[ctx(0), ctx(0), lat(0, False), lat(0, True)],
        out_shape=[jax.ShapeDtypeStruct((1, bsz, tc, d), F32), jax.ShapeDtypeStruct((1, bsz, tc, d), F32),
                   jax.ShapeDtypeStruct((1, bsz, t, d), F32), jax.ShapeDtypeStruct((1, bsz, t, d), F32)],
        scratch_shapes=[pltpu.VMEM((w, w), F32), pltpu.VMEM((w, w), F32)],
        compiler_params=_params("arbitrary", "arbitrary", "arbitrary"),
        name="hgrn",
    )(p_ctx, p_ctx, lf_ctx, lf_ctx, kk_ctx, kk_ctx,
      p_lat, p_lat, lf_lat, kk_lat, p_lat, p_lat, lf_lat, kk_lat)


def _lru_gates(xc, w_all, hb_all, c2_ref):
    w = HEAD_W
    th = jnp.tanh(jnp.dot(xc.astype(BF16), w_all, preferred_element_type=F32) + hb_all)
    out = []
    for dirn in range(2):
        t_r = th[:, (2 * dirn) * w:(2 * dirn + 1) * w]
        t_i = th[:, (2 * dirn + 1) * w:(2 * dirn + 2) * w]
        c2 = c2_ref[dirn:dirn + 1, :]
        a = jnp.exp2(c2 * t_r + c2)
        y = 1.0 - a * a
        u = jnp.where(y > 0.0, y * lax.rsqrt(y), 0.0) * ((0.5 * t_i + 0.5) * xc)
        out += [a, u]
    return out


def _lru_kernel(xl_ref, xc_ref, cw_ref, cb_ref, wa_ref, wx_ref, ba_ref, bx_ref, lam_ref,
                hl_ref, hc_ref,
                xpad, cpad, a_f, u_f, a_b, u_b, ca_f, cu_f, ca_b, cu_b,
                ends_f, prods_f, carry_f, ends_b, prods_b, carry_b, c2_ref,
                *, t, tc):
    w = HEAD_W
    gw = GRID_W
    rows = t // gw
    lam = lam_ref[...]
    neg = -lam
    softplus = jnp.maximum(neg, 0.0) + jnp.log(1.0 + jnp.exp(-jnp.abs(neg)))
    c2_ref[...] = (-0.5 * LRU_C * LOG2E) * softplus
    w_all = (0.5 * jnp.concatenate([wa_ref[0, 0], wx_ref[0, 0], wa_ref[1, 0], wx_ref[1, 0]], axis=1)).astype(BF16)
    hb_all = 0.5 * jnp.concatenate([ba_ref[0:1, :], bx_ref[0:1, :], ba_ref[1:2, :], bx_ref[1:2, :]], axis=1)
    w0, w1, w2, w3 = (cw_ref[k:k + 1, :] for k in range(4))
    cb = cb_ref[...]

    cpad[...] = jnp.zeros_like(cpad)
    cpad[8:8 + tc, :] = xc_ref[0, 0].astype(F32)
    xcc = (w0 * cpad[6:6 + tc, :] + w1 * cpad[7:7 + tc, :] + w2 * cpad[8:8 + tc, :]
           + w3 * cpad[9:9 + tc, :] + cb)
    ca_f[...], cu_f[...], ca_b[...], cu_b[...] = _lru_gates(xcc, w_all, hb_all, c2_ref)

    def ctx_step(i, hs):
        hf, hb = hs
        p = tc - 1 - i
        hf = ca_f[pl.ds(i, 1), :] * hf + cu_f[pl.ds(i, 1), :]
        hb = ca_b[pl.ds(p, 1), :] * hb + cu_b[pl.ds(p, 1), :]
        cu_f[pl.ds(i, 1), :] = hf
        cu_b[pl.ds(p, 1), :] = hb
        return hf, hb
    zero_row = jnp.zeros((1, w), F32)
    s0_f, s0_b = lax.fori_loop(0, tc, ctx_step, (zero_row, zero_row), unroll=8)
    hc_ref[0] = (cu_f[...] + cu_b[...]).astype(hc_ref.dtype)

    xpad[2 * gw:2 * gw + t, :] = xl_ref[0, 0].astype(F32)
    col = lax.broadcasted_iota(I32, (gw, w), 0)
    body0 = 2 * gw
    xpad[gw:2 * gw, :] = jnp.where(col == 0, 0.0, xpad[pl.ds(body0 + (rows - 1) * gw - 1, gw), :])
    xpad[0:gw, :] = jnp.where(col == 0, 0.0, xpad[pl.ds(body0 + (rows - 2) * gw - 1, gw), :])
    xpad[body0 + t:body0 + t + gw, :] = jnp.where(col == gw - 1, 0.0, xpad[pl.ds(body0 + 1, gw), :])

    gate_rows = min(t, 512)

    def gate_body(c, carry_):
        base = pl.multiple_of(c * gate_rows, gate_rows)
        xcv = (w0 * xpad[pl.ds(base, gate_rows), :] + w1 * xpad[pl.ds(base + gw, gate_rows), :]
               + w2 * xpad[pl.ds(base + 2 * gw, gate_rows), :]
               + w3 * xpad[pl.ds(base + 3 * gw, gate_rows), :] + cb)
        sl = pl.ds(base, gate_rows)
        a_f[sl, :], u_f[sl, :], a_b[sl, :], u_b[sl, :] = _lru_gates(xcv, w_all, hb_all, c2_ref)
        return carry_
    lax.fori_loop(0, t // gate_rows, gate_body, 0)

    def slab(r):
        return pl.ds(pl.multiple_of(r * gw, gw), gw)

    def p1(i, c):
        hf, pf, hb, pb = c
        sf, sb = slab(i), slab(rows - 1 - i)
        af, ab = a_f[sf, :], a_b[sb, :]
        return af * hf + u_f[sf, :], af * pf, ab * hb + u_b[sb, :], ab * pb
    zeros, ones = jnp.zeros((gw, w), F32), jnp.ones((gw, w), F32)
    ends_f[...], prods_f[...], ends_b[...], prods_b[...] = lax.fori_loop(
        0, rows, p1, (zeros, ones, zeros, ones), unroll=4)

    def chain(i, c):
        cf, cb = c
        jf, jb = pl.ds(i, 1), pl.ds(gw - 1 - i, 1)
        carry_f[jf, :] = cf
        carry_b[jb, :] = cb
        return prods_f[jf, :] * cf + ends_f[jf, :], prods_b[jb, :] * cb + ends_b[jb, :]
    lax.fori_loop(0, gw, chain, (s0_f, s0_b), unroll=8)

    def p2(i, c):
        hf, hb = c
        sf, sb = slab(i), slab(rows - 1 - i)
        hf = a_f[sf, :] * hf + u_f[sf, :]
        hb = a_b[sb, :] * hb + u_b[sb, :]
        u_f[sf, :] = hf
        u_b[sb, :] = hb
        return hf, hb
    lax.fori_loop(0, rows, p2, (carry_f[...], carry_b[...]), unroll=4)
    hl_ref[0] = (u_f[...] + u_b[...]).astype(hl_ref.dtype)


def _lru_call(p_lat, p_ctx, g_lat, g_ctx, conv_w, conv_b, wa, wx, ba, bx, lam):
    _, bsz, t, d = p_lat.shape
    tc = p_ctx.shape[2]
    w = HEAD_W
    nb = d // w
    gw = GRID_W
    vec = lambda n: pl.BlockSpec((n, w), lambda b, k: (0, k))
    mat = pl.BlockSpec((2, 1, w, w), lambda b, k: (0, k, 0, 0))
    return pl.pallas_call(
        functools.partial(_lru_kernel, t=t, tc=tc),
        grid=(bsz, nb),
        in_specs=[
            pl.BlockSpec((1, 1, t, w), lambda b, k: (g_lat, b, 0, k)),
            pl.BlockSpec((1, 1, tc, w), lambda b, k: (g_ctx, b, 0, k)),
            vec(4), vec(1), mat, mat, vec(2), vec(2), vec(2),
        ],
        out_specs=[pl.BlockSpec((1, t, w), lambda b, k: (b, 0, k)),
                   pl.BlockSpec((1, tc, w), lambda b, k: (b, 0, k))],
        out_shape=[jax.ShapeDtypeStruct((bsz, t, d), BF16), jax.ShapeDtypeStruct((bsz, tc, d), BF16)],
        scratch_shapes=[
            pltpu.VMEM((t + 3 * gw, w), F32), pltpu.VMEM((tc + 16, w), F32),
            pltpu.VMEM((t, w), F32), pltpu.VMEM((t, w), F32), pltpu.VMEM((t, w), F32), pltpu.VMEM((t, w), F32),
            pltpu.VMEM((tc, w), F32), pltpu.VMEM((tc, w), F32), pltpu.VMEM((tc, w), F32), pltpu.VMEM((tc, w), F32),
            pltpu.VMEM((gw, w), F32), pltpu.VMEM((gw, w), F32), pltpu.VMEM((gw, w), F32),
            pltpu.VMEM((gw, w), F32), pltpu.VMEM((gw, w), F32), pltpu.VMEM((gw, w), F32),
            pltpu.VMEM((2, w), F32),
        ],
        compiler_params=_params("arbitrary", "arbitrary"),
        name="rglru",
    )(p_lat, p_ctx, conv_w, conv_b.reshape(1, d), wa, wx, ba, bx, lam)


def _merge_kernel(of_ref, ob_ref, og_ref, hl_ref, ly_ref, ma_ref, mb_ref, x_ref, mod_ref, gn_ref,
                  wa_ref, wb_ref, wo_ref, lng_ref, lnb_ref, wrt_ref,
                  x1_ref, h2_ref, afft_ref, *, ctx_row):
    d = D_MODEL
    b = pl.program_id(0)
    row = b if ctx_row is None else ctx_row
    o = of_ref[0, 0] + ob_ref[0, 0]
    gn = gn_ref[...]
    parts = []
    for h in range(N_HEADS):
        oh = o[:, h * HEAD_W:(h + 1) * HEAD_W]
        ms = jnp.mean(oh * oh, axis=-1, keepdims=True)
        parts.append(oh * lax.rsqrt(ms + RMS_EPS) * gn)
    o_a = (jnp.concatenate(parts, axis=1) * og_ref[0, 0].astype(F32)).astype(BF16)
    y_a = jnp.dot(o_a, wa_ref[...], preferred_element_type=F32)
    y_b = jnp.dot((hl_ref[0].astype(F32) * ly_ref[0, 0].astype(F32)).astype(BF16), wb_ref[...],
                  preferred_element_type=F32)
    z = ma_ref[0, 0].astype(F32) * y_a + mb_ref[0, 0].astype(F32) * y_b
    y = jnp.dot(z.astype(BF16), wo_ref[...], preferred_element_type=F32)
    g1 = mod_ref[pl.ds(row, 1), 2 * d:3 * d]
    x1 = _layer_norm(ALPHA * x_ref[0] + g1 * y, lng_ref[0:1, :], lnb_ref[0:1, :])
    x1_ref[0] = x1
    sh2 = mod_ref[pl.ds(row, 1), 3 * d:4 * d]
    sc2 = mod_ref[pl.ds(row, 1), 4 * d:5 * d]
    h2 = x1 * (1.0 + sc2) + sh2
    h2_ref[0] = h2.astype(BF16)
    logits_t = lax.dot_general(wrt_ref[...], h2, (((1,), (1,)), ((), ())),
                               preferred_element_type=F32, precision=HIGHEST)
    pt = jnp.exp(logits_t - jnp.max(logits_t, axis=0, keepdims=True))
    afft_ref[0] = pt / jnp.sum(pt, axis=0, keepdims=True)


def _merge_call(o_f, o_b, p, groups, h_lru, x, mod_l, gn, wa, wb, wo, ln_g, ln_b, w_router, ctx_row, name):
    bsz, t, d = x.shape
    tm = min(t, 512)
    e = N_EXPERTS
    tile = lambda g: pl.BlockSpec((1, 1, tm, d), lambda b, i: (g, b, i, 0))
    tok = pl.BlockSpec((1, tm, d), lambda b, i: (b, i, 0))
    full = lambda shape: pl.BlockSpec(shape, lambda b, i: tuple(0 for _ in shape))
    return pl.pallas_call(
        functools.partial(_merge_kernel, ctx_row=ctx_row),
        grid=(bsz, t // tm),
        in_specs=[tile(0), tile(0), tile(groups["og"]), tok, tile(groups["ly"]), tile(groups["ma"]),
                  tile(groups["mb"]), tok, full((8, 6 * d)), full((1, HEAD_W)),
                  full((d, d)), full((d, d)), full((d, d)), full((2, d)), full((2, d)),
                  full((e, d))],
        out_specs=[tok, tok, pl.BlockSpec((1, e, tm), lambda b, i: (b, 0, i))],
        out_shape=[jax.ShapeDtypeStruct((bsz, t, d), F32), jax.ShapeDtypeStruct((bsz, t, d), BF16),
                   jax.ShapeDtypeStruct((bsz, e, t), F32)],
        compiler_params=_params("arbitrary", "arbitrary"),
        name=name,
    )(o_f, o_b, p, h_lru, p, p, p, x, mod_l, gn.reshape(1, HEAD_W), wa, wb, wo, ln_g, ln_b,
      w_router.T)


def _route_kernel(a_ref, posm_ref, off_ref, span_ref, *, cap, t, tt, grt):
    e = N_EXPERTS
    u = lax.bitcast_convert_type(a_ref[0], I32)
    thr = jnp.zeros((e, 1), I32)
    for bit in range(30, -1, -1):
        cand = thr | (1 << bit)
        cnt = jnp.sum((u >= cand).astype(F32), axis=1, keepdims=True)
        thr = jnp.where(cnt >= cap, cand, thr)
    gt = u > thr
    eq = u == thr
    need = cap - jnp.sum(gt.astype(F32), axis=1, keepdims=True)
    ri = lax.broadcasted_iota(I32, (tt, tt), 0)
    ci = lax.broadcasted_iota(I32, (tt, tt), 1)
    before = (ri < ci).astype(BF16)
    lane = lax.broadcasted_iota(I32, (e, OFF_STRIDE), 1)
    offs = jnp.zeros((e, OFF_STRIDE), F32)
    offs_end = jnp.zeros((e, OFF_STRIDE), F32)
    n_eq = jnp.zeros((e, 1), F32)
    n_sel = jnp.zeros((e, 1), F32)
    for j in range(t // tt):
        sl = slice(j * tt, (j + 1) * tt)
        eq_j = eq[:, sl]
        rank_eq = jnp.dot(eq_j.astype(BF16), before, preferred_element_type=F32) + n_eq
        sel_j = gt[:, sl] | (eq_j & (rank_eq < need))
        sel_f = sel_j.astype(F32)
        pos = jnp.dot(sel_j.astype(BF16), before, preferred_element_type=F32) + n_sel
        posm_ref[0, :, sl] = jnp.where(sel_j, pos.astype(I32), -1)
        offs = jnp.where(lane == j, n_sel, offs)
        n_eq = n_eq + jnp.sum(eq_j.astype(F32), axis=1, keepdims=True)
        n_sel = n_sel + jnp.sum(sel_f, axis=1, keepdims=True)
        offs_end = jnp.where(lane == j, n_sel, offs_end)
    nt = t // tt
    offs = jnp.where(lane == nt, n_sel, offs)
    off_ref[0] = offs.astype(I32)
    n_rt = cap // grt
    tile = lane < nt
    spans = jnp.zeros((e, OFF_STRIDE), F32)
    for i in range(n_rt):
        skipped = jnp.sum(jnp.where(tile & (offs_end <= i * grt), 1.0, 0.0), axis=1, keepdims=True)
        started = jnp.sum(jnp.where(tile & (offs < (i + 1) * grt), 1.0, 0.0), axis=1, keepdims=True)
        spans = jnp.where(lane == i, skipped, jnp.where(lane == n_rt + i, started, spans))
    span_ref[0] = spans.astype(I32)


def _route_call(aff_t, cap, tt, grt, name):
    bsz, e, t = aff_t.shape
    small = pl.BlockSpec((1, e, OFF_STRIDE), lambda b: (b, 0, 0))
    return pl.pallas_call(
        functools.partial(_route_kernel, cap=cap, t=t, tt=tt, grt=grt),
        grid=(bsz,),
        in_specs=[pl.BlockSpec((1, e, t), lambda b: (b, 0, 0))],
        out_specs=[pl.BlockSpec((1, e, t), lambda b: (b, 0, 0)), small, small],
        out_shape=[jax.ShapeDtypeStruct((bsz, e, t), I32), jax.ShapeDtypeStruct((bsz, e, OFF_STRIDE), I32),
                   jax.ShapeDtypeStruct((bsz, e, OFF_STRIDE), I32)],
        compiler_params=_params("arbitrary"),
        name=name,
    )(aff_t)


def _expert_kernel(span_ref, posm_ref, h_ref, wg_ref, wu_ref, wd_ref, y_ref, x_ref, *, cap, rt, tt, nt, nb):
    if nb == 1:
        b0, e = pl.program_id(0), pl.program_id(1)
    else:
        b0, e = 0, pl.program_id(0)
    x_ref[...] = jnp.zeros_like(x_ref)
    kt = min(nt, GATHER_WINDOW_TILES)
    kw = kt * tt
    rows = lax.broadcasted_iota(I32, (rt, kw), 0)
    lane = lax.broadcasted_iota(I32, (rt, kw), 1)
    n_rt = cap // rt

    for bb in range(nb):
        base = ((b0 + bb) * N_EXPERTS + e) * OFF_STRIDE
        for i in range(n_rt):
            first = span_ref[base + i]
            n_win = (span_ref[base + n_rt + i] - 1 - first) // kt + 1
            out_rows = slice((bb * n_rt + i) * rt, (bb * n_rt + i + 1) * rt)

            def body(wdx, carry, bb=bb, i=i, first=first, out_rows=out_rows):
                want = first + wdx * kt
                start = jnp.minimum(want, nt - kt)
                tok = pl.ds(pl.multiple_of(start * tt, tt), kw)
                hit = ((posm_ref[bb, 0, :, tok] - i * rt) == rows) & (lane >= (want - start) * tt)
                x_ref[out_rows, :] += jnp.dot(hit.astype(BF16), h_ref[bb, tok, :],
                                              preferred_element_type=F32).astype(BF16)
                return carry
            lax.fori_loop(0, n_win, body, 0)
    wg, wu, wd = wg_ref[0, 0].astype(BF16), wu_ref[0, 0].astype(BF16), wd_ref[0, 0].astype(BF16)
    total = nb * cap
    ft = min(total, 256)
    for r in range(total // ft):
        xb = x_ref[r * ft:(r + 1) * ft, :]
        g = jnp.dot(xb, wg, preferred_element_type=F32)
        u = jnp.dot(xb, wu, preferred_element_type=F32)
        hid = (_silu(g) * u).astype(BF16)
        out = jnp.dot(hid, wd, preferred_element_type=F32).astype(BF16)
        if ft <= cap:
            row0 = (r * ft) % cap
            y_ref[(r * ft) // cap, 0, row0:row0 + ft, :] = out
        else:
            for q in range(ft // cap):
                y_ref[r * (ft // cap) + q, 0, :, :] = out[q * cap:(q + 1) * cap, :]


def _expert_call(span_flat, posm, h2, w_gate, w_up, w_down, layer, cap, rt, tt, name):
    bsz, t, d = h2.shape
    _, e, _, f = w_gate.shape
    nt = t // tt
    if t < d:
        nb, grid = bsz, (e,)
        bk = lambda k, off: (0, k)
        tok_spec = pl.BlockSpec((bsz, t, d), lambda k, off: (0, 0, 0))
    else:
        nb, grid = 1, (bsz, e)
        bk = lambda b, k, off: (b, k)
        tok_spec = pl.BlockSpec((1, t, d), lambda b, k, off: (b, 0, 0), pipeline_mode=pl.Buffered(1))
    weight = lambda r, c: pl.BlockSpec((1, 1, r, c), lambda *i: (layer, bk(*i)[1], 0, 0))
    grid_spec = pltpu.PrefetchScalarGridSpec(
        num_scalar_prefetch=1,
        grid=grid,
        in_specs=[
            pl.BlockSpec((nb, 1, 1, t), lambda *i: (*bk(*i), 0, 0)),
            tok_spec, weight(d, f), weight(d, f), weight(f, d),
        ],
        out_specs=pl.BlockSpec((nb, 1, cap, d), lambda *i: (*bk(*i), 0, 0)),
        scratch_shapes=[pltpu.VMEM((nb * cap, d), BF16)],
    )
    return pl.pallas_call(
        functools.partial(_expert_kernel, cap=cap, rt=rt, tt=tt, nt=nt, nb=nb),
        grid_spec=grid_spec,
        out_shape=jax.ShapeDtypeStruct((bsz, e, cap, d), BF16),
        compiler_params=_params(*(("arbitrary",) * len(grid))),
        name=name,
    )(span_flat, posm.reshape(bsz, e, 1, t), h2, w_gate, w_up, w_down)


def _combine_kernel(off_ref, posm_ref, afft_ref, x1_ref, y_hbm, mod_ref, lng_ref, lnb_ref, o_ref,
                    ybuf, pbuf, sem, acc_ref, *, rt, tt, nw, ctx_row, n_tiles):
    d = D_MODEL
    b = pl.program_id(0)
    j = pl.program_id(1)
    row = b if ctx_row is None else ctx_row
    shift = rt.bit_length() - 1
    step = b * n_tiles + j
    cur = lax.rem(step, 2)
    is_last = step == pl.num_programs(0) * n_tiles - 1

    def plan(bb, jj):
        out, n_used = [], jnp.int32(0)
        for e in range(N_EXPERTS):
            base = (bb * N_EXPERTS + e) * OFF_STRIDE
            lo = off_ref[base + jj]
            hi = off_ref[base + jj + 1]
            t0 = lax.shift_right_logical(lo, shift)
            n = jnp.where(hi > lo, lax.shift_right_logical(hi - 1, shift) - t0 + 1, 0)
            out.append((t0, n, n_used))
            n_used = n_used + n
        return out, n_used

    def for_used_windows(windows, fn):
        for e, (t0, n, s0) in enumerate(windows):
            for wdx in range(nw):
                pl.when(wdx < n)(functools.partial(fn, e, t0 + wdx, s0 + wdx))

    def window_copy(bb, e, window, slot, half):
        src = y_hbm.at[bb, e, pl.ds(pl.multiple_of(window * rt, rt), rt), :]
        dst = ybuf.at[half, pl.ds(pl.multiple_of(slot * rt, rt), rt), :]
        return pltpu.make_async_copy(src, dst, sem.at[half])

    def fetch(bb, windows, half):
        for_used_windows(windows, lambda e, window, slot: window_copy(bb, e, window, slot, half).start())

    windows, n_used = plan(b, j)
    n_groups = lax.shift_right_logical(n_used + (COMBINE_GROUP - 1), COMBINE_GROUP.bit_length() - 1)

    @pl.when(step == 0)
    def _():
        ybuf[...] = jnp.zeros_like(ybuf)
        fetch(b, windows, 0)

    @pl.when(jnp.logical_not(is_last))
    def _():
        wrap = j == n_tiles - 1
        b_next = jnp.where(wrap, b + 1, b)
        fetch(b_next, plan(b_next, jnp.where(wrap, 0, j + 1))[0], 1 - cur)

    def clear(s, carry):
        pbuf[pl.ds(pl.multiple_of(s * rt, rt), rt), :] = jnp.zeros((rt, tt), BF16)
        return carry
    lax.fori_loop(n_used, n_groups * COMBINE_GROUP, clear, 0)

    rows = lax.broadcasted_iota(I32, (rt, tt), 0)

    def weights(e, window, slot):
        pos = posm_ref[0, e]
        gate = afft_ref[0, e:e + 1, :]
        pbuf[pl.ds(pl.multiple_of(slot * rt, rt), rt), :] = jnp.where(
            (pos - window * rt) == rows, gate, 0.0).astype(BF16)
    for_used_windows(windows, weights)

    def drain(i, carry):
        window_copy(0, 0, 0, 0, cur).wait()
        return carry
    lax.fori_loop(0, n_used, drain, 0)

    acc_ref[...] = jnp.zeros_like(acc_ref)
    gk = COMBINE_GROUP * rt
    for g in range(pl.cdiv(N_EXPERTS * nw, COMBINE_GROUP)):
        @pl.when(g < n_groups)
        def _(g=g):
            acc_ref[...] += lax.dot_general(pbuf[g * gk:(g + 1) * gk, :], ybuf[cur, g * gk:(g + 1) * gk, :],
                                            (((0,), (0,)), ((), ())), preferred_element_type=F32)
    g2 = mod_ref[pl.ds(row, 1), 5 * d:6 * d]
    o_ref[0] = _layer_norm(ALPHA * x1_ref[0] + g2 * acc_ref[...], lng_ref[1:2, :], lnb_ref[1:2, :])


def _combine_call(off_flat, posm, aff_t, x1, y, mod_l, ln_g, ln_b, rt, tt, ctx_row, name):
    bsz, t, d = x1.shape
    e = N_EXPERTS
    cap = y.shape[2]
    nw = min(cap // rt, (tt - 1) // rt + 2)
    n_slots = pl.cdiv(e * nw, COMBINE_GROUP) * COMBINE_GROUP
    grid_spec = pltpu.PrefetchScalarGridSpec(
        num_scalar_prefetch=1,
        grid=(bsz, t // tt),
        in_specs=[
            pl.BlockSpec((1, e, 1, tt), lambda b, j, off: (b, 0, 0, j)),
            pl.BlockSpec((1, e, tt), lambda b, j, off: (b, 0, j)),
            pl.BlockSpec((1, tt, d), lambda b, j, off: (b, j, 0)),
            pl.BlockSpec(memory_space=pl.ANY),
            pl.BlockSpec((8, 6 * d), lambda b, j, off: (0, 0)),
            pl.BlockSpec((2, d), lambda b, j, off: (0, 0)),
            pl.BlockSpec((2, d), lambda b, j, off: (0, 0)),
        ],
        out_specs=pl.BlockSpec((1, tt, d), lambda b, j, off: (b, j, 0)),
        scratch_shapes=[pltpu.VMEM((2, n_slots * rt, d), BF16), pltpu.VMEM((n_slots * rt, tt), BF16),
                        pltpu.SemaphoreType.DMA((2,)), pltpu.VMEM((tt, d), F32)],
    )
    return pl.pallas_call(
        functools.partial(_combine_kernel, rt=rt, tt=tt, nw=nw, ctx_row=ctx_row, n_tiles=t // tt),
        grid_spec=grid_spec,
        out_shape=jax.ShapeDtypeStruct((bsz, t, d), F32),
        compiler_params=_params("arbitrary", "arbitrary"),
        name=name,
    )(off_flat, posm.reshape(bsz, e, 1, t), aff_t, x1, y, mod_l, ln_g, ln_b)


def _moe(x1, h2, aff_t, mod_l, w_gate, w_up, w_down, layer, ln_g, ln_b, ctx_row, tag):
    bsz, t, d = x1.shape
    cap = EC_FACTOR * t // N_EXPERTS
    tt = min(t, 256)
    grt = min(cap, GATHER_ROWS)
    posm, off, span = _route_call(aff_t, cap, tt, grt, "route_" + tag)
    y = _expert_call(span.reshape(-1), posm, h2, w_gate, w_up, w_down, layer, cap, grt, tt, "expert_" + tag)
    return _combine_call(off.reshape(-1), posm, aff_t, x1, y, mod_l, ln_g, ln_b, min(cap, SCATTER_ROWS), tt, ctx_row,
                         "combine_" + tag)


def kernel(x, c, ctx, c_ctx, w_mod, b_mod, w_in, hgrn_lb_logits, hgrn_norm_g, conv_w, conv_b, lru_wa,
           lru_ba, lru_wx, lru_bx, lru_lambda, w_branch_a, w_branch_b, w_out, ln_g, ln_b, w_router,
           w_gate, w_up, w_down):
    depth = w_in.shape[0]
    bsz = x.shape[0]
    ctx_row = bsz
    assert bsz < 8 and depth == DEPTH

    lb_cum = jnp.cumsum(jax.nn.softmax(hgrn_lb_logits.astype(F32), axis=0), axis=0)
    lbs = lb_cum - lb_cum[0]
    log1m_lb = jnp.log1p(-lbs)

    c_all = jnp.zeros((8, x.shape[2]), F32).at[:bsz].set(c).at[ctx_row].set(c_ctx)
    mod = _mod_call(c_all, w_mod, b_mod)

    p_index = P_INDEX
    for l in range(depth):
        need_ctx = l < depth - 1
        mod_l = mod[l]
        w_in_bf = w_in[l].astype(BF16)
        p_lat, lf_lat, kk_lat = _proj_call(x, mod_l, w_in_bf, lbs[l], log1m_lb[l], None, "proj_lat")
        p_ctx, lf_ctx, kk_ctx = _proj_call(ctx, mod_l, w_in_bf, lbs[l], log1m_lb[l], ctx_row,
                                           "proj_ctx")

        oc_f, oc_b, ol_f, ol_b = _hgrn_call(p_lat, lf_lat, kk_lat, p_ctx, lf_ctx, kk_ctx)
        h_lat, h_ctx = _lru_call(p_lat, p_ctx, p_index["lx"], p_index["lx"], conv_w[l], conv_b[l],
                                 lru_wa[l], lru_wx[l], lru_ba[l], lru_bx[l], lru_lambda[l])

        wa, wb, wo = (w.astype(BF16) for w in (w_branch_a[l], w_branch_b[l], w_out[l]))
        x1, h2, aff_t = _merge_call(ol_f, ol_b, p_lat, p_index, h_lat, x, mod_l, hgrn_norm_g[l],
                                         wa, wb, wo, ln_g[l], ln_b[l], w_router[l], None, "merge_lat")
        if need_ctx:
            c1, ch2, caff_t = _merge_call(oc_f, oc_b, p_ctx, p_index, h_ctx, ctx, mod_l, hgrn_norm_g[l],
                                                wa, wb, wo, ln_g[l], ln_b[l], w_router[l], ctx_row,
                                                "merge_ctx")
        x = _moe(x1, h2, aff_t, mod_l, w_gate, w_up, w_down, l, ln_g[l], ln_b[l], None, "lat")
        if need_ctx:
            ctx = _moe(c1, ch2, caff_t, mod_l, w_gate, w_up, w_down, l, ln_g[l], ln_b[l], ctx_row, "ctx")
    return x
```

```python
import functools

import jax
import jax.numpy as jnp
from jax import lax
from jax.experimental import pallas as pl
from jax.experimental.pallas import tpu as pltpu

F32 = jnp.float32
BF16 = jnp.bfloat16
I32 = jnp.int32
HIGHEST = lax.Precision.HIGHEST

DEPTH = 2
D_MODEL = 1024
GRID_W = 64
N_HEADS = 8
HEAD_W = 128
CHUNK = 64
N_EXPERTS = 16
EC_FACTOR = 2
LRU_C = 8.0
ALPHA = (2.0 * DEPTH) ** 0.25
LN_EPS = 1e-5
RMS_EPS = 1e-6
LN2 = 0.6931471805599453
LOG2E = 1.4426950408889634
SQRT_2_OVER_PI = 0.7978845608028654
EXP_CLAMP = 80.0

V7X_VMEM_LIMIT_BYTES = 56 * 1024 * 1024
OFF_STRIDE = 128
GATHER_WINDOW_TILES = 6
COMBINE_GROUP = 8
GATHER_ROWS = 128
SCATTER_ROWS = 64

G_Q, G_V, G_FF, G_FB, G_OG, G_LX, G_LY, G_MA, G_MB = range(9)


def _params(*sem):
    return pltpu.CompilerParams(dimension_semantics=sem, vmem_limit_bytes=V7X_VMEM_LIMIT_BYTES)


def _sigmoid(x):
    return 0.5 * jnp.tanh(0.5 * x) + 0.5


def _silu(x):
    h = 0.5 * x
    return h * jnp.tanh(h) + h


def _layer_norm(x, g, b):
    mu = jnp.mean(x, axis=-1, keepdims=True)
    xc = x - mu
    var = jnp.mean(xc * xc, axis=-1, keepdims=True)
    return xc * lax.rsqrt(var + LN_EPS) * g + b


def _mod_kernel(c_ref, w_ref, b_ref, o_ref):
    c = c_ref[...]
    s = _silu(c)
    o_ref[0] = jnp.dot(s, w_ref[0], preferred_element_type=F32, precision=HIGHEST) + b_ref[0]


def _mod_call(c_all, w_mod, b_mod):
    depth, d, n = w_mod.shape
    tn = 1536
    return pl.pallas_call(
        _mod_kernel,
        grid=(depth, n // tn),
        in_specs=[
            pl.BlockSpec((8, d), lambda l, j: (0, 0)),
            pl.BlockSpec((1, d, tn), lambda l, j: (l, 0, j)),
            pl.BlockSpec((1, 1, tn), lambda l, j: (l, 0, j)),
        ],
        out_specs=pl.BlockSpec((1, 8, tn), lambda l, j: (l, 0, j)),
        out_shape=jax.ShapeDtypeStruct((depth, 8, n), F32),
        compiler_params=_params("arbitrary", "arbitrary"),
        name="mod",
    )(c_all, w_mod, b_mod.reshape(depth, 1, n))


def _apply_act(act, p):
    if act == "id":
        return p
    if act == "silu_scale":
        return _silu(p) * (HEAD_W ** -0.5)
    if act == "silu":
        return _silu(p)
    if act == "sigmoid":
        return _sigmoid(p)
    if act == "gelu":
        ph = 0.5 * p
        return ph * jnp.tanh(p * ((0.044715 * SQRT_2_OVER_PI) * (p * p) + SQRT_2_OVER_PI)) + ph
    raise ValueError(act)


P_GROUPS = (G_Q, G_V, G_OG, G_LX, G_LY, G_MA, G_MB)
P_ACTS = ("silu_scale", "id", "silu", "id", "gelu", "sigmoid", "sigmoid")
P_INDEX = {"q": 0, "v": 1, "og": 2, "lx": 3, "ly": 4, "ma": 5, "mb": 6}


def _hgrn_gates(fp, lb, l1):
    kh = 0.5 * (1.0 - lb)
    p = kh * jnp.tanh(0.5 * fp)
    log_f = jnp.maximum(jnp.log((lb + kh) + p), jnp.minimum(fp, 0.0) + (l1 - LN2))
    return log_f, kh - p


def _proj_kernel(x_ref, mod_ref, w_ref, lb_ref, l1_ref, p_ref, lf_ref, kk_ref, *, ctx_row):
    d = D_MODEL
    b = pl.program_id(0)
    row = b if ctx_row is None else ctx_row
    sh = mod_ref[pl.ds(row, 1), 0:d]
    sc = mod_ref[pl.ds(row, 1), d:2 * d]
    h = (x_ref[0] * (1.0 + sc) + sh).astype(BF16)

    def group(col):
        return jnp.dot(h, w_ref[:, col * d:(col + 1) * d], preferred_element_type=F32)

    def plain(name):
        g = P_INDEX[name]
        p_ref[g, 0] = _apply_act(P_ACTS[g], group(P_GROUPS[g])).astype(BF16)

    def gate(dirn, col):
        log_f, k = _hgrn_gates(group(col), lb_ref[dirn:dirn + 1, :], l1_ref[dirn:dirn + 1, :])
        lf_ref[dirn, 0] = log_f
        kk_ref[dirn, 0] = k.astype(BF16)

    gate(0, G_FF)
    plain("v")
    gate(1, G_FB)
    plain("lx")
    plain("q")
    plain("ly")
    plain("og")
    plain("ma")
    plain("mb")


def _proj_call(x, mod_l, w_in_bf, lb, l1, ctx_row, name):
    bsz, t, d = x.shape
    tm = min(t, 512)
    ng = len(P_GROUPS)
    full = lambda shape: pl.BlockSpec(shape, lambda b, i: tuple(0 for _ in shape))
    out = lambda n: pl.BlockSpec((n, 1, tm, d), lambda b, i: (0, b, i, 0))
    return pl.pallas_call(
        functools.partial(_proj_kernel, ctx_row=ctx_row),
        grid=(bsz, t // tm),
        in_specs=[
            pl.BlockSpec((1, tm, d), lambda b, i: (b, i, 0)),
            full((8, 6 * d)),
            pl.BlockSpec(w_in_bf.shape, lambda b, i: (0, 0), pipeline_mode=pl.Buffered(1)),
            full((2, d)), full((2, d)),
        ],
        out_specs=[out(ng), out(2), out(2)],
        out_shape=[jax.ShapeDtypeStruct((ng, bsz, t, d), BF16), jax.ShapeDtypeStruct((2, bsz, t, d), F32),
                   jax.ShapeDtypeStruct((2, bsz, t, d), BF16)],
        compiler_params=_params("arbitrary", "arbitrary"),
        name=name,
    )(x, mod_l, w_in_bf, lb, l1)


def _hgrn_prepare(q_ref, v_ref, lf_ref, k_ref, row0, n, reverse):
    w = HEAD_W
    rows = pl.ds(row0, n * CHUNK)
    log_f = lf_ref[0, 0, rows, :]
    k = k_ref[0, 0, rows, :].astype(F32)
    q = q_ref[0, 0, rows, :]
    v = v_ref[0, 0, rows, :]

    ri = lax.broadcasted_iota(I32, (CHUNK, CHUNK), 0)
    ci = lax.broadcasted_iota(I32, (CHUNK, CHUNK), 1)
    causal = (ci >= ri) if reverse else (ci <= ri)
    tri = causal.astype(BF16)
    lf = jnp.concatenate([log_f[c * CHUNK:(c + 1) * CHUNK, :] for c in range(n)], axis=1)
    hi = lf.astype(BF16)
    rest = lf - hi.astype(F32)
    mid = rest.astype(BF16)
    lo = (rest - mid.astype(F32)).astype(BF16)
    bc = (jnp.dot(tri, hi, preferred_element_type=F32) + jnp.dot(tri, mid, preferred_element_type=F32)
          + jnp.dot(tri, lo, preferred_element_type=F32))

    half = CHUNK // 2
    chunks = []
    for c in range(n):
        sl = slice(c * CHUNK, (c + 1) * CHUNK)
        bcum = bc[:, c * w:(c + 1) * w]
        b_ref = bcum[half:half + 1, :]
        b_end = bcum[0:1, :] if reverse else bcum[CHUNK - 1:CHUNK, :]
        e1 = jnp.exp(jnp.minimum(bcum - b_ref, EXP_CLAMP))
        e2 = jnp.exp(jnp.minimum(b_ref - bcum, EXP_CLAMP))
        qt = q[sl].astype(F32) * e1
        kt = k[sl] * e2
        att = lax.dot_general(qt.astype(BF16), kt.astype(BF16), (((1,), (1,)), ((), ())),
                              preferred_element_type=F32)
        att = jnp.where(causal, att, 0.0).astype(BF16)
        o_intra = jnp.dot(att, v[sl], preferred_element_type=F32)
        qs = (qt * jnp.exp(b_ref)).astype(BF16)
        ke = (kt * jnp.exp(b_end - b_ref)).astype(BF16)
        upd = lax.dot_general(v[sl], ke, (((0,), (0,)), ((), ())), preferred_element_type=F32)
        chunks.append((o_intra, qs, upd, jnp.exp(b_end)))
    return chunks


def _hgrn_state_step(chunk, st, o_ref, row):
    o_intra, qs, upd, decay = chunk
    o_ref[0, 0, pl.ds(row, CHUNK), :] = o_intra + lax.dot_general(
        qs, st.astype(BF16), (((1,), (1,)), ((), ())), preferred_element_type=F32)
    return decay * st + upd


HGRN_SUB = 32


def _hgrn_kernel(qc_ref, vc_ref, lcf_ref, lcb_ref, kcf_ref, kcb_ref,
                 qf_ref, vf_ref, lff_ref, kf_ref, qb_ref, vb_ref, lfb_ref, kb_ref,
                 ocf_ref, ocb_ref, of_ref, ob_ref, sf_ref, sb_ref, *, n_ctx_chunks, n_lat_chunks):
    s = pl.program_id(2)

    def run(n_chunks, fwd, o_f, bwd, o_b):
        sub = min(n_chunks, HGRN_SUB)
        n_sub = n_chunks // sub

        def body(i, carry):
            rf = pl.multiple_of(i * (sub * CHUNK), sub * CHUNK)
            rb = pl.multiple_of((n_sub - 1 - i) * (sub * CHUNK), sub * CHUNK)
            cf = _hgrn_prepare(*fwd, rf, sub, False)
            cb = _hgrn_prepare(*bwd, rb, sub, True)
            st_f, st_b = sf_ref[...], sb_ref[...]
            for c in range(sub):
                st_f = _hgrn_state_step(cf[c], st_f, o_f, rf + c * CHUNK)
                st_b = _hgrn_state_step(cb[sub - 1 - c], st_b, o_b, rb + (sub - 1 - c) * CHUNK)
            sf_ref[...] = st_f
            sb_ref[...] = st_b
            return carry
        lax.fori_loop(0, n_sub, body, 0)

    @pl.when(s == 0)
    def _():
        sf_ref[...] = jnp.zeros_like(sf_ref)
        sb_ref[...] = jnp.zeros_like(sb_ref)
        run(n_ctx_chunks, (qc_ref, vc_ref, lcf_ref, kcf_ref), ocf_ref, (qc_ref, vc_ref, lcb_ref, kcb_ref), ocb_ref)

    run(n_lat_chunks, (qf_ref, vf_ref, lff_ref, kf_ref), of_ref, (qb_ref, vb_ref, lfb_ref, kb_ref), ob_ref)


def _hgrn_call(p_lat, lf_lat, kk_lat, p_ctx, lf_ctx, kk_ctx):
    _, bsz, t, d = p_lat.shape
    tc = p_ctx.shape[2]
    w = HEAD_W
    blk = min(t, 2048)
    ns = t // blk
    lat = lambda g, rev: pl.BlockSpec(
        (1, 1, blk, w), (lambda b, h, s: (g, b, ns - 1 - s, h)) if rev else (lambda b, h, s: (g, b, s, h)))
    ctx = lambda g: pl.BlockSpec((1, 1, tc, w), lambda b, h, s: (g, b, 0, h))
    return pl.pallas_call(
        functools.partial(_hgrn_kernel, n_ctx_chunks=tc // CHUNK, n_lat_chunks=blk // CHUNK),
        grid=(bsz, N_HEADS, ns),
        in_specs=[ctx(0), ctx(1), ctx(0), ctx(1), ctx(0), ctx(1),
                  lat(0, False), lat(1, False), lat(0, False), lat(0, False),
                  lat(0, True), lat(1, True), lat(1, True), lat(1, True)],
        out_specs=[ctx(0), ctx(0), lat(0, False), lat(0, True)],
        out_shape=[jax.ShapeDtypeStruct((1, bsz, tc, d), F32), jax.ShapeDtypeStruct((1, bsz, tc, d), F32),
                   jax.ShapeDtypeStruct((1, bsz, t, d), F32), jax.ShapeDtypeStruct((1, bsz, t, d), F32)],
        scratch_shapes=[pltpu.VMEM((w, w), F32), pltpu.VMEM((w, w), F32)],
        compiler_params=_params("arbitrary", "arbitrary", "arbitrary"),
        name="hgrn",
    )(p_ctx, p_ctx, lf_ctx, lf_ctx, kk_ctx, kk_ctx,
      p_lat, p_lat, lf_lat, kk_lat, p_lat, p_lat, lf_lat, kk_lat)


def _lru_gates(xc, w_all, hb_all, c2_ref):
    w = HEAD_W
    th = jnp.tanh(jnp.dot(xc.astype(BF16), w_all, preferred_element_type=F32) + hb_all)
    out = []
    for dirn in range(2):
        t_r = th[:, (2 * dirn) * w:(2 * dirn + 1) * w]
        t_i = th[:, (2 * dirn + 1) * w:(2 * dirn + 2) * w]
        c2 = c2_ref[dirn:dirn + 1, :]
        a = jnp.exp2(c2 * t_r + c2)
        y = 1.0 - a * a
        u = jnp.where(y > 0.0, y * lax.rsqrt(y), 0.0) * ((0.5 * t_i + 0.5) * xc)
        out += [a, u]
    return out


def _lru_kernel(xl_ref, xc_ref, cw_ref, cb_ref, wa_ref, wx_ref, ba_ref, bx_ref, lam_ref,
                hl_ref, hc_ref,
                xpad, cpad, a_f, u_f, a_b, u_b, ca_f, cu_f, ca_b, cu_b,
                ends_f, prods_f, carry_f, ends_b, prods_b, carry_b, c2_ref,
                *, t, tc):
    w = HEAD_W
    gw = GRID_W
    rows = t // gw
    lam = lam_ref[...]
    neg = -lam
    softplus = jnp.maximum(neg, 0.0) + jnp.log(1.0 + jnp.exp(-jnp.abs(neg)))
    c2_ref[...] = (-0.5 * LRU_C * LOG2E) * softplus
    w_all = (0.5 * jnp.concatenate([wa_ref[0, 0], wx_ref[0, 0], wa_ref[1, 0], wx_ref[1, 0]], axis=1)).astype(BF16)
    hb_all = 0.5 * jnp.concatenate([ba_ref[0:1, :], bx_ref[0:1, :], ba_ref[1:2, :], bx_ref[1:2, :]], axis=1)
    w0, w1, w2, w3 = (cw_ref[k:k + 1, :] for k in range(4))
    cb = cb_ref[...]

    cpad[...] = jnp.zeros_like(cpad)
    cpad[8:8 + tc, :] = xc_ref[0, 0].astype(F32)
    xcc = (w0 * cpad[6:6 + tc, :] + w1 * cpad[7:7 + tc, :] + w2 * cpad[8:8 + tc, :]
           + w3 * cpad[9:9 + tc, :] + cb)
    ca_f[...], cu_f[...], ca_b[...], cu_b[...] = _lru_gates(xcc, w_all, hb_all, c2_ref)

    def ctx_step(i, hs):
        hf, hb = hs
        p = tc - 1 - i
        hf = ca_f[pl.ds(i, 1), :] * hf + cu_f[pl.ds(i, 1), :]
        hb = ca_b[pl.ds(p, 1), :] * hb + cu_b[pl.ds(p, 1), :]
        cu_f[pl.ds(i, 1), :] = hf
        cu_b[pl.ds(p, 1), :] = hb
        return hf, hb
    zero_row = jnp.zeros((1, w), F32)
    s0_f, s0_b = lax.fori_loop(0, tc, ctx_step, (zero_row, zero_row), unroll=8)
    hc_ref[0] = (cu_f[...] + cu_b[...]).astype(hc_ref.dtype)

    xpad[2 * gw:2 * gw + t, :] = xl_ref[0, 0].astype(F32)
    col = lax.broadcasted_iota(I32, (gw, w), 0)
    body0 = 2 * gw
    xpad[gw:2 * gw, :] = jnp.where(col == 0, 0.0, xpad[pl.ds(body0 + (rows - 1) * gw - 1, gw), :])
    xpad[0:gw, :] = jnp.where(col == 0, 0.0, xpad[pl.ds(body0 + (rows - 2) * gw - 1, gw), :])
    xpad[body0 + t:body0 + t + gw, :] = jnp.where(col == gw - 1, 0.0, xpad[pl.ds(body0 + 1, gw), :])

    gate_rows = min(t, 512)

    def gate_body(c, carry_):
        base = pl.multiple_of(c * gate_rows, gate_rows)
        xcv = (w0 * xpad[pl.ds(base, gate_rows), :] + w1 * xpad[pl.ds(base + gw, gate_rows), :]
               + w2 * xpad[pl.ds(base + 2 * gw, gate_rows), :]
               + w3 * xpad[pl.ds(base + 3 * gw, gate_rows), :] + cb)
        sl = pl.ds(base, gate_rows)
        a_f[sl, :], u_f[sl, :], a_b[sl, :], u_b[sl, :] = _lru_gates(xcv, w_all, hb_all, c2_ref)
        return carry_
    lax.fori_loop(0, t // gate_rows, gate_body, 0)

    def slab(r):
        return pl.ds(pl.multiple_of(r * gw, gw), gw)

    def p1(i, c):
        hf, pf, hb, pb = c
        sf, sb = slab(i), slab(rows - 1 - i)
        af, ab = a_f[sf, :], a_b[sb, :]
        return af * hf + u_f[sf, :], af * pf, ab * hb + u_b[sb, :], ab * pb
    zeros, ones = jnp.zeros((gw, w), F32), jnp.ones((gw, w), F32)
    ends_f[...], prods_f[...], ends_b[...], prods_b[...] = lax.fori_loop(
        0, rows, p1, (zeros, ones, zeros, ones), unroll=4)

    def chain(i, c):
        cf, cb = c
        jf, jb = pl.ds(i, 1), pl.ds(gw - 1 - i, 1)
        carry_f[jf, :] = cf
        carry_b[jb, :] = cb
        return prods_f[jf, :] * cf + ends_f[jf, :], prods_b[jb, :] * cb + ends_b[jb, :]
    lax.fori_loop(0, gw, chain, (s0_f, s0_b), unroll=8)

    def p2(i, c):
        hf, hb = c
        sf, sb = slab(i), slab(rows - 1 - i)
        hf = a_f[sf, :] * hf + u_f[sf, :]
        hb = a_b[sb, :] * hb + u_b[sb, :]
        u_f[sf, :] = hf
        u_b[sb, :] = hb
        return hf, hb
    lax.fori_loop(0, rows, p2, (carry_f[...], carry_b[...]), unroll=4)
    hl_ref[0] = (u_f[...] + u_b[...]).astype(hl_ref.dtype)


def _lru_call(p_lat, p_ctx, g_lat, g_ctx, conv_w, conv_b, wa, wx, ba, bx, lam):
    _, bsz, t, d = p_lat.shape
    tc = p_ctx.shape[2]
    w = HEAD_W
    nb = d // w
    gw = GRID_W
    vec = lambda n: pl.BlockSpec((n, w), lambda b, k: (0, k))
    mat = pl.BlockSpec((2, 1, w, w), lambda b, k: (0, k, 0, 0))
    return pl.pallas_call(
        functools.partial(_lru_kernel, t=t, tc=tc),
        grid=(bsz, nb),
        in_specs=[
            pl.BlockSpec((1, 1, t, w), lambda b, k: (g_lat, b, 0, k)),
            pl.BlockSpec((1, 1, tc, w), lambda b, k: (g_ctx, b, 0, k)),
            vec(4), vec(1), mat, mat, vec(2), vec(2), vec(2),
        ],
        out_specs=[pl.BlockSpec((1, t, w), lambda b, k: (b, 0, k)),
                   pl.BlockSpec((1, tc, w), lambda b, k: (b, 0, k))],
        out_shape=[jax.ShapeDtypeStruct((bsz, t, d), BF16), jax.ShapeDtypeStruct((bsz, tc, d), BF16)],
        scratch_shapes=[
            pltpu.VMEM((t + 3 * gw, w), F32), pltpu.VMEM((tc + 16, w), F32),
            pltpu.VMEM((t, w), F32), pltpu.VMEM((t, w), F32), pltpu.VMEM((t, w), F32), pltpu.VMEM((t, w), F32),
            pltpu.VMEM((tc, w), F32), pltpu.VMEM((tc, w), F32), pltpu.VMEM((tc, w), F32), pltpu.VMEM((tc, w), F32),
            pltpu.VMEM((gw, w), F32), pltpu.VMEM((gw, w), F32), pltpu.VMEM((gw, w), F32),
            pltpu.VMEM((gw, w), F32), pltpu.VMEM((gw, w), F32), pltpu.VMEM((gw, w), F32),
            pltpu.VMEM((2, w), F32),
        ],
        compiler_params=_params("arbitrary", "arbitrary"),
        name="rglru",
    )(p_lat, p_ctx, conv_w, conv_b.reshape(1, d), wa, wx, ba, bx, lam)


def _merge_kernel(of_ref, ob_ref, og_ref, hl_ref, ly_ref, ma_ref, mb_ref, x_ref, mod_ref, gn_ref,
                  wa_ref, wb_ref, wo_ref, lng_ref, lnb_ref, wrt_ref,
                  x1_ref, h2_ref, afft_ref, *, ctx_row):
    d = D_MODEL
    b = pl.program_id(0)
    row = b if ctx_row is None else ctx_row
    o = of_ref[0, 0] + ob_ref[0, 0]
    gn = gn_ref[...]
    parts = []
    for h in range(N_HEADS):
        oh = o[:, h * HEAD_W:(h + 1) * HEAD_W]
        ms = jnp.mean(oh * oh, axis=-1, keepdims=True)
        parts.append(oh * lax.rsqrt(ms + RMS_EPS) * gn)
    o_a = (jnp.concatenate(parts, axis=1) * og_ref[0, 0].astype(F32)).astype(BF16)
    y_a = jnp.dot(o_a, wa_ref[...], preferred_element_type=F32)
    y_b = jnp.dot((hl_ref[0].astype(F32) * ly_ref[0, 0].astype(F32)).astype(BF16), wb_ref[...],
                  preferred_element_type=F32)
    z = ma_ref[0, 0].astype(F32) * y_a + mb_ref[0, 0].astype(F32) * y_b
    y = jnp.dot(z.astype(BF16), wo_ref[...], preferred_element_type=F32)
    g1 = mod_ref[pl.ds(row, 1), 2 * d:3 * d]
    x1 = _layer_norm(ALPHA * x_ref[0] + g1 * y, lng_ref[0:1, :], lnb_ref[0:1, :])
    x1_ref[0] = x1
    sh2 = mod_ref[pl.ds(row, 1), 3 * d:4 * d]
    sc2 = mod_ref[pl.ds(row, 1), 4 * d:5 * d]
    h2 = x1 * (1.0 + sc2) + sh2
    h2_ref[0] = h2.astype(BF16)
    logits_t = lax.dot_general(wrt_ref[...], h2, (((1,), (1,)), ((), ())),
                               preferred_element_type=F32, precision=HIGHEST)
    pt = jnp.exp(logits_t - jnp.max(logits_t, axis=0, keepdims=True))
    afft_ref[0] = pt / jnp.sum(pt, axis=0, keepdims=True)


def _merge_call(o_f, o_b, p, groups, h_lru, x, mod_l, gn, wa, wb, wo, ln_g, ln_b, w_router, ctx_row, name):
    bsz, t, d = x.shape
    tm = min(t, 512)
    e = N_EXPERTS
    tile = lambda g: pl.BlockSpec((1, 1, tm, d), lambda b, i: (g, b, i, 0))
    tok = pl.BlockSpec((1, tm, d), lambda b, i: (b, i, 0))
    full = lambda shape: pl.BlockSpec(shape, lambda b, i: tuple(0 for _ in shape))
    return pl.pallas_call(
        functools.partial(_merge_kernel, ctx_row=ctx_row),
        grid=(bsz, t // tm),
        in_specs=[tile(0), tile(0), tile(groups["og"]), tok, tile(groups["ly"]), tile(groups["ma"]),
                  tile(groups["mb"]), tok, full((8, 6 * d)), full((1, HEAD_W)),
                  full((d, d)), full((d, d)), full((d, d)), full((2, d)), full((2, d)),
                  full((e, d))],
        out_specs=[tok, tok, pl.BlockSpec((1, e, tm), lambda b, i: (b, 0, i))],
        out_shape=[jax.ShapeDtypeStruct((bsz, t, d), F32), jax.ShapeDtypeStruct((bsz, t, d), BF16),
                   jax.ShapeDtypeStruct((bsz, e, t), F32)],
        compiler_params=_params("arbitrary", "arbitrary"),
        name=name,
    )(o_f, o_b, p, h_lru, p, p, p, x, mod_l, gn.reshape(1, HEAD_W), wa, wb, wo, ln_g, ln_b,
      w_router.T)


def _route_kernel(a_ref, posm_ref, off_ref, span_ref, *, cap, t, tt, grt):
    e = N_EXPERTS
    u = lax.bitcast_convert_type(a_ref[0], I32)
    thr = jnp.zeros((e, 1), I32)
    for bit in range(30, -1, -1):
        cand = thr | (1 << bit)
        cnt = jnp.sum((u >= cand).astype(F32), axis=1, keepdims=True)
        thr = jnp.where(cnt >= cap, cand, thr)
    gt = u > thr
    eq = u == thr
    need = cap - jnp.sum(gt.astype(F32), axis=1, keepdims=True)
    ri = lax.broadcasted_iota(I32, (tt, tt), 0)
    ci = lax.broadcasted_iota(I32, (tt, tt), 1)
    before = (ri < ci).astype(BF16)
    lane = lax.broadcasted_iota(I32, (e, OFF_STRIDE), 1)
    offs = jnp.zeros((e, OFF_STRIDE), F32)
    offs_end = jnp.zeros((e, OFF_STRIDE), F32)
    n_eq = jnp.zeros((e, 1), F32)
    n_sel = jnp.zeros((e, 1), F32)
    for j in range(t // tt):
        sl = slice(j * tt, (j + 1) * tt)
        eq_j = eq[:, sl]
        rank_eq = jnp.dot(eq_j.astype(BF16), before, preferred_element_type=F32) + n_eq
        sel_j = gt[:, sl] | (eq_j & (rank_eq < need))
        sel_f = sel_j.astype(F32)
        pos = jnp.dot(sel_j.astype(BF16), before, preferred_element_type=F32) + n_sel
        posm_ref[0, :, sl] = jnp.where(sel_j, pos.astype(I32), -1)
        offs = jnp.where(lane == j, n_sel, offs)
        n_eq = n_eq + jnp.sum(eq_j.astype(F32), axis=1, keepdims=True)
        n_sel = n_sel + jnp.sum(sel_f, axis=1, keepdims=True)
        offs_end = jnp.where(lane == j, n_sel, offs_end)
    nt = t // tt
    offs = jnp.where(lane == nt, n_sel, offs)
    off_ref[0] = offs.astype(I32)
    n_rt = cap // grt
    tile = lane < nt
    spans = jnp.zeros((e, OFF_STRIDE), F32)
    for i in range(n_rt):
        skipped = jnp.sum(jnp.where(tile & (offs_end <= i * grt), 1.0, 0.0), axis=1, keepdims=True)
        started = jnp.sum(jnp.where(tile & (offs < (i + 1) * grt), 1.0, 0.0), axis=1, keepdims=True)
        spans = jnp.where(lane == i, skipped, jnp.where(lane == n_rt + i, started, spans))
    span_ref[0] = spans.astype(I32)


def _route_call(aff_t, cap, tt, grt, name):
    bsz, e, t = aff_t.shape
    small = pl.BlockSpec((1, e, OFF_STRIDE), lambda b: (b, 0, 0))
    return pl.pallas_call(
        functools.partial(_route_kernel, cap=cap, t=t, tt=tt, grt=grt),
        grid=(bsz,),
        in_specs=[pl.BlockSpec((1, e, t), lambda b: (b, 0, 0))],
        out_specs=[pl.BlockSpec((1, e, t), lambda b: (b, 0, 0)), small, small],
        out_shape=[jax.ShapeDtypeStruct((bsz, e, t), I32), jax.ShapeDtypeStruct((bsz, e, OFF_STRIDE), I32),
                   jax.ShapeDtypeStruct((bsz, e, OFF_STRIDE), I32)],
        compiler_params=_params("arbitrary"),
        name=name,
    )(aff_t)


def _expert_kernel(span_ref, posm_ref, h_ref, wg_ref, wu_ref, wd_ref, y_ref, x_ref, *, cap, rt, tt, nt, nb):
    if nb == 1:
        b0, e = pl.program_id(0), pl.program_id(1)
    else:
        b0, e = 0, pl.program_id(0)
    x_ref[...] = jnp.zeros_like(x_ref)
    kt = min(nt, GATHER_WINDOW_TILES)
    kw = kt * tt
    rows = lax.broadcasted_iota(I32, (rt, kw), 0)
    lane = lax.broadcasted_iota(I32, (rt, kw), 1)
    n_rt = cap // rt

    for bb in range(nb):
        base = ((b0 + bb) * N_EXPERTS + e) * OFF_STRIDE
        for i in range(n_rt):
            first = span_ref[base + i]
            n_win = (span_ref[base + n_rt + i] - 1 - first) // kt + 1
            out_rows = slice((bb * n_rt + i) * rt, (bb * n_rt + i + 1) * rt)

            def body(wdx, carry, bb=bb, i=i, first=first, out_rows=out_rows):
                want = first + wdx * kt
                start = jnp.minimum(want, nt - kt)
                tok = pl.ds(pl.multiple_of(start * tt, tt), kw)
                hit = ((posm_ref[bb, 0, :, tok] - i * rt) == rows) & (lane >= (want - start) * tt)
                x_ref[out_rows, :] += jnp.dot(hit.astype(BF16), h_ref[bb, tok, :],
                                              preferred_element_type=F32).astype(BF16)
                return carry
            lax.fori_loop(0, n_win, body, 0)
    wg, wu, wd = wg_ref[0, 0].astype(BF16), wu_ref[0, 0].astype(BF16), wd_ref[0, 0].astype(BF16)
    total = nb * cap
    ft = min(total, 256)
    for r in range(total // ft):
        xb = x_ref[r * ft:(r + 1) * ft, :]
        g = jnp.dot(xb, wg, preferred_element_type=F32)
        u = jnp.dot(xb, wu, preferred_element_type=F32)
        hid = (_silu(g) * u).astype(BF16)
        out = jnp.dot(hid, wd, preferred_element_type=F32).astype(BF16)
        if ft <= cap:
            row0 = (r * ft) % cap
            y_ref[(r * ft) // cap, 0, row0:row0 + ft, :] = out
        else:
            for q in range(ft // cap):
                y_ref[r * (ft // cap) + q, 0, :, :] = out[q * cap:(q + 1) * cap, :]


def _expert_call(span_flat, posm, h2, w_gate, w_up, w_down, layer, cap, rt, tt, name):
    bsz, t, d = h2.shape
    _, e, _, f = w_gate.shape
    nt = t // tt
    if t < d:
        nb, grid = bsz, (e,)
        bk = lambda k, off: (0, k)
        tok_spec = pl.BlockSpec((bsz, t, d), lambda k, off: (0, 0, 0))
    else:
        nb, grid = 1, (bsz, e)
        bk = lambda b, k, off: (b, k)
        tok_spec = pl.BlockSpec((1, t, d), lambda b, k, off: (b, 0, 0), pipeline_mode=pl.Buffered(1))
    weight = lambda r, c: pl.BlockSpec((1, 1, r, c), lambda *i: (layer, bk(*i)[1], 0, 0))
    grid_spec = pltpu.PrefetchScalarGridSpec(
        num_scalar_prefetch=1,
        grid=grid,
        in_specs=[
            pl.BlockSpec((nb, 1, 1, t), lambda *i: (*bk(*i), 0, 0)),
            tok_spec, weight(d, f), weight(d, f), weight(f, d),
        ],
        out_specs=pl.BlockSpec((nb, 1, cap, d), lambda *i: (*bk(*i), 0, 0)),
        scratch_shapes=[pltpu.VMEM((nb * cap, d), BF16)],
    )
    return pl.pallas_call(
        functools.partial(_expert_kernel, cap=cap, rt=rt, tt=tt, nt=nt, nb=nb),
        grid_spec=grid_spec,
        out_shape=jax.ShapeDtypeStruct((bsz, e, cap, d), BF16),
        compiler_params=_params(*(("arbitrary",) * len(grid))),
        name=name,
    )(span_flat, posm.reshape(bsz, e, 1, t), h2, w_gate, w_up, w_down)


def _combine_kernel(off_ref, posm_ref, afft_ref, x1_ref, y_hbm, mod_ref, lng_ref, lnb_ref, o_ref,
                    ybuf, pbuf, sem, acc_ref, *, rt, tt, nw, ctx_row, n_tiles):
    d = D_MODEL
    b = pl.program_id(0)
    j = pl.program_id(1)
    row = b if ctx_row is None else ctx_row
    shift = rt.bit_length() - 1
    step = b * n_tiles + j
    cur = lax.rem(step, 2)
    is_last = step == pl.num_programs(0) * n_tiles - 1

    def plan(bb, jj):
        out, n_used = [], jnp.int32(0)
        for e in range(N_EXPERTS):
            base = (bb * N_EXPERTS + e) * OFF_STRIDE
            lo = off_ref[base + jj]
            hi = off_ref[base + jj + 1]
            t0 = lax.shift_right_logical(lo, shift)
            n = jnp.where(hi > lo, lax.shift_right_logical(hi - 1, shift) - t0 + 1, 0)
            out.append((t0, n, n_used))
            n_used = n_used + n
        return out, n_used

    def for_used_windows(windows, fn):
        for e, (t0, n, s0) in enumerate(windows):
            def body(wdx, carry, e=e, t0=t0, s0=s0):
                fn(e, t0 + wdx, s0 + wdx)
                return carry
            lax.fori_loop(0, n, body, 0)

    def window_copy(bb, e, window, slot, half):
        src = y_hbm.at[bb, e, pl.ds(pl.multiple_of(window * rt, rt), rt), :]
        dst = ybuf.at[half, pl.ds(pl.multiple_of(slot * rt, rt), rt), :]
        return pltpu.make_async_copy(src, dst, sem.at[half])

    def fetch(bb, windows, half):
        for_used_windows(windows, lambda e, window, slot: window_copy(bb, e, window, slot, half).start())

    windows, n_used = plan(b, j)
    n_groups = lax.shift_right_logical(n_used + (COMBINE_GROUP - 1), COMBINE_GROUP.bit_length() - 1)

    @pl.when(step == 0)
    def _():
        ybuf[...] = jnp.zeros_like(ybuf)
        fetch(b, windows, 0)

    @pl.when(jnp.logical_not(is_last))
    def _():
        wrap = j == n_tiles - 1
        b_next = jnp.where(wrap, b + 1, b)
        fetch(b_next, plan(b_next, jnp.where(wrap, 0, j + 1))[0], 1 - cur)

    def clear(s, carry):
        pbuf[pl.ds(pl.multiple_of(s * rt, rt), rt), :] = jnp.zeros((rt, tt), BF16)
        return carry
    lax.fori_loop(n_used, n_groups * COMBINE_GROUP, clear, 0)

    rows = lax.broadcasted_iota(I32, (rt, tt), 0)

    def weights(e, window, slot):
        pos = posm_ref[0, e]
        gate = afft_ref[0, e:e + 1, :]
        pbuf[pl.ds(pl.multiple_of(slot * rt, rt), rt), :] = jnp.where(
            (pos - window * rt) == rows, gate, 0.0).astype(BF16)
    for_used_windows(windows, weights)

    def drain(i, carry):
        window_copy(0, 0, 0, 0, cur).wait()
        return carry
    lax.fori_loop(0, n_used, drain, 0)

    acc_ref[...] = jnp.zeros_like(acc_ref)
    gk = COMBINE_GROUP * rt
    for g in range(pl.cdiv(N_EXPERTS * nw, COMBINE_GROUP)):
        @pl.when(g < n_groups)
        def _(g=g):
            acc_ref[...] += lax.dot_general(pbuf[g * gk:(g + 1) * gk, :], ybuf[cur, g * gk:(g + 1) * gk, :],
                                            (((0,), (0,)), ((), ())), preferred_element_type=F32)
    g2 = mod_ref[pl.ds(row, 1), 5 * d:6 * d]
    o_ref[0] = _layer_norm(ALPHA * x1_ref[0] + g2 * acc_ref[...], lng_ref[1:2, :], lnb_ref[1:2, :])


def _combine_call(off_flat, posm, aff_t, x1, y, mod_l, ln_g, ln_b, rt, tt, ctx_row, name):
    bsz, t, d = x1.shape
    e = N_EXPERTS
    cap = y.shape[2]
    nw = min(cap // rt, (tt - 1) // rt + 2)
    n_slots = pl.cdiv(e * nw, COMBINE_GROUP) * COMBINE_GROUP
    grid_spec = pltpu.PrefetchScalarGridSpec(
        num_scalar_prefetch=1,
        grid=(bsz, t // tt),
        in_specs=[
            pl.BlockSpec((1, e, 1, tt), lambda b, j, off: (b, 0, 0, j)),
            pl.BlockSpec((1, e, tt), lambda b, j, off: (b, 0, j)),
            pl.BlockSpec((1, tt, d), lambda b, j, off: (b, j, 0)),
            pl.BlockSpec(memory_space=pl.ANY),
            pl.BlockSpec((8, 6 * d), lambda b, j, off: (0, 0)),
            pl.BlockSpec((2, d), lambda b, j, off: (0, 0)),
            pl.BlockSpec((2, d), lambda b, j, off: (0, 0)),
        ],
        out_specs=pl.BlockSpec((1, tt, d), lambda b, j, off: (b, j, 0)),
        scratch_shapes=[pltpu.VMEM((2, n_slots * rt, d), BF16), pltpu.VMEM((n_slots * rt, tt), BF16),
                        pltpu.SemaphoreType.DMA((2,)), pltpu.VMEM((tt, d), F32)],
    )
    return pl.pallas_call(
        functools.partial(_combine_kernel, rt=rt, tt=tt, nw=nw, ctx_row=ctx_row, n_tiles=t // tt),
        grid_spec=grid_spec,
        out_shape=jax.ShapeDtypeStruct((bsz, t, d), F32),
        compiler_params=_params("arbitrary", "arbitrary"),
        name=name,
    )(off_flat, posm.reshape(bsz, e, 1, t), aff_t, x1, y, mod_l, ln_g, ln_b)


def _moe(x1, h2, aff_t, mod_l, w_gate, w_up, w_down, layer, ln_g, ln_b, ctx_row, tag):
    bsz, t, d = x1.shape
    cap = EC_FACTOR * t // N_EXPERTS
    tt = min(t, 256)
    grt = min(cap, GATHER_ROWS)
    posm, off, span = _route_call(aff_t, cap, tt, grt, "route_" + tag)
    y = _expert_call(span.reshape(-1), posm, h2, w_gate, w_up, w_down, layer, cap, grt, tt, "expert_" + tag)
    return _combine_call(off.reshape(-1), posm, aff_t, x1, y, mod_l, ln_g, ln_b, min(cap, SCATTER_ROWS), tt, ctx_row,
                         "combine_" + tag)


def kernel(x, c, ctx, c_ctx, w_mod, b_mod, w_in, hgrn_lb_logits, hgrn_norm_g, conv_w, conv_b, lru_wa,
           lru_ba, lru_wx, lru_bx, lru_lambda, w_branch_a, w_branch_b, w_out, ln_g, ln_b, w_router,
           w_gate, w_up, w_down):
    depth = w_in.shape[0]
    bsz = x.shape[0]
    ctx_row = bsz
    assert bsz < 8 and depth == DEPTH

    lb_cum = jnp.cumsum(jax.nn.softmax(hgrn_lb_logits.astype(F32), axis=0), axis=0)
    lbs = lb_cum - lb_cum[0]
    log1m_lb = jnp.log1p(-lbs)

    c_all = jnp.zeros((8, x.shape[2]), F32).at[:bsz].set(c).at[ctx_row].set(c_ctx)
    mod = _mod_call(c_all, w_mod, b_mod)

    p_index = P_INDEX
    for l in range(depth):
        need_ctx = l < depth - 1
        mod_l = mod[l]
        w_in_bf = w_in[l].astype(BF16)
        p_lat, lf_lat, kk_lat = _proj_call(x, mod_l, w_in_bf, lbs[l], log1m_lb[l], None, "proj_lat")
        p_ctx, lf_ctx, kk_ctx = _proj_call(ctx, mod_l, w_in_bf, lbs[l], log1m_lb[l], ctx_row,
                                           "proj_ctx")

        oc_f, oc_b, ol_f, ol_b = _hgrn_call(p_lat, lf_lat, kk_lat, p_ctx, lf_ctx, kk_ctx)
        h_lat, h_ctx = _lru_call(p_lat, p_ctx, p_index["lx"], p_index["lx"], conv_w[l], conv_b[l],
                                 lru_wa[l], lru_wx[l], lru_ba[l], lru_bx[l], lru_lambda[l])

        wa, wb, wo = (w.astype(BF16) for w in (w_branch_a[l], w_branch_b[l], w_out[l]))
        x1, h2, aff_t = _merge_call(ol_f, ol_b, p_lat, p_index, h_lat, x, mod_l, hgrn_norm_g[l],
                                         wa, wb, wo, ln_g[l], ln_b[l], w_router[l], None, "merge_lat")
        if need_ctx:
            c1, ch2, caff_t = _merge_call(oc_f, oc_b, p_ctx, p_index, h_ctx, ctx, mod_l, hgrn_norm_g[l],
                                                wa, wb, wo, ln_g[l], ln_b[l], w_router[l], ctx_row,
                                                "merge_ctx")
        x = _moe(x1, h2, aff_t, mod_l, w_gate, w_up, w_down, l, ln_g[l], ln_b[l], None, "lat")
        if need_ctx:
            ctx = _moe(c1, ch2, caff_t, mod_l, w_gate, w_up, w_down, l, ln_g[l], ln_b[l], ctx_row, "ctx")
    return x
```

```python
import functools

import jax
import jax.numpy as jnp
from jax import lax
from jax.experimental import pallas as pl
from jax.experimental.pallas import tpu as pltpu

F32 = jnp.float32
BF16 = jnp.bfloat16
I32 = jnp.int32
HIGHEST = lax.Precision.HIGHEST

DEPTH = 2
D_MODEL = 1024
GRID_W = 64
N_HEADS = 8
HEAD_W = 128
CHUNK = 64
N_EXPERTS = 16
EC_FACTOR = 2
LRU_C = 8.0
ALPHA = (2.0 * DEPTH) ** 0.25
LN_EPS = 1e-5
RMS_EPS = 1e-6
LN2 = 0.6931471805599453
LOG2E = 1.4426950408889634
SQRT_2_OVER_PI = 0.7978845608028654
EXP_CLAMP = 80.0

V7X_VMEM_LIMIT_BYTES = 56 * 1024 * 1024
OFF_STRIDE = 128
GATHER_WINDOW_TILES = 6
COMBINE_GROUP = 8
GATHER_ROWS = 128
SCATTER_ROWS = 64

G_Q, G_V, G_FF, G_FB, G_OG, G_LX, G_LY, G_MA, G_MB = range(9)


def _params(*sem):
    return pltpu.CompilerParams(dimension_semantics=sem, vmem_limit_bytes=V7X_VMEM_LIMIT_BYTES)


def _sigmoid(x):
    return 0.5 * jnp.tanh(0.5 * x) + 0.5


def _silu(x):
    h = 0.5 * x
    return h * jnp.tanh(h) + h


def _layer_norm(x, g, b):
    mu = jnp.mean(x, axis=-1, keepdims=True)
    xc = x - mu
    var = jnp.mean(xc * xc, axis=-1, keepdims=True)
    return xc * lax.rsqrt(var + LN_EPS) * g + b


def _mod_kernel(c_ref, w_ref, b_ref, o_ref):
    c = c_ref[...]
    s = _silu(c)
    o_ref[0] = jnp.dot(s, w_ref[0], preferred_element_type=F32, precision=HIGHEST) + b_ref[0]


def _mod_call(c_all, w_mod, b_mod):
    depth, d, n = w_mod.shape
    tn = 1536
    return pl.pallas_call(
        _mod_kernel,
        grid=(depth, n // tn),
        in_specs=[
            pl.BlockSpec((8, d), lambda l, j: (0, 0)),
            pl.BlockSpec((1, d, tn), lambda l, j: (l, 0, j)),
            pl.BlockSpec((1, 1, tn), lambda l, j: (l, 0, j)),
        ],
        out_specs=pl.BlockSpec((1, 8, tn), lambda l, j: (l, 0, j)),
        out_shape=jax.ShapeDtypeStruct((depth, 8, n), F32),
        compiler_params=_params("arbitrary", "arbitrary"),
        name="mod",
    )(c_all, w_mod, b_mod.reshape(depth, 1, n))


def _apply_act(act, p):
    if act == "id":
        return p
    if act == "silu_scale":
        return _silu(p) * (HEAD_W ** -0.5)
    if act == "silu":
        return _silu(p)
    if act == "sigmoid":
        return _sigmoid(p)
    if act == "gelu":
        ph = 0.5 * p
        return ph * jnp.tanh(p * ((0.044715 * SQRT_2_OVER_PI) * (p * p) + SQRT_2_OVER_PI)) + ph
    raise ValueError(act)


P_GROUPS = (G_Q, G_V, G_OG, G_LX, G_LY, G_MA, G_MB)
P_ACTS = ("silu_scale", "id", "silu", "id", "gelu", "sigmoid", "sigmoid")
P_INDEX = {"q": 0, "v": 1, "og": 2, "lx": 3, "ly": 4, "ma": 5, "mb": 6}


def _hgrn_gates(fp, lb, l1):
    kh = 0.5 * (1.0 - lb)
    p = kh * jnp.tanh(0.5 * fp)
    log_f = jnp.maximum(jnp.log((lb + kh) + p), jnp.minimum(fp, 0.0) + (l1 - LN2))
    return log_f, kh - p


def _proj_kernel(x_ref, mod_ref, w_ref, lb_ref, l1_ref, p_ref, lf_ref, kk_ref, *, ctx_row):
    d = D_MODEL
    b = pl.program_id(0)
    row = b if ctx_row is None else ctx_row
    sh = mod_ref[pl.ds(row, 1), 0:d]
    sc = mod_ref[pl.ds(row, 1), d:2 * d]
    h = (x_ref[0] * (1.0 + sc) + sh).astype(BF16)

    def group(col):
        return jnp.dot(h, w_ref[:, col * d:(col + 1) * d], preferred_element_type=F32)

    def plain(name):
        g = P_INDEX[name]
        p_ref[g, 0] = _apply_act(P_ACTS[g], group(P_GROUPS[g])).astype(BF16)

    def gate(dirn, col):
        log_f, k = _hgrn_gates(group(col), lb_ref[dirn:dirn + 1, :], l1_ref[dirn:dirn + 1, :])
        lf_ref[dirn, 0] = log_f
        kk_ref[dirn, 0] = k.astype(BF16)

    gate(0, G_FF)
    plain("v")
    gate(1, G_FB)
    plain("lx")
    plain("q")
    plain("ly")
    plain("og")
    plain("ma")
    plain("mb")


def _proj_call(x, mod_l, w_in_bf, lb, l1, ctx_row, name):
    bsz, t, d = x.shape
    tm = min(t, 512)
    ng = len(P_GROUPS)
    full = lambda shape: pl.BlockSpec(shape, lambda b, i: tuple(0 for _ in shape))
    out = lambda n: pl.BlockSpec((n, 1, tm, d), lambda b, i: (0, b, i, 0))
    return pl.pallas_call(
        functools.partial(_proj_kernel, ctx_row=ctx_row),
        grid=(bsz, t // tm),
        in_specs=[
            pl.BlockSpec((1, tm, d), lambda b, i: (b, i, 0)),
            full((8, 6 * d)),
            pl.BlockSpec(w_in_bf.shape, lambda b, i: (0, 0), pipeline_mode=pl.Buffered(1)),
            full((2, d)), full((2, d)),
        ],
        out_specs=[out(ng), out(2), out(2)],
        out_shape=[jax.ShapeDtypeStruct((ng, bsz, t, d), BF16), jax.ShapeDtypeStruct((2, bsz, t, d), F32),
                   jax.ShapeDtypeStruct((2, bsz, t, d), BF16)],
        compiler_params=_params("arbitrary", "arbitrary"),
        name=name,
    )(x, mod_l, w_in_bf, lb, l1)


def _hgrn_prepare(q_ref, v_ref, lf_ref, k_ref, row0, n, reverse):
    w = HEAD_W
    rows = pl.ds(row0, n * CHUNK)
    log_f = lf_ref[0, 0, rows, :]
    k = k_ref[0, 0, rows, :].astype(F32)
    q = q_ref[0, 0, rows, :]
    v = v_ref[0, 0, rows, :]

    ri = lax.broadcasted_iota(I32, (CHUNK, CHUNK), 0)
    ci = lax.broadcasted_iota(I32, (CHUNK, CHUNK), 1)
    causal = (ci >= ri) if reverse else (ci <= ri)
    tri = causal.astype(BF16)
    lf = jnp.concatenate([log_f[c * CHUNK:(c + 1) * CHUNK, :] for c in range(n)], axis=1)
    hi = lf.astype(BF16)
    rest = lf - hi.astype(F32)
    mid = rest.astype(BF16)
    lo = (rest - mid.astype(F32)).astype(BF16)
    bc = (jnp.dot(tri, hi, preferred_element_type=F32) + jnp.dot(tri, mid, preferred_element_type=F32)
          + jnp.dot(tri, lo, preferred_element_type=F32))

    half = CHUNK // 2
    chunks = []
    for c in range(n):
        sl = slice(c * CHUNK, (c + 1) * CHUNK)
        bcum = bc[:, c * w:(c + 1) * w]
        b_ref = bcum[half:half + 1, :]
        b_end = bcum[0:1, :] if reverse else bcum[CHUNK - 1:CHUNK, :]
        e1 = jnp.exp(jnp.minimum(bcum - b_ref, EXP_CLAMP))
        e2 = jnp.exp(jnp.minimum(b_ref - bcum, EXP_CLAMP))
        qt = q[sl].astype(F32) * e1
        kt = k[sl] * e2
        att = lax.dot_general(qt.astype(BF16), kt.astype(BF16), (((1,), (1,)), ((), ())),
                              preferred_element_type=F32)
        att = jnp.where(causal, att, 0.0).astype(BF16)
        o_intra = jnp.dot(att, v[sl], preferred_element_type=F32)
        qs = (qt * jnp.exp(b_ref)).astype(BF16)
        ke = (kt * jnp.exp(b_end - b_ref)).astype(BF16)
        upd = lax.dot_general(v[sl], ke, (((0,), (0,)), ((), ())), preferred_element_type=F32)
        chunks.append((o_intra, qs, upd, jnp.exp(b_end)))
    return chunks


def _hgrn_state_step(chunk, st, o_ref, row):
    o_intra, qs, upd, decay = chunk
    o_ref[0, 0, pl.ds(row, CHUNK), :] = o_intra + lax.dot_general(
        qs, st.astype(BF16), (((1,), (1,)), ((), ())), preferred_element_type=F32)
    return decay * st + upd


HGRN_SUB = 32


def _hgrn_kernel(qc_ref, vc_ref, lcf_ref, lcb_ref, kcf_ref, kcb_ref,
                 qf_ref, vf_ref, lff_ref, kf_ref, qb_ref, vb_ref, lfb_ref, kb_ref,
                 ocf_ref, ocb_ref, of_ref, ob_ref, sf_ref, sb_ref, *, n_ctx_chunks, n_lat_chunks):
    s = pl.program_id(2)

    def run(n_chunks, fwd, o_f, bwd, o_b):
        sub = min(n_chunks, HGRN_SUB)
        n_sub = n_chunks // sub

        def body(i, carry):
            rf = pl.multiple_of(i * (sub * CHUNK), sub * CHUNK)
            rb = pl.multiple_of((n_sub - 1 - i) * (sub * CHUNK), sub * CHUNK)
            cf = _hgrn_prepare(*fwd, rf, sub, False)
            cb = _hgrn_prepare(*bwd, rb, sub, True)
            st_f, st_b = sf_ref[...], sb_ref[...]
            for c in range(sub):
                st_f = _hgrn_state_step(cf[c], st_f, o_f, rf + c * CHUNK)
                st_b = _hgrn_state_step(cb[sub - 1 - c], st_b, o_b, rb + (sub - 1 - c) * CHUNK)
            sf_ref[...] = st_f
            sb_ref[...] = st_b
            return carry
        lax.fori_loop(0, n_sub, body, 0)

    @pl.when(s == 0)
    def _():
        sf_ref[...] = jnp.zeros_like(sf_ref)
        sb_ref[...] = jnp.zeros_like(sb_ref)
        run(n_ctx_chunks, (qc_ref, vc_ref, lcf_ref, kcf_ref), ocf_ref, (qc_ref, vc_ref, lcb_ref, kcb_ref), ocb_ref)

    run(n_lat_chunks, (qf_ref, vf_ref, lff_ref, kf_ref), of_ref, (qb_ref, vb_ref, lfb_ref, kb_ref), ob_ref)


def _hgrn_call(p_lat, lf_lat, kk_lat, p_ctx, lf_ctx, kk_ctx):
    _, bsz, t, d = p_lat.shape
    tc = p_ctx.shape[2]
    w = HEAD_W
    blk = min(t, 4096)
    ns = t // blk
    lat = lambda g, rev: pl.BlockSpec(
        (1, 1, blk, w), (lambda b, h, s: (g, b, ns - 1 - s, h)) if rev else (lambda b, h, s: (g, b, s, h)))
    ctx = lambda g: pl.BlockSpec((1, 1, tc, w), lambda b, h, s: (g, b, 0, h))
    return pl.pallas_call(
        functools.partial(_hgrn_kernel, n_ctx_chunks=tc // CHUNK, n_lat_chunks=blk // CHUNK),
        grid=(bsz, N_HEADS, ns),
        in_specs=[ctx(0), ctx(1), ctx(0), ctx(1), ctx(0), ctx(1),
                  lat(0, False), lat(1, False), lat(0, False), lat(0, False),
                  lat(0, True), lat(1, True), lat(1, True), lat(1, True)],
        out_specs=[ctx(0), ctx(0), lat(0, False), lat(0, True)],
        out_shape=[jax.ShapeDtypeStruct((1, bsz, tc, d), F32), jax.ShapeDtypeStruct((1, bsz, tc, d), F32),
                   jax.ShapeDtypeStruct((1, bsz, t, d), F32), jax.ShapeDtypeStruct((1, bsz, t, d), F32)],
        scratch_shapes=[pltpu.VMEM((w, w), F32), pltpu.VMEM((w, w), F32)],
        compiler_params=_params("arbitrary", "arbitrary", "arbitrary"),
        name="hgrn",
    )(p_ctx, p_ctx, lf_ctx, lf_ctx, kk_ctx, kk_ctx,
      p_lat, p_lat, lf_lat, kk_lat, p_lat, p_lat, lf_lat, kk_lat)


def _lru_gates(xc, w_all, hb_all, c2_ref):
    w = HEAD_W
    th = jnp.tanh(jnp.dot(xc.astype(BF16), w_all, preferred_element_type=F32) + hb_all)
    out = []
    for dirn in range(2):
        t_r = th[:, (2 * dirn) * w:(2 * dirn + 1) * w]
        t_i = th[:, (2 * dirn + 1) * w:(2 * dirn + 2) * w]
        c2 = c2_ref[dirn:dirn + 1, :]
        a = jnp.exp2(c2 * t_r + c2)
        y = 1.0 - a * a
        u = jnp.where(y > 0.0, y * lax.rsqrt(y), 0.0) * ((0.5 * t_i + 0.5) * xc)
        out += [a, u]
    return out


def _lru_kernel(xl_ref, xc_ref, cw_ref, cb_ref, wa_ref, wx_ref, ba_ref, bx_ref, lam_ref,
                hl_ref, hc_ref,
                xpad, cpad, a_f, u_f, a_b, u_b, ca_f, cu_f, ca_b, cu_b,
                ends_f, prods_f, carry_f, ends_b, prods_b, carry_b, c2_ref,
                *, t, tc):
    w = HEAD_W
    gw = GRID_W
    rows = t // gw
    lam = lam_ref[...]
    neg = -lam
    softplus = jnp.maximum(neg, 0.0) + jnp.log(1.0 + jnp.exp(-jnp.abs(neg)))
    c2_ref[...] = (-0.5 * LRU_C * LOG2E) * softplus
    w_all = (0.5 * jnp.concatenate([wa_ref[0, 0], wx_ref[0, 0], wa_ref[1, 0], wx_ref[1, 0]], axis=1)).astype(BF16)
    hb_all = 0.5 * jnp.concatenate([ba_ref[0:1, :], bx_ref[0:1, :], ba_ref[1:2, :], bx_ref[1:2, :]], axis=1)
    w0, w1, w2, w3 = (cw_ref[k:k + 1, :] for k in range(4))
    cb = cb_ref[...]

    cpad[...] = jnp.zeros_like(cpad)
    cpad[8:8 + tc, :] = xc_ref[0, 0].astype(F32)
    xcc = (w0 * cpad[6:6 + tc, :] + w1 * cpad[7:7 + tc, :] + w2 * cpad[8:8 + tc, :]
           + w3 * cpad[9:9 + tc, :] + cb)
    ca_f[...], cu_f[...], ca_b[...], cu_b[...] = _lru_gates(xcc, w_all, hb_all, c2_ref)

    def ctx_step(i, hs):
        hf, hb = hs
        p = tc - 1 - i
        hf = ca_f[pl.ds(i, 1), :] * hf + cu_f[pl.ds(i, 1), :]
        hb = ca_b[pl.ds(p, 1), :] * hb + cu_b[pl.ds(p, 1), :]
        cu_f[pl.ds(i, 1), :] = hf
        cu_b[pl.ds(p, 1), :] = hb
        return hf, hb
    zero_row = jnp.zeros((1, w), F32)
    s0_f, s0_b = lax.fori_loop(0, tc, ctx_step, (zero_row, zero_row), unroll=8)
    hc_ref[0] = (cu_f[...] + cu_b[...]).astype(hc_ref.dtype)

    xpad[2 * gw:2 * gw + t, :] = xl_ref[0, 0].astype(F32)
    col = lax.broadcasted_iota(I32, (gw, w), 0)
    body0 = 2 * gw
    xpad[gw:2 * gw, :] = jnp.where(col == 0, 0.0, xpad[pl.ds(body0 + (rows - 1) * gw - 1, gw), :])
    xpad[0:gw, :] = jnp.where(col == 0, 0.0, xpad[pl.ds(body0 + (rows - 2) * gw - 1, gw), :])
    xpad[body0 + t:body0 + t + gw, :] = jnp.where(col == gw - 1, 0.0, xpad[pl.ds(body0 + 1, gw), :])

    gate_rows = min(t, 512)

    def gate_body(c, carry_):
        base = pl.multiple_of(c * gate_rows, gate_rows)
        xcv = (w0 * xpad[pl.ds(base, gate_rows), :] + w1 * xpad[pl.ds(base + gw, gate_rows), :]
               + w2 * xpad[pl.ds(base + 2 * gw, gate_rows), :]
               + w3 * xpad[pl.ds(base + 3 * gw, gate_rows), :] + cb)
        sl = pl.ds(base, gate_rows)
        a_f[sl, :], u_f[sl, :], a_b[sl, :], u_b[sl, :] = _lru_gates(xcv, w_all, hb_all, c2_ref)
        return carry_
    lax.fori_loop(0, t // gate_rows, gate_body, 0, unroll=2)

    def slab(r):
        return pl.ds(pl.multiple_of(r * gw, gw), gw)

    def p1(i, c):
        hf, pf, hb, pb = c
        sf, sb = slab(i), slab(rows - 1 - i)
        af, ab = a_f[sf, :], a_b[sb, :]
        return af * hf + u_f[sf, :], af * pf, ab * hb + u_b[sb, :], ab * pb
    zeros, ones = jnp.zeros((gw, w), F32), jnp.ones((gw, w), F32)
    ends_f[...], prods_f[...], ends_b[...], prods_b[...] = lax.fori_loop(
        0, rows, p1, (zeros, ones, zeros, ones), unroll=4)

    def chain(i, c):
        cf, cb = c
        jf, jb = pl.ds(i, 1), pl.ds(gw - 1 - i, 1)
        carry_f[jf, :] = cf
        carry_b[jb, :] = cb
        return prods_f[jf, :] * cf + ends_f[jf, :], prods_b[jb, :] * cb + ends_b[jb, :]
    lax.fori_loop(0, gw, chain, (s0_f, s0_b), unroll=8)

    def p2(i, c):
        hf, hb = c
        sf, sb = slab(i), slab(rows - 1 - i)
        hf = a_f[sf, :] * hf + u_f[sf, :]
        hb = a_b[sb, :] * hb + u_b[sb, :]
        u_f[sf, :] = hf
        u_b[sb, :] = hb
        return hf, hb
    lax.fori_loop(0, rows, p2, (carry_f[...], carry_b[...]), unroll=4)
    hl_ref[0] = (u_f[...] + u_b[...]).astype(hl_ref.dtype)


def _lru_call(p_lat, p_ctx, g_lat, g_ctx, conv_w, conv_b, wa, wx, ba, bx, lam):
    _, bsz, t, d = p_lat.shape
    tc = p_ctx.shape[2]
    w = HEAD_W
    nb = d // w
    gw = GRID_W
    vec = lambda n: pl.BlockSpec((n, w), lambda b, k: (0, k))
    mat = pl.BlockSpec((2, 1, w, w), lambda b, k: (0, k, 0, 0))
    return pl.pallas_call(
        functools.partial(_lru_kernel, t=t, tc=tc),
        grid=(bsz, nb),
        in_specs=[
            pl.BlockSpec((1, 1, t, w), lambda b, k: (g_lat, b, 0, k)),
            pl.BlockSpec((1, 1, tc, w), lambda b, k: (g_ctx, b, 0, k)),
            vec(4), vec(1), mat, mat, vec(2), vec(2), vec(2),
        ],
        out_specs=[pl.BlockSpec((1, t, w), lambda b, k: (b, 0, k)),
                   pl.BlockSpec((1, tc, w), lambda b, k: (b, 0, k))],
        out_shape=[jax.ShapeDtypeStruct((bsz, t, d), BF16), jax.ShapeDtypeStruct((bsz, tc, d), BF16)],
        scratch_shapes=[
            pltpu.VMEM((t + 3 * gw, w), F32), pltpu.VMEM((tc + 16, w), F32),
            pltpu.VMEM((t, w), F32), pltpu.VMEM((t, w), F32), pltpu.VMEM((t, w), F32), pltpu.VMEM((t, w), F32),
            pltpu.VMEM((tc, w), F32), pltpu.VMEM((tc, w), F32), pltpu.VMEM((tc, w), F32), pltpu.VMEM((tc, w), F32),
            pltpu.VMEM((gw, w), F32), pltpu.VMEM((gw, w), F32), pltpu.VMEM((gw, w), F32),
            pltpu.VMEM((gw, w), F32), pltpu.VMEM((gw, w), F32), pltpu.VMEM((gw, w), F32),
            pltpu.VMEM((2, w), F32),
        ],
        compiler_params=_params("arbitrary", "arbitrary"),
        name="rglru",
    )(p_lat, p_ctx, conv_w, conv_b.reshape(1, d), wa, wx, ba, bx, lam)


def _merge_kernel(of_ref, ob_ref, og_ref, hl_ref, ly_ref, ma_ref, mb_ref, x_ref, mod_ref, gn_ref,
                  wa_ref, wb_ref, wo_ref, lng_ref, lnb_ref, wrt_ref,
                  x1_ref, h2_ref, afft_ref, *, ctx_row):
    d = D_MODEL
    b = pl.program_id(0)
    row = b if ctx_row is None else ctx_row
    o = of_ref[0, 0] + ob_ref[0, 0]
    gn = gn_ref[...]
    parts = []
    for h in range(N_HEADS):
        oh = o[:, h * HEAD_W:(h + 1) * HEAD_W]
        ms = jnp.mean(oh * oh, axis=-1, keepdims=True)
        parts.append(oh * lax.rsqrt(ms + RMS_EPS) * gn)
    o_a = (jnp.concatenate(parts, axis=1) * og_ref[0, 0].astype(F32)).astype(BF16)
    y_a = jnp.dot(o_a, wa_ref[...], preferred_element_type=F32)
    y_b = jnp.dot((hl_ref[0].astype(F32) * ly_ref[0, 0].astype(F32)).astype(BF16), wb_ref[...],
                  preferred_element_type=F32)
    z = ma_ref[0, 0].astype(F32) * y_a + mb_ref[0, 0].astype(F32) * y_b
    y = jnp.dot(z.astype(BF16), wo_ref[...], preferred_element_type=F32)
    g1 = mod_ref[pl.ds(row, 1), 2 * d:3 * d]
    x1 = _layer_norm(ALPHA * x_ref[0] + g1 * y, lng_ref[0:1, :], lnb_ref[0:1, :])
    x1_ref[0] = x1
    sh2 = mod_ref[pl.ds(row, 1), 3 * d:4 * d]
    sc2 = mod_ref[pl.ds(row, 1), 4 * d:5 * d]
    h2 = x1 * (1.0 + sc2) + sh2
    h2_ref[0] = h2.astype(BF16)
    logits_t = lax.dot_general(wrt_ref[...], h2, (((1,), (1,)), ((), ())),
                               preferred_element_type=F32, precision=HIGHEST)
    pt = jnp.exp(logits_t - jnp.max(logits_t, axis=0, keepdims=True))
    afft_ref[0] = pt / jnp.sum(pt, axis=0, keepdims=True)


def _merge_call(o_f, o_b, p, groups, h_lru, x, mod_l, gn, wa, wb, wo, ln_g, ln_b, w_router, ctx_row, name):
    bsz, t, d = x.shape
    tm = min(t, 512)
    e = N_EXPERTS
    tile = lambda g: pl.BlockSpec((1, 1, tm, d), lambda b, i: (g, b, i, 0))
    tok = pl.BlockSpec((1, tm, d), lambda b, i: (b, i, 0))
    full = lambda shape: pl.BlockSpec(shape, lambda b, i: tuple(0 for _ in shape))
    return pl.pallas_call(
        functools.partial(_merge_kernel, ctx_row=ctx_row),
        grid=(bsz, t // tm),
        in_specs=[tile(0), tile(0), tile(groups["og"]), tok, tile(groups["ly"]), tile(groups["ma"]),
                  tile(groups["mb"]), tok, full((8, 6 * d)), full((1, HEAD_W)),
                  full((d, d)), full((d, d)), full((d, d)), full((2, d)), full((2, d)),
                  full((e, d))],
        out_specs=[tok, tok, pl.BlockSpec((1, e, tm), lambda b, i: (b, 0, i))],
        out_shape=[jax.ShapeDtypeStruct((bsz, t, d), F32), jax.ShapeDtypeStruct((bsz, t, d), BF16),
                   jax.ShapeDtypeStruct((bsz, e, t), F32)],
        compiler_params=_params("arbitrary", "arbitrary"),
        name=name,
    )(o_f, o_b, p, h_lru, p, p, p, x, mod_l, gn.reshape(1, HEAD_W), wa, wb, wo, ln_g, ln_b,
      w_router.T)


def _route_kernel(a_ref, posm_ref, off_ref, span_ref, *, cap, t, tt, grt):
    e = N_EXPERTS
    u = lax.bitcast_convert_type(a_ref[0], I32)
    thr = jnp.zeros((e, 1), I32)
    for bit in range(30, -1, -1):
        cand = thr | (1 << bit)
        cnt = jnp.sum((u >= cand).astype(F32), axis=1, keepdims=True)
        thr = jnp.where(cnt >= cap, cand, thr)
    gt = u > thr
    eq = u == thr
    need = cap - jnp.sum(gt.astype(F32), axis=1, keepdims=True)
    ri = lax.broadcasted_iota(I32, (tt, tt), 0)
    ci = lax.broadcasted_iota(I32, (tt, tt), 1)
    before = (ri < ci).astype(BF16)
    lane = lax.broadcasted_iota(I32, (e, OFF_STRIDE), 1)
    offs = jnp.zeros((e, OFF_STRIDE), F32)
    offs_end = jnp.zeros((e, OFF_STRIDE), F32)
    n_eq = jnp.zeros((e, 1), F32)
    n_sel = jnp.zeros((e, 1), F32)
    for j in range(t // tt):
        sl = slice(j * tt, (j + 1) * tt)
        eq_j = eq[:, sl]
        rank_eq = jnp.dot(eq_j.astype(BF16), before, preferred_element_type=F32) + n_eq
        sel_j = gt[:, sl] | (eq_j & (rank_eq < need))
        sel_f = sel_j.astype(F32)
        pos = jnp.dot(sel_j.astype(BF16), before, preferred_element_type=F32) + n_sel
        posm_ref[0, :, sl] = jnp.where(sel_j, pos.astype(I32), -1)
        offs = jnp.where(lane == j, n_sel, offs)
        n_eq = n_eq + jnp.sum(eq_j.astype(F32), axis=1, keepdims=True)
        n_sel = n_sel + jnp.sum(sel_f, axis=1, keepdims=True)
        offs_end = jnp.where(lane == j, n_sel, offs_end)
    nt = t // tt
    offs = jnp.where(lane == nt, n_sel, offs)
    off_ref[0] = offs.astype(I32)
    n_rt = cap // grt
    tile = lane < nt
    spans = jnp.zeros((e, OFF_STRIDE), F32)
    for i in range(n_rt):
        skipped = jnp.sum(jnp.where(tile & (offs_end <= i * grt), 1.0, 0.0), axis=1, keepdims=True)
        started = jnp.sum(jnp.where(tile & (offs < (i + 1) * grt), 1.0, 0.0), axis=1, keepdims=True)
        spans = jnp.where(lane == i, skipped, jnp.where(lane == n_rt + i, started, spans))
    span_ref[0] = spans.astype(I32)


def _route_call(aff_t, cap, tt, grt, name):
    bsz, e, t = aff_t.shape
    small = pl.BlockSpec((1, e, OFF_STRIDE), lambda b: (b, 0, 0))
    return pl.pallas_call(
        functools.partial(_route_kernel, cap=cap, t=t, tt=tt, grt=grt),
        grid=(bsz,),
        in_specs=[pl.BlockSpec((1, e, t), lambda b: (b, 0, 0))],
        out_specs=[pl.BlockSpec((1, e, t), lambda b: (b, 0, 0)), small, small],
        out_shape=[jax.ShapeDtypeStruct((bsz, e, t), I32), jax.ShapeDtypeStruct((bsz, e, OFF_STRIDE), I32),
                   jax.ShapeDtypeStruct((bsz, e, OFF_STRIDE), I32)],
        compiler_params=_params("arbitrary"),
        name=name,
    )(aff_t)


def _expert_kernel(span_ref, posm_ref, h_ref, wg_ref, wu_ref, wd_ref, y_ref, x_ref, *, cap, rt, tt, nt, nb):
    if nb == 1:
        b0, e = pl.program_id(0), pl.program_id(1)
    else:
        b0, e = 0, pl.program_id(0)
    x_ref[...] = jnp.zeros_like(x_ref)
    kt = min(nt, GATHER_WINDOW_TILES)
    kw = kt * tt
    rows = lax.broadcasted_iota(I32, (rt, kw), 0)
    lane = lax.broadcasted_iota(I32, (rt, kw), 1)
    n_rt = cap // rt

    for bb in range(nb):
        base = ((b0 + bb) * N_EXPERTS + e) * OFF_STRIDE
        for i in range(n_rt):
            first = span_ref[base + i]
            n_win = (span_ref[base + n_rt + i] - 1 - first) // kt + 1
            out_rows = slice((bb * n_rt + i) * rt, (bb * n_rt + i + 1) * rt)

            def body(wdx, carry, bb=bb, i=i, first=first, out_rows=out_rows):
                want = first + wdx * kt
                start = jnp.minimum(want, nt - kt)
                tok = pl.ds(pl.multiple_of(start * tt, tt), kw)
                hit = ((posm_ref[bb, 0, :, tok] - i * rt) == rows) & (lane >= (want - start) * tt)
                x_ref[out_rows, :] += jnp.dot(hit.astype(BF16), h_ref[bb, tok, :],
                                              preferred_element_type=F32).astype(BF16)
                return carry
            lax.fori_loop(0, n_win, body, 0)
    wg, wu, wd = wg_ref[0, 0].astype(BF16), wu_ref[0, 0].astype(BF16), wd_ref[0, 0].astype(BF16)
    total = nb * cap
    ft = min(total, 256)
    for r in range(total // ft):
        xb = x_ref[r * ft:(r + 1) * ft, :]
        g = jnp.dot(xb, wg, preferred_element_type=F32)
        u = jnp.dot(xb, wu, preferred_element_type=F32)
        hid = (_silu(g) * u).astype(BF16)
        out = jnp.dot(hid, wd, preferred_element_type=F32).astype(BF16)
        if ft <= cap:
            row0 = (r * ft) % cap
            y_ref[(r * ft) // cap, 0, row0:row0 + ft, :] = out
        else:
            for q in range(ft // cap):
                y_ref[r * (ft // cap) + q, 0, :, :] = out[q * cap:(q + 1) * cap, :]


def _expert_call(span_flat, posm, h2, w_gate, w_up, w_down, layer, cap, rt, tt, name):
    bsz, t, d = h2.shape
    _, e, _, f = w_gate.shape
    nt = t // tt
    if t < d:
        nb, grid = bsz, (e,)
        bk = lambda k, off: (0, k)
        tok_spec = pl.BlockSpec((bsz, t, d), lambda k, off: (0, 0, 0))
    else:
        nb, grid = 1, (bsz, e)
        bk = lambda b, k, off: (b, k)
        tok_spec = pl.BlockSpec((1, t, d), lambda b, k, off: (b, 0, 0), pipeline_mode=pl.Buffered(1))
    weight = lambda r, c: pl.BlockSpec((1, 1, r, c), lambda *i: (layer, bk(*i)[1], 0, 0))
    grid_spec = pltpu.PrefetchScalarGridSpec(
        num_scalar_prefetch=1,
        grid=grid,
        in_specs=[
            pl.BlockSpec((nb, 1, 1, t), lambda *i: (*bk(*i), 0, 0)),
            tok_spec, weight(d, f), weight(d, f), weight(f, d),
        ],
        out_specs=pl.BlockSpec((nb, 1, cap, d), lambda *i: (*bk(*i), 0, 0)),
        scratch_shapes=[pltpu.VMEM((nb * cap, d), BF16)],
    )
    return pl.pallas_call(
        functools.partial(_expert_kernel, cap=cap, rt=rt, tt=tt, nt=nt, nb=nb),
        grid_spec=grid_spec,
        out_shape=jax.ShapeDtypeStruct((bsz, e, cap, d), BF16),
        compiler_params=_params(*(("arbitrary",) * len(grid))),
        name=name,
    )(span_flat, posm.reshape(bsz, e, 1, t), h2, w_gate, w_up, w_down)


def _combine_kernel(off_ref, posm_ref, afft_ref, x1_ref, y_hbm, mod_ref, lng_ref, lnb_ref, o_ref,
                    ybuf, pbuf, sem, acc_ref, *, rt, tt, nw, ctx_row, n_tiles):
    d = D_MODEL
    b = pl.program_id(0)
    j = pl.program_id(1)
    row = b if ctx_row is None else ctx_row
    shift = rt.bit_length() - 1
    step = b * n_tiles + j
    cur = lax.rem(step, 2)
    is_last = step == pl.num_programs(0) * n_tiles - 1

    def plan(bb, jj):
        out, n_used = [], jnp.int32(0)
        for e in range(N_EXPERTS):
            base = (bb * N_EXPERTS + e) * OFF_STRIDE
            lo = off_ref[base + jj]
            hi = off_ref[base + jj + 1]
            t0 = lax.shift_right_logical(lo, shift)
            n = jnp.where(hi > lo, lax.shift_right_logical(hi - 1, shift) - t0 + 1, 0)
            out.append((t0, n, n_used))
            n_used = n_used + n
        return out, n_used

    def for_used_windows(windows, fn):
        for e, (t0, n, s0) in enumerate(windows):
            def body(wdx, carry, e=e, t0=t0, s0=s0):
                fn(e, t0 + wdx, s0 + wdx)
                return carry
            lax.fori_loop(0, n, body, 0)

    def window_copy(bb, e, window, slot, half):
        src = y_hbm.at[bb, e, pl.ds(pl.multiple_of(window * rt, rt), rt), :]
        dst = ybuf.at[half, pl.ds(pl.multiple_of(slot * rt, rt), rt), :]
        return pltpu.make_async_copy(src, dst, sem.at[half])

    def fetch(bb, windows, half):
        for_used_windows(windows, lambda e, window, slot: window_copy(bb, e, window, slot, half).start())

    windows, n_used = plan(b, j)
    n_groups = lax.shift_right_logical(n_used + (COMBINE_GROUP - 1), COMBINE_GROUP.bit_length() - 1)

    @pl.when(step == 0)
    def _():
        ybuf[...] = jnp.zeros_like(ybuf)
        fetch(b, windows, 0)

    @pl.when(jnp.logical_not(is_last))
    def _():
        wrap = j == n_tiles - 1
        b_next = jnp.where(wrap, b + 1, b)
        fetch(b_next, plan(b_next, jnp.where(wrap, 0, j + 1))[0], 1 - cur)

    def clear(s, carry):
        pbuf[pl.ds(pl.multiple_of(s * rt, rt), rt), :] = jnp.zeros((rt, tt), BF16)
        return carry
    lax.fori_loop(n_used, n_groups * COMBINE_GROUP, clear, 0)

    rows = lax.broadcasted_iota(I32, (rt, tt), 0)

    def weights(e, window, slot):
        pos = posm_ref[0, e]
        gate = afft_ref[0, e:e + 1, :]
        pbuf[pl.ds(pl.multiple_of(slot * rt, rt), rt), :] = jnp.where(
            (pos - window * rt) == rows, gate, 0.0).astype(BF16)
    for_used_windows(windows, weights)

    def drain(i, carry):
        window_copy(0, 0, 0, 0, cur).wait()
        return carry
    lax.fori_loop(0, n_used, drain, 0)

    acc_ref[...] = jnp.zeros_like(acc_ref)
    gk = COMBINE_GROUP * rt
    for g in range(pl.cdiv(N_EXPERTS * nw, COMBINE_GROUP)):
        @pl.when(g < n_groups)
        def _(g=g):
            acc_ref[...] += lax.dot_general(pbuf[g * gk:(g + 1) * gk, :], ybuf[cur, g * gk:(g + 1) * gk, :],
                                            (((0,), (0,)), ((), ())), preferred_element_type=F32)
    g2 = mod_ref[pl.ds(row, 1), 5 * d:6 * d]
    o_ref[0] = _layer_norm(ALPHA * x1_ref[0] + g2 * acc_ref[...], lng_ref[1:2, :], lnb_ref[1:2, :])


def _combine_call(off_flat, posm, aff_t, x1, y, mod_l, ln_g, ln_b, rt, tt, ctx_row, name):
    bsz, t, d = x1.shape
    e = N_EXPERTS
    cap = y.shape[2]
    nw = min(cap // rt, (tt - 1) // rt + 2)
    n_slots = pl.cdiv(e * nw, COMBINE_GROUP) * COMBINE_GROUP
    grid_spec = pltpu.PrefetchScalarGridSpec(
        num_scalar_prefetch=1,
        grid=(bsz, t // tt),
        in_specs=[
            pl.BlockSpec((1, e, 1, tt), lambda b, j, off: (b, 0, 0, j)),
            pl.BlockSpec((1, e, tt), lambda b, j, off: (b, 0, j)),
            pl.BlockSpec((1, tt, d), lambda b, j, off: (b, j, 0)),
            pl.BlockSpec(memory_space=pl.ANY),
            pl.BlockSpec((8, 6 * d), lambda b, j, off: (0, 0)),
            pl.BlockSpec((2, d), lambda b, j, off: (0, 0)),
            pl.BlockSpec((2, d), lambda b, j, off: (0, 0)),
        ],
        out_specs=pl.BlockSpec((1, tt, d), lambda b, j, off: (b, j, 0)),
        scratch_shapes=[pltpu.VMEM((2, n_slots * rt, d), BF16), pltpu.VMEM((n_slots * rt, tt), BF16),
                        pltpu.SemaphoreType.DMA((2,)), pltpu.VMEM((tt, d), F32)],
    )
    return pl.pallas_call(
        functools.partial(_combine_kernel, rt=rt, tt=tt, nw=nw, ctx_row=ctx_row, n_tiles=t // tt),
        grid_spec=grid_spec,
        out_shape=jax.ShapeDtypeStruct((bsz, t, d), F32),
        compiler_params=_params("arbitrary", "arbitrary"),
        name=name,
    )(off_flat, posm.reshape(bsz, e, 1, t), aff_t, x1, y, mod_l, ln_g, ln_b)


def _moe(x1, h2, aff_t, mod_l, w_gate, w_up, w_down, layer, ln_g, ln_b, ctx_row, tag):
    bsz, t, d = x1.shape
    cap = EC_FACTOR * t // N_EXPERTS
    tt = min(t, 256)
    grt = min(cap, GATHER_ROWS)
    posm, off, span = _route_call(aff_t, cap, tt, grt, "route_" + tag)
    y = _expert_call(span.reshape(-1), posm, h2, w_gate, w_up, w_down, layer, cap, grt, tt, "expert_" + tag)
    return _combine_call(off.reshape(-1), posm, aff_t, x1, y, mod_l, ln_g, ln_b, min(cap, SCATTER_ROWS), tt, ctx_row,
                         "combine_" + tag)


def kernel(x, c, ctx, c_ctx, w_mod, b_mod, w_in, hgrn_lb_logits, hgrn_norm_g, conv_w, conv_b, lru_wa,
           lru_ba, lru_wx, lru_bx, lru_lambda, w_branch_a, w_branch_b, w_out, ln_g, ln_b, w_router,
           w_gate, w_up, w_down):
    depth = w_in.shape[0]
    bsz = x.shape[0]
    ctx_row = bsz
    assert bsz < 8 and depth == DEPTH

    lb_cum = jnp.cumsum(jax.nn.softmax(hgrn_lb_logits.astype(F32), axis=0), axis=0)
    lbs = lb_cum - lb_cum[0]
    log1m_lb = jnp.log1p(-lbs)

    c_all = jnp.zeros((8, x.shape[2]), F32).at[:bsz].set(c).at[ctx_row].set(c_ctx)
    mod = _mod_call(c_all, w_mod, b_mod)

    p_index = P_INDEX
    for l in range(depth):
        need_ctx = l < depth - 1
        mod_l = mod[l]
        w_in_bf = w_in[l].astype(BF16)
        p_lat, lf_lat, kk_lat = _proj_call(x, mod_l, w_in_bf, lbs[l], log1m_lb[l], None, "proj_lat")
        p_ctx, lf_ctx, kk_ctx = _proj_call(ctx, mod_l, w_in_bf, lbs[l], log1m_lb[l], ctx_row,
                                           "proj_ctx")

        oc_f, oc_b, ol_f, ol_b = _hgrn_call(p_lat, lf_lat, kk_lat, p_ctx, lf_ctx, kk_ctx)
        h_lat, h_ctx = _lru_call(p_lat, p_ctx, p_index["lx"], p_index["lx"], conv_w[l], conv_b[l],
                                 lru_wa[l], lru_wx[l], lru_ba[l], lru_bx[l], lru_lambda[l])

        wa, wb, wo = (w.astype(BF16) for w in (w_branch_a[l], w_branch_b[l], w_out[l]))
        x1, h2, aff_t = _merge_call(ol_f, ol_b, p_lat, p_index, h_lat, x, mod_l, hgrn_norm_g[l],
                                         wa, wb, wo, ln_g[l], ln_b[l], w_router[l], None, "merge_lat")
        if need_ctx:
            c1, ch2, caff_t = _merge_call(oc_f, oc_b, p_ctx, p_index, h_ctx, ctx, mod_l, hgrn_norm_g[l],
                                                wa, wb, wo, ln_g[l], ln_b[l], w_router[l], ctx_row,
                                                "merge_ctx")
        x = _moe(x1, h2, aff_t, mod_l, w_gate, w_up, w_down, l, ln_g[l], ln_b[l], None, "lat")
        if need_ctx:
            ctx = _moe(c1, ch2, caff_t, mod_l, w_gate, w_up, w_down, l, ln_g[l], ln_b[l], ctx_row, "ctx")
    return x
```

```python
import functools

import jax
import jax.numpy as jnp
from jax import lax
from jax.experimental import pallas as pl
from jax.experimental.pallas import tpu as pltpu

F32 = jnp.float32
BF16 = jnp.bfloat16
I32 = jnp.int32
HIGHEST = lax.Precision.HIGHEST

DEPTH = 2
D_MODEL = 1024
GRID_W = 64
N_HEADS = 8
HEAD_W = 128
CHUNK = 64
N_EXPERTS = 16
EC_FACTOR = 2
LRU_C = 8.0
ALPHA = (2.0 * DEPTH) ** 0.25
LN_EPS = 1e-5
RMS_EPS = 1e-6
LN2 = 0.6931471805599453
LOG2E = 1.4426950408889634
SQRT_2_OVER_PI = 0.7978845608028654
EXP_CLAMP = 80.0

V7X_VMEM_LIMIT_BYTES = 56 * 1024 * 1024
OFF_STRIDE = 128
GATHER_WINDOW_TILES = 6
COMBINE_GROUP = 8
GATHER_ROWS = 128
SCATTER_ROWS = 64

G_Q, G_V, G_FF, G_FB, G_OG, G_LX, G_LY, G_MA, G_MB = range(9)


def _params(*sem):
    return pltpu.CompilerParams(dimension_semantics=sem, vmem_limit_bytes=V7X_VMEM_LIMIT_BYTES)


def _sigmoid(x):
    return 0.5 * jnp.tanh(0.5 * x) + 0.5


def _silu(x):
    h = 0.5 * x
    return h * jnp.tanh(h) + h


def _layer_norm(x, g, b):
    mu = jnp.mean(x, axis=-1, keepdims=True)
    xc = x - mu
    var = jnp.mean(xc * xc, axis=-1, keepdims=True)
    return xc * lax.rsqrt(var + LN_EPS) * g + b


def _mod_kernel(c_ref, w_ref, b_ref, o_ref):
    c = c_ref[...]
    s = _silu(c)
    o_ref[0] = jnp.dot(s, w_ref[0], preferred_element_type=F32, precision=HIGHEST) + b_ref[0]


def _mod_call(c_all, w_mod, b_mod):
    depth, d, n = w_mod.shape
    tn = 1536
    return pl.pallas_call(
        _mod_kernel,
        grid=(depth, n // tn),
        in_specs=[
            pl.BlockSpec((8, d), lambda l, j: (0, 0)),
            pl.BlockSpec((1, d, tn), lambda l, j: (l, 0, j)),
            pl.BlockSpec((1, 1, tn), lambda l, j: (l, 0, j)),
        ],
        out_specs=pl.BlockSpec((1, 8, tn), lambda l, j: (l, 0, j)),
        out_shape=jax.ShapeDtypeStruct((depth, 8, n), F32),
        compiler_params=_params("arbitrary", "arbitrary"),
        name="mod",
    )(c_all, w_mod, b_mod.reshape(depth, 1, n))


def _apply_act(act, p):
    if act == "id":
        return p
    if act == "silu_scale":
        return _silu(p) * (HEAD_W ** -0.5)
    if act == "silu":
        return _silu(p)
    if act == "sigmoid":
        return _sigmoid(p)
    if act == "gelu":
        ph = 0.5 * p
        return ph * jnp.tanh(p * ((0.044715 * SQRT_2_OVER_PI) * (p * p) + SQRT_2_OVER_PI)) + ph
    raise ValueError(act)


P_GROUPS = (G_Q, G_V, G_OG, G_LX, G_LY, G_MA, G_MB)
P_ACTS = ("silu_scale", "id", "silu", "id", "gelu", "sigmoid", "sigmoid")
P_INDEX = {"q": 0, "v": 1, "og": 2, "lx": 3, "ly": 4, "ma": 5, "mb": 6}


def _hgrn_gates(fp, lb, l1):
    kh = 0.5 * (1.0 - lb)
    p = kh * jnp.tanh(0.5 * fp)
    log_f = jnp.maximum(jnp.log((lb + kh) + p), jnp.minimum(fp, 0.0) + (l1 - LN2))
    return log_f, kh - p


def _proj_kernel(x_ref, mod_ref, w_ref, lb_ref, l1_ref, p_ref, lf_ref, kk_ref, *, ctx_row):
    d = D_MODEL
    b = pl.program_id(0)
    row = b if ctx_row is None else ctx_row
    sh = mod_ref[pl.ds(row, 1), 0:d]
    sc = mod_ref[pl.ds(row, 1), d:2 * d]
    h = (x_ref[0] * (1.0 + sc) + sh).astype(BF16)

    def group(col):
        return jnp.dot(h, w_ref[:, col * d:(col + 1) * d], preferred_element_type=F32)

    def plain(name):
        g = P_INDEX[name]
        p_ref[g, 0] = _apply_act(P_ACTS[g], group(P_GROUPS[g])).astype(BF16)

    def gate(dirn, col):
        log_f, k = _hgrn_gates(group(col), lb_ref[dirn:dirn + 1, :], l1_ref[dirn:dirn + 1, :])
        lf_ref[dirn, 0] = log_f
        kk_ref[dirn, 0] = k.astype(BF16)

    gate(0, G_FF)
    plain("v")
    gate(1, G_FB)
    plain("lx")
    plain("q")
    plain("ly")
    plain("og")
    plain("ma")
    plain("mb")


def _proj_call(x, mod_l, w_in_bf, lb, l1, ctx_row, name):
    bsz, t, d = x.shape
    tm = min(t, 512)
    ng = len(P_GROUPS)
    full = lambda shape: pl.BlockSpec(shape, lambda b, i: tuple(0 for _ in shape))
    out = lambda n: pl.BlockSpec((n, 1, tm, d), lambda b, i: (0, b, i, 0))
    return pl.pallas_call(
        functools.partial(_proj_kernel, ctx_row=ctx_row),
        grid=(bsz, t // tm),
        in_specs=[
            pl.BlockSpec((1, tm, d), lambda b, i: (b, i, 0)),
            full((8, 6 * d)),
            pl.BlockSpec(w_in_bf.shape, lambda b, i: (0, 0), pipeline_mode=pl.Buffered(1)),
            full((2, d)), full((2, d)),
        ],
        out_specs=[out(ng), out(2), out(2)],
        out_shape=[jax.ShapeDtypeStruct((ng, bsz, t, d), BF16), jax.ShapeDtypeStruct((2, bsz, t, d), F32),
                   jax.ShapeDtypeStruct((2, bsz, t, d), BF16)],
        compiler_params=_params("arbitrary", "arbitrary"),
        name=name,
    )(x, mod_l, w_in_bf, lb, l1)


def _hgrn_prepare(q_ref, v_ref, lf_ref, k_ref, row0, n, reverse):
    w = HEAD_W
    rows = pl.ds(row0, n * CHUNK)
    log_f = lf_ref[0, 0, rows, :]
    k = k_ref[0, 0, rows, :].astype(F32)
    q = q_ref[0, 0, rows, :]
    v = v_ref[0, 0, rows, :]

    ri = lax.broadcasted_iota(I32, (CHUNK, CHUNK), 0)
    ci = lax.broadcasted_iota(I32, (CHUNK, CHUNK), 1)
    causal = (ci >= ri) if reverse else (ci <= ri)
    tri = causal.astype(BF16)
    lf = jnp.concatenate([log_f[c * CHUNK:(c + 1) * CHUNK, :] for c in range(n)], axis=1)
    hi = lf.astype(BF16)
    rest = lf - hi.astype(F32)
    mid = rest.astype(BF16)
    lo = (rest - mid.astype(F32)).astype(BF16)
    bc = (jnp.dot(tri, hi, preferred_element_type=F32) + jnp.dot(tri, mid, preferred_element_type=F32)
          + jnp.dot(tri, lo, preferred_element_type=F32))

    half = CHUNK // 2
    chunks = []
    for c in range(n):
        sl = slice(c * CHUNK, (c + 1) * CHUNK)
        bcum = bc[:, c * w:(c + 1) * w]
        b_ref = bcum[half:half + 1, :]
        b_end = bcum[0:1, :] if reverse else bcum[CHUNK - 1:CHUNK, :]
        e1 = jnp.exp(jnp.minimum(bcum - b_ref, EXP_CLAMP))
        e2 = jnp.exp(jnp.minimum(b_ref - bcum, EXP_CLAMP))
        qt = q[sl].astype(F32) * e1
        kt = k[sl] * e2
        att = lax.dot_general(qt.astype(BF16), kt.astype(BF16), (((1,), (1,)), ((), ())),
                              preferred_element_type=F32)
        att = jnp.where(causal, att, 0.0).astype(BF16)
        o_intra = jnp.dot(att, v[sl], preferred_element_type=F32)
        qs = (qt * jnp.exp(b_ref)).astype(BF16)
        ke = (kt * jnp.exp(b_end - b_ref)).astype(BF16)
        upd = lax.dot_general(v[sl], ke, (((0,), (0,)), ((), ())), preferred_element_type=F32)
        chunks.append((o_intra, qs, upd, jnp.exp(b_end)))
    return chunks


def _hgrn_state_step(chunk, st, o_ref, row):
    o_intra, qs, upd, decay = chunk
    o_ref[0, 0, pl.ds(row, CHUNK), :] = o_intra + lax.dot_general(
        qs, st.astype(BF16), (((1,), (1,)), ((), ())), preferred_element_type=F32)
    return decay * st + upd


HGRN_SUB = 32


def _hgrn_kernel(qc_ref, vc_ref, lcf_ref, lcb_ref, kcf_ref, kcb_ref,
                 qf_ref, vf_ref, lff_ref, kf_ref, qb_ref, vb_ref, lfb_ref, kb_ref,
                 ocf_ref, ocb_ref, of_ref, ob_ref, sf_ref, sb_ref, *, n_ctx_chunks, n_lat_chunks):
    s = pl.program_id(2)

    def run(n_chunks, fwd, o_f, bwd, o_b):
        sub = min(n_chunks, HGRN_SUB)
        n_sub = n_chunks // sub

        def body(i, carry):
            rf = pl.multiple_of(i * (sub * CHUNK), sub * CHUNK)
            rb = pl.multiple_of((n_sub - 1 - i) * (sub * CHUNK), sub * CHUNK)
            cf = _hgrn_prepare(*fwd, rf, sub, False)
            cb = _hgrn_prepare(*bwd, rb, sub, True)
            st_f, st_b = sf_ref[...], sb_ref[...]
            for c in range(sub):
                st_f = _hgrn_state_step(cf[c], st_f, o_f, rf + c * CHUNK)
                st_b = _hgrn_state_step(cb[sub - 1 - c], st_b, o_b, rb + (sub - 1 - c) * CHUNK)
            sf_ref[...] = st_f
            sb_ref[...] = st_b
            return carry
        lax.fori_loop(0, n_sub, body, 0)

    @pl.when(s == 0)
    def _():
        sf_ref[...] = jnp.zeros_like(sf_ref)
        sb_ref[...] = jnp.zeros_like(sb_ref)
        run(n_ctx_chunks, (qc_ref, vc_ref, lcf_ref, kcf_ref), ocf_ref, (qc_ref, vc_ref, lcb_ref, kcb_ref), ocb_ref)

    run(n_lat_chunks, (qf_ref, vf_ref, lff_ref, kf_ref), of_ref, (qb_ref, vb_ref, lfb_ref, kb_ref), ob_ref)


def _hgrn_call(p_lat, lf_lat, kk_lat, p_ctx, lf_ctx, kk_ctx):
    _, bsz, t, d = p_lat.shape
    tc = p_ctx.shape[2]
    w = HEAD_W
    blk = min(t, 4096)
    ns = t // blk
    lat = lambda g, rev: pl.BlockSpec(
        (1, 1, blk, w), (lambda b, h, s: (g, b, ns - 1 - s, h)) if rev else (lambda b, h, s: (g, b, s, h)))
    ctx = lambda g: pl.BlockSpec((1, 1, tc, w), lambda b, h, s: (g, b, 0, h))
    return pl.pallas_call(
        functools.partial(_hgrn_kernel, n_ctx_chunks=tc // CHUNK, n_lat_chunks=blk // CHUNK),
        grid=(bsz, N_HEADS, ns),
        in_specs=[ctx(0), ctx(1), ctx(0), ctx(1), ctx(0), ctx(1),
                  lat(0, False), lat(1, False), lat(0, False), lat(0, False),
                  lat(0, True), lat(1, True), lat(1, True), lat(1, True)],
        out_specs=[ctx(0), ctx(0), lat(0, False), lat(0, True)],
        out_shape=[jax.ShapeDtypeStruct((1, bsz, tc, d), F32), jax.ShapeDtypeStruct((1, bsz, tc, d), F32),
                   jax.ShapeDtypeStruct((1, bsz, t, d), F32), jax.ShapeDtypeStruct((1, bsz, t, d), F32)],
        scratch_shapes=[pltpu.VMEM((w, w), F32), pltpu.VMEM((w, w), F32)],
        compiler_params=_params("arbitrary", "arbitrary", "arbitrary"),
        name="hgrn",
    )(p_ctx, p_ctx, lf_ctx, lf_ctx, kk_ctx, kk_ctx,
      p_lat, p_lat, lf_lat, kk_lat, p_lat, p_lat, lf_lat, kk_lat)


def _lru_gates(xc, w_all, hb_all, c2_ref):
    w = HEAD_W
    th = jnp.tanh(jnp.dot(xc.astype(BF16), w_all, preferred_element_type=F32) + hb_all)
    out = []
    for dirn in range(2):
        t_r = th[:, (2 * dirn) * w:(2 * dirn + 1) * w]
        t_i = th[:, (2 * dirn + 1) * w:(2 * dirn + 2) * w]
        c2 = c2_ref[dirn:dirn + 1, :]
        a = jnp.exp2(c2 * t_r + c2)
        y = 1.0 - a * a
        u = jnp.where(y > 0.0, y * lax.rsqrt(y), 0.0) * ((0.5 * t_i + 0.5) * xc)
        out += [a, u]
    return out


def _lru_kernel(xl_ref, xc_ref, cw_ref, cb_ref, wa_ref, wx_ref, ba_ref, bx_ref, lam_ref,
                hl_ref, hc_ref,
                xpad, cpad, a_f, u_f, a_b, u_b, ca_f, cu_f, ca_b, cu_b,
                ends_f, prods_f, carry_f, ends_b, prods_b, carry_b, c2_ref,
                *, t, tc):
    w = HEAD_W
    gw = GRID_W
    rows = t // gw
    lam = lam_ref[...]
    neg = -lam
    softplus = jnp.maximum(neg, 0.0) + jnp.log(1.0 + jnp.exp(-jnp.abs(neg)))
    c2_ref[...] = (-0.5 * LRU_C * LOG2E) * softplus
    w_all = (0.5 * jnp.concatenate([wa_ref[0, 0], wx_ref[0, 0], wa_ref[1, 0], wx_ref[1, 0]], axis=1)).astype(BF16)
    hb_all = 0.5 * jnp.concatenate([ba_ref[0:1, :], bx_ref[0:1, :], ba_ref[1:2, :], bx_ref[1:2, :]], axis=1)
    w0, w1, w2, w3 = (cw_ref[k:k + 1, :] for k in range(4))
    cb = cb_ref[...]

    cpad[...] = jnp.zeros_like(cpad)
    cpad[8:8 + tc, :] = xc_ref[0, 0].astype(F32)
    xcc = (w0 * cpad[6:6 + tc, :] + w1 * cpad[7:7 + tc, :] + w2 * cpad[8:8 + tc, :]
           + w3 * cpad[9:9 + tc, :] + cb)
    ca_f[...], cu_f[...], ca_b[...], cu_b[...] = _lru_gates(xcc, w_all, hb_all, c2_ref)

    def ctx_step(i, hs):
        hf, hb = hs
        p = tc - 1 - i
        hf = ca_f[pl.ds(i, 1), :] * hf + cu_f[pl.ds(i, 1), :]
        hb = ca_b[pl.ds(p, 1), :] * hb + cu_b[pl.ds(p, 1), :]
        cu_f[pl.ds(i, 1), :] = hf
        cu_b[pl.ds(p, 1), :] = hb
        return hf, hb
    zero_row = jnp.zeros((1, w), F32)
    s0_f, s0_b = lax.fori_loop(0, tc, ctx_step, (zero_row, zero_row), unroll=8)
    hc_ref[0] = (cu_f[...] + cu_b[...]).astype(hc_ref.dtype)

    xpad[2 * gw:2 * gw + t, :] = xl_ref[0, 0].astype(F32)
    col = lax.broadcasted_iota(I32, (gw, w), 0)
    body0 = 2 * gw
    xpad[gw:2 * gw, :] = jnp.where(col == 0, 0.0, xpad[pl.ds(body0 + (rows - 1) * gw - 1, gw), :])
    xpad[0:gw, :] = jnp.where(col == 0, 0.0, xpad[pl.ds(body0 + (rows - 2) * gw - 1, gw), :])
    xpad[body0 + t:body0 + t + gw, :] = jnp.where(col == gw - 1, 0.0, xpad[pl.ds(body0 + 1, gw), :])

    gate_rows = min(t, 512)

    def gate_body(c, carry_):
        base = pl.multiple_of(c * gate_rows, gate_rows)
        xcv = (w0 * xpad[pl.ds(base, gate_rows), :] + w1 * xpad[pl.ds(base + gw, gate_rows), :]
               + w2 * xpad[pl.ds(base + 2 * gw, gate_rows), :]
               + w3 * xpad[pl.ds(base + 3 * gw, gate_rows), :] + cb)
        sl = pl.ds(base, gate_rows)
        a_f[sl, :], u_f[sl, :], a_b[sl, :], u_b[sl, :] = _lru_gates(xcv, w_all, hb_all, c2_ref)
        return carry_
    lax.fori_loop(0, t // gate_rows, gate_body, 0, unroll=4)

    def slab(r):
        return pl.ds(pl.multiple_of(r * gw, gw), gw)

    def p1(i, c):
        hf, pf, hb, pb = c
        sf, sb = slab(i), slab(rows - 1 - i)
        af, ab = a_f[sf, :], a_b[sb, :]
        return af * hf + u_f[sf, :], af * pf, ab * hb + u_b[sb, :], ab * pb
    zeros, ones = jnp.zeros((gw, w), F32), jnp.ones((gw, w), F32)
    ends_f[...], prods_f[...], ends_b[...], prods_b[...] = lax.fori_loop(
        0, rows, p1, (zeros, ones, zeros, ones), unroll=4)

    def chain(i, c):
        cf, cb = c
        jf, jb = pl.ds(i, 1), pl.ds(gw - 1 - i, 1)
        carry_f[jf, :] = cf
        carry_b[jb, :] = cb
        return prods_f[jf, :] * cf + ends_f[jf, :], prods_b[jb, :] * cb + ends_b[jb, :]
    lax.fori_loop(0, gw, chain, (s0_f, s0_b), unroll=8)

    def p2(i, c):
        hf, hb = c
        sf, sb = slab(i), slab(rows - 1 - i)
        hf = a_f[sf, :] * hf + u_f[sf, :]
        hb = a_b[sb, :] * hb + u_b[sb, :]
        u_f[sf, :] = hf
        u_b[sb, :] = hb
        return hf, hb
    lax.fori_loop(0, rows, p2, (carry_f[...], carry_b[...]), unroll=4)
    hl_ref[0] = (u_f[...] + u_b[...]).astype(hl_ref.dtype)


def _lru_call(p_lat, p_ctx, g_lat, g_ctx, conv_w, conv_b, wa, wx, ba, bx, lam):
    _, bsz, t, d = p_lat.shape
    tc = p_ctx.shape[2]
    w = HEAD_W
    nb = d // w
    gw = GRID_W
    vec = lambda n: pl.BlockSpec((n, w), lambda b, k: (0, k))
    mat = pl.BlockSpec((2, 1, w, w), lambda b, k: (0, k, 0, 0))
    return pl.pallas_call(
        functools.partial(_lru_kernel, t=t, tc=tc),
        grid=(bsz, nb),
        in_specs=[
            pl.BlockSpec((1, 1, t, w), lambda b, k: (g_lat, b, 0, k)),
            pl.BlockSpec((1, 1, tc, w), lambda b, k: (g_ctx, b, 0, k)),
            vec(4), vec(1), mat, mat, vec(2), vec(2), vec(2),
        ],
        out_specs=[pl.BlockSpec((1, t, w), lambda b, k: (b, 0, k)),
                   pl.BlockSpec((1, tc, w), lambda b, k: (b, 0, k))],
        out_shape=[jax.ShapeDtypeStruct((bsz, t, d), BF16), jax.ShapeDtypeStruct((bsz, tc, d), BF16)],
        scratch_shapes=[
            pltpu.VMEM((t + 3 * gw, w), F32), pltpu.VMEM((tc + 16, w), F32),
            pltpu.VMEM((t, w), F32), pltpu.VMEM((t, w), F32), pltpu.VMEM((t, w), F32), pltpu.VMEM((t, w), F32),
            pltpu.VMEM((tc, w), F32), pltpu.VMEM((tc, w), F32), pltpu.VMEM((tc, w), F32), pltpu.VMEM((tc, w), F32),
            pltpu.VMEM((gw, w), F32), pltpu.VMEM((gw, w), F32), pltpu.VMEM((gw, w), F32),
            pltpu.VMEM((gw, w), F32), pltpu.VMEM((gw, w), F32), pltpu.VMEM((gw, w), F32),
            pltpu.VMEM((2, w), F32),
        ],
        compiler_params=_params("arbitrary", "arbitrary"),
        name="rglru",
    )(p_lat, p_ctx, conv_w, conv_b.reshape(1, d), wa, wx, ba, bx, lam)


def _merge_kernel(of_ref, ob_ref, og_ref, hl_ref, ly_ref, ma_ref, mb_ref, x_ref, mod_ref, gn_ref,
                  wa_ref, wb_ref, wo_ref, lng_ref, lnb_ref, wrt_ref,
                  x1_ref, h2_ref, afft_ref, *, ctx_row):
    d = D_MODEL
    b = pl.program_id(0)
    row = b if ctx_row is None else ctx_row
    o = of_ref[0, 0] + ob_ref[0, 0]
    gn = gn_ref[...]
    parts = []
    for h in range(N_HEADS):
        oh = o[:, h * HEAD_W:(h + 1) * HEAD_W]
        ms = jnp.mean(oh * oh, axis=-1, keepdims=True)
        parts.append(oh * lax.rsqrt(ms + RMS_EPS) * gn)
    o_a = (jnp.concatenate(parts, axis=1) * og_ref[0, 0].astype(F32)).astype(BF16)
    y_a = jnp.dot(o_a, wa_ref[...], preferred_element_type=F32)
    y_b = jnp.dot((hl_ref[0].astype(F32) * ly_ref[0, 0].astype(F32)).astype(BF16), wb_ref[...],
                  preferred_element_type=F32)
    z = ma_ref[0, 0].astype(F32) * y_a + mb_ref[0, 0].astype(F32) * y_b
    y = jnp.dot(z.astype(BF16), wo_ref[...], preferred_element_type=F32)
    g1 = mod_ref[pl.ds(row, 1), 2 * d:3 * d]
    x1 = _layer_norm(ALPHA * x_ref[0] + g1 * y, lng_ref[0:1, :], lnb_ref[0:1, :])
    x1_ref[0] = x1
    sh2 = mod_ref[pl.ds(row, 1), 3 * d:4 * d]
    sc2 = mod_ref[pl.ds(row, 1), 4 * d:5 * d]
    h2 = x1 * (1.0 + sc2) + sh2
    h2_ref[0] = h2.astype(BF16)
    logits_t = lax.dot_general(wrt_ref[...], h2, (((1,), (1,)), ((), ())),
                               preferred_element_type=F32, precision=HIGHEST)
    pt = jnp.exp(logits_t - jnp.max(logits_t, axis=0, keepdims=True))
    afft_ref[0] = pt / jnp.sum(pt, axis=0, keepdims=True)


def _merge_call(o_f, o_b, p, groups, h_lru, x, mod_l, gn, wa, wb, wo, ln_g, ln_b, w_router, ctx_row, name):
    bsz, t, d = x.shape
    tm = min(t, 512)
    e = N_EXPERTS
    tile = lambda g: pl.BlockSpec((1, 1, tm, d), lambda b, i: (g, b, i, 0))
    tok = pl.BlockSpec((1, tm, d), lambda b, i: (b, i, 0))
    full = lambda shape: pl.BlockSpec(shape, lambda b, i: tuple(0 for _ in shape))
    return pl.pallas_call(
        functools.partial(_merge_kernel, ctx_row=ctx_row),
        grid=(bsz, t // tm),
        in_specs=[tile(0), tile(0), tile(groups["og"]), tok, tile(groups["ly"]), tile(groups["ma"]),
                  tile(groups["mb"]), tok, full((8, 6 * d)), full((1, HEAD_W)),
                  full((d, d)), full((d, d)), full((d, d)), full((2, d)), full((2, d)),
                  full((e, d))],
        out_specs=[tok, tok, pl.BlockSpec((1, e, tm), lambda b, i: (b, 0, i))],
        out_shape=[jax.ShapeDtypeStruct((bsz, t, d), F32), jax.ShapeDtypeStruct((bsz, t, d), BF16),
                   jax.ShapeDtypeStruct((bsz, e, t), F32)],
        compiler_params=_params("arbitrary", "arbitrary"),
        name=name,
    )(o_f, o_b, p, h_lru, p, p, p, x, mod_l, gn.reshape(1, HEAD_W), wa, wb, wo, ln_g, ln_b,
      w_router.T)


def _route_kernel(a_ref, posm_ref, off_ref, span_ref, *, cap, t, tt, grt):
    e = N_EXPERTS
    u = lax.bitcast_convert_type(a_ref[0], I32)
    thr = jnp.zeros((e, 1), I32)
    for bit in range(30, -1, -1):
        cand = thr | (1 << bit)
        cnt = jnp.sum((u >= cand).astype(F32), axis=1, keepdims=True)
        thr = jnp.where(cnt >= cap, cand, thr)
    gt = u > thr
    eq = u == thr
    need = cap - jnp.sum(gt.astype(F32), axis=1, keepdims=True)
    ri = lax.broadcasted_iota(I32, (tt, tt), 0)
    ci = lax.broadcasted_iota(I32, (tt, tt), 1)
    before = (ri < ci).astype(BF16)
    lane = lax.broadcasted_iota(I32, (e, OFF_STRIDE), 1)
    offs = jnp.zeros((e, OFF_STRIDE), F32)
    offs_end = jnp.zeros((e, OFF_STRIDE), F32)
    n_eq = jnp.zeros((e, 1), F32)
    n_sel = jnp.zeros((e, 1), F32)
    for j in range(t // tt):
        sl = slice(j * tt, (j + 1) * tt)
        eq_j = eq[:, sl]
        rank_eq = jnp.dot(eq_j.astype(BF16), before, preferred_element_type=F32) + n_eq
        sel_j = gt[:, sl] | (eq_j & (rank_eq < need))
        sel_f = sel_j.astype(F32)
        pos = jnp.dot(sel_j.astype(BF16), before, preferred_element_type=F32) + n_sel
        posm_ref[0, :, sl] = jnp.where(sel_j, pos.astype(I32), -1)
        offs = jnp.where(lane == j, n_sel, offs)
        n_eq = n_eq + jnp.sum(eq_j.astype(F32), axis=1, keepdims=True)
        n_sel = n_sel + jnp.sum(sel_f, axis=1, keepdims=True)
        offs_end = jnp.where(lane == j, n_sel, offs_end)
    nt = t // tt
    offs = jnp.where(lane == nt, n_sel, offs)
    off_ref[0] = offs.astype(I32)
    n_rt = cap // grt
    tile = lane < nt
    spans = jnp.zeros((e, OFF_STRIDE), F32)
    for i in range(n_rt):
        skipped = jnp.sum(jnp.where(tile & (offs_end <= i * grt), 1.0, 0.0), axis=1, keepdims=True)
        started = jnp.sum(jnp.where(tile & (offs < (i + 1) * grt), 1.0, 0.0), axis=1, keepdims=True)
        spans = jnp.where(lane == i, skipped, jnp.where(lane == n_rt + i, started, spans))
    span_ref[0] = spans.astype(I32)


def _route_call(aff_t, cap, tt, grt, name):
    bsz, e, t = aff_t.shape
    small = pl.BlockSpec((1, e, OFF_STRIDE), lambda b: (b, 0, 0))
    return pl.pallas_call(
        functools.partial(_route_kernel, cap=cap, t=t, tt=tt, grt=grt),
        grid=(bsz,),
        in_specs=[pl.BlockSpec((1, e, t), lambda b: (b, 0, 0))],
        out_specs=[pl.BlockSpec((1, e, t), lambda b: (b, 0, 0)), small, small],
        out_shape=[jax.ShapeDtypeStruct((bsz, e, t), I32), jax.ShapeDtypeStruct((bsz, e, OFF_STRIDE), I32),
                   jax.ShapeDtypeStruct((bsz, e, OFF_STRIDE), I32)],
        compiler_params=_params("arbitrary"),
        name=name,
    )(aff_t)


def _expert_kernel(span_ref, posm_ref, h_ref, wg_ref, wu_ref, wd_ref, y_ref, x_ref, *, cap, rt, tt, nt, nb):
    if nb == 1:
        b0, e = pl.program_id(0), pl.program_id(1)
    else:
        b0, e = 0, pl.program_id(0)
    x_ref[...] = jnp.zeros_like(x_ref)
    kt = min(nt, GATHER_WINDOW_TILES)
    kw = kt * tt
    rows = lax.broadcasted_iota(I32, (rt, kw), 0)
    lane = lax.broadcasted_iota(I32, (rt, kw), 1)
    n_rt = cap // rt

    for bb in range(nb):
        base = ((b0 + bb) * N_EXPERTS + e) * OFF_STRIDE
        for i in range(n_rt):
            first = span_ref[base + i]
            n_win = (span_ref[base + n_rt + i] - 1 - first) // kt + 1
            out_rows = slice((bb * n_rt + i) * rt, (bb * n_rt + i + 1) * rt)

            def body(wdx, carry, bb=bb, i=i, first=first, out_rows=out_rows):
                want = first + wdx * kt
                start = jnp.minimum(want, nt - kt)
                tok = pl.ds(pl.multiple_of(start * tt, tt), kw)
                hit = ((posm_ref[bb, 0, :, tok] - i * rt) == rows) & (lane >= (want - start) * tt)
                x_ref[out_rows, :] += jnp.dot(hit.astype(BF16), h_ref[bb, tok, :],
                                              preferred_element_type=F32).astype(BF16)
                return carry
            lax.fori_loop(0, n_win, body, 0)
    wg, wu, wd = wg_ref[0, 0].astype(BF16), wu_ref[0, 0].astype(BF16), wd_ref[0, 0].astype(BF16)
    total = nb * cap
    ft = min(total, 256)
    for r in range(total // ft):
        xb = x_ref[r * ft:(r + 1) * ft, :]
        g = jnp.dot(xb, wg, preferred_element_type=F32)
        u = jnp.dot(xb, wu, preferred_element_type=F32)
        hid = (_silu(g) * u).astype(BF16)
        out = jnp.dot(hid, wd, preferred_element_type=F32).astype(BF16)
        if ft <= cap:
            row0 = (r * ft) % cap
            y_ref[(r * ft) // cap, 0, row0:row0 + ft, :] = out
        else:
            for q in range(ft // cap):
                y_ref[r * (ft // cap) + q, 0, :, :] = out[q * cap:(q + 1) * cap, :]


def _expert_call(span_flat, posm, h2, w_gate, w_up, w_down, layer, cap, rt, tt, name):
    bsz, t, d = h2.shape
    _, e, _, f = w_gate.shape
    nt = t // tt
    if t < d:
        nb, grid = bsz, (e,)
        bk = lambda k, off: (0, k)
        tok_spec = pl.BlockSpec((bsz, t, d), lambda k, off: (0, 0, 0))
    else:
        nb, grid = 1, (bsz, e)
        bk = lambda b, k, off: (b, k)
        tok_spec = pl.BlockSpec((1, t, d), lambda b, k, off: (b, 0, 0), pipeline_mode=pl.Buffered(1))
    weight = lambda r, c: pl.BlockSpec((1, 1, r, c), lambda *i: (layer, bk(*i)[1], 0, 0))
    grid_spec = pltpu.PrefetchScalarGridSpec(
        num_scalar_prefetch=1,
        grid=grid,
        in_specs=[
            pl.BlockSpec((nb, 1, 1, t), lambda *i: (*bk(*i), 0, 0)),
            tok_spec, weight(d, f), weight(d, f), weight(f, d),
        ],
        out_specs=pl.BlockSpec((nb, 1, cap, d), lambda *i: (*bk(*i), 0, 0)),
        scratch_shapes=[pltpu.VMEM((nb * cap, d), BF16)],
    )
    return pl.pallas_call(
        functools.partial(_expert_kernel, cap=cap, rt=rt, tt=tt, nt=nt, nb=nb),
        grid_spec=grid_spec,
        out_shape=jax.ShapeDtypeStruct((bsz, e, cap, d), BF16),
        compiler_params=_params(*(("arbitrary",) * len(grid))),
        name=name,
    )(span_flat, posm.reshape(bsz, e, 1, t), h2, w_gate, w_up, w_down)


def _combine_kernel(off_ref, posm_ref, afft_ref, x1_ref, y_hbm, mod_ref, lng_ref, lnb_ref, o_ref,
                    ybuf, pbuf, sem, acc_ref, *, rt, tt, nw, ctx_row, n_tiles):
    d = D_MODEL
    b = pl.program_id(0)
    j = pl.program_id(1)
    row = b if ctx_row is None else ctx_row
    shift = rt.bit_length() - 1
    step = b * n_tiles + j
    cur = lax.rem(step, 2)
    is_last = step == pl.num_programs(0) * n_tiles - 1

    def plan(bb, jj):
        out, n_used = [], jnp.int32(0)
        for e in range(N_EXPERTS):
            base = (bb * N_EXPERTS + e) * OFF_STRIDE
            lo = off_ref[base + jj]
            hi = off_ref[base + jj + 1]
            t0 = lax.shift_right_logical(lo, shift)
            n = jnp.where(hi > lo, lax.shift_right_logical(hi - 1, shift) - t0 + 1, 0)
            out.append((t0, n, n_used))
            n_used = n_used + n
        return out, n_used

    def for_used_windows(windows, fn):
        for e, (t0, n, s0) in enumerate(windows):
            def body(wdx, carry, e=e, t0=t0, s0=s0):
                fn(e, t0 + wdx, s0 + wdx)
                return carry
            lax.fori_loop(0, n, body, 0)

    def window_copy(bb, e, window, slot, half):
        src = y_hbm.at[bb, e, pl.ds(pl.multiple_of(window * rt, rt), rt), :]
        dst = ybuf.at[half, pl.ds(pl.multiple_of(slot * rt, rt), rt), :]
        return pltpu.make_async_copy(src, dst, sem.at[half])

    def fetch(bb, windows, half):
        for_used_windows(windows, lambda e, window, slot: window_copy(bb, e, window, slot, half).start())

    windows, n_used = plan(b, j)
    n_groups = lax.shift_right_logical(n_used + (COMBINE_GROUP - 1), COMBINE_GROUP.bit_length() - 1)

    @pl.when(step == 0)
    def _():
        ybuf[...] = jnp.zeros_like(ybuf)
        fetch(b, windows, 0)

    @pl.when(jnp.logical_not(is_last))
    def _():
        wrap = j == n_tiles - 1
        b_next = jnp.where(wrap, b + 1, b)
        fetch(b_next, plan(b_next, jnp.where(wrap, 0, j + 1))[0], 1 - cur)

    def clear(s, carry):
        pbuf[pl.ds(pl.multiple_of(s * rt, rt), rt), :] = jnp.zeros((rt, tt), BF16)
        return carry
    lax.fori_loop(n_used, n_groups * COMBINE_GROUP, clear, 0)

    rows = lax.broadcasted_iota(I32, (rt, tt), 0)

    def weights(e, window, slot):
        pos = posm_ref[0, e]
        gate = afft_ref[0, e:e + 1, :]
        pbuf[pl.ds(pl.multiple_of(slot * rt, rt), rt), :] = jnp.where(
            (pos - window * rt) == rows, gate, 0.0).astype(BF16)
    for_used_windows(windows, weights)

    def drain(i, carry):
        window_copy(0, 0, 0, 0, cur).wait()
        return carry
    lax.fori_loop(0, n_used, drain, 0)

    acc_ref[...] = jnp.zeros_like(acc_ref)
    gk = COMBINE_GROUP * rt
    for g in range(pl.cdiv(N_EXPERTS * nw, COMBINE_GROUP)):
        @pl.when(g < n_groups)
        def _(g=g):
            acc_ref[...] += lax.dot_general(pbuf[g * gk:(g + 1) * gk, :], ybuf[cur, g * gk:(g + 1) * gk, :],
                                            (((0,), (0,)), ((), ())), preferred_element_type=F32)
    g2 = mod_ref[pl.ds(row, 1), 5 * d:6 * d]
    o_ref[0] = _layer_norm(ALPHA * x1_ref[0] + g2 * acc_ref[...], lng_ref[1:2, :], lnb_ref[1:2, :])


def _combine_call(off_flat, posm, aff_t, x1, y, mod_l, ln_g, ln_b, rt, tt, ctx_row, name):
    bsz, t, d = x1.shape
    e = N_EXPERTS
    cap = y.shape[2]
    nw = min(cap // rt, (tt - 1) // rt + 2)
    n_slots = pl.cdiv(e * nw, COMBINE_GROUP) * COMBINE_GROUP
    grid_spec = pltpu.PrefetchScalarGridSpec(
        num_scalar_prefetch=1,
        grid=(bsz, t // tt),
        in_specs=[
            pl.BlockSpec((1, e, 1, tt), lambda b, j, off: (b, 0, 0, j)),
            pl.BlockSpec((1, e, tt), lambda b, j, off: (b, 0, j)),
            pl.BlockSpec((1, tt, d), lambda b, j, off: (b, j, 0)),
            pl.BlockSpec(memory_space=pl.ANY),
            pl.BlockSpec((8, 6 * d), lambda b, j, off: (0, 0)),
            pl.BlockSpec((2, d), lambda b, j, off: (0, 0)),
            pl.BlockSpec((2, d), lambda b, j, off: (0, 0)),
        ],
        out_specs=pl.BlockSpec((1, tt, d), lambda b, j, off: (b, j, 0)),
        scratch_shapes=[pltpu.VMEM((2, n_slots * rt, d), BF16), pltpu.VMEM((n_slots * rt, tt), BF16),
                        pltpu.SemaphoreType.DMA((2,)), pltpu.VMEM((tt, d), F32)],
    )
    return pl.pallas_call(
        functools.partial(_combine_kernel, rt=rt, tt=tt, nw=nw, ctx_row=ctx_row, n_tiles=t // tt),
        grid_spec=grid_spec,
        out_shape=jax.ShapeDtypeStruct((bsz, t, d), F32),
        compiler_params=_params("arbitrary", "arbitrary"),
        name=name,
    )(off_flat, posm.reshape(bsz, e, 1, t), aff_t, x1, y, mod_l, ln_g, ln_b)


def _moe(x1, h2, aff_t, mod_l, w_gate, w_up, w_down, layer, ln_g, ln_b, ctx_row, tag):
    bsz, t, d = x1.shape
    cap = EC_FACTOR * t // N_EXPERTS
    tt = min(t, 256)
    grt = min(cap, GATHER_ROWS)
    posm, off, span = _route_call(aff_t, cap, tt, grt, "route_" + tag)
    y = _expert_call(span.reshape(-1), posm, h2, w_gate, w_up, w_down, layer, cap, grt, tt, "expert_" + tag)
    return _combine_call(off.reshape(-1), posm, aff_t, x1, y, mod_l, ln_g, ln_b, min(cap, SCATTER_ROWS), tt, ctx_row,
                         "combine_" + tag)


def kernel(x, c, ctx, c_ctx, w_mod, b_mod, w_in, hgrn_lb_logits, hgrn_norm_g, conv_w, conv_b, lru_wa,
           lru_ba, lru_wx, lru_bx, lru_lambda, w_branch_a, w_branch_b, w_out, ln_g, ln_b, w_router,
           w_gate, w_up, w_down):
    depth = w_in.shape[0]
    bsz = x.shape[0]
    ctx_row = bsz
    assert bsz < 8 and depth == DEPTH

    lb_cum = jnp.cumsum(jax.nn.softmax(hgrn_lb_logits.astype(F32), axis=0), axis=0)
    lbs = lb_cum - lb_cum[0]
    log1m_lb = jnp.log1p(-lbs)

    c_all = jnp.zeros((8, x.shape[2]), F32).at[:bsz].set(c).at[ctx_row].set(c_ctx)
    mod = _mod_call(c_all, w_mod, b_mod)

    p_index = P_INDEX
    for l in range(depth):
        need_ctx = l < depth - 1
        mod_l = mod[l]
        w_in_bf = w_in[l].astype(BF16)
        p_lat, lf_lat, kk_lat = _proj_call(x, mod_l, w_in_bf, lbs[l], log1m_lb[l], None, "proj_lat")
        p_ctx, lf_ctx, kk_ctx = _proj_call(ctx, mod_l, w_in_bf, lbs[l], log1m_lb[l], ctx_row,
                                           "proj_ctx")

        oc_f, oc_b, ol_f, ol_b = _hgrn_call(p_lat, lf_lat, kk_lat, p_ctx, lf_ctx, kk_ctx)
        h_lat, h_ctx = _lru_call(p_lat, p_ctx, p_index["lx"], p_index["lx"], conv_w[l], conv_b[l],
                                 lru_wa[l], lru_wx[l], lru_ba[l], lru_bx[l], lru_lambda[l])

        wa, wb, wo = (w.astype(BF16) for w in (w_branch_a[l], w_branch_b[l], w_out[l]))
        x1, h2, aff_t = _merge_call(ol_f, ol_b, p_lat, p_index, h_lat, x, mod_l, hgrn_norm_g[l],
                                         wa, wb, wo, ln_g[l], ln_b[l], w_router[l], None, "merge_lat")
        if need_ctx:
            c1, ch2, caff_t = _merge_call(oc_f, oc_b, p_ctx, p_index, h_ctx, ctx, mod_l, hgrn_norm_g[l],
                                                wa, wb, wo, ln_g[l], ln_b[l], w_router[l], ctx_row,
                                                "merge_ctx")
        x = _moe(x1, h2, aff_t, mod_l, w_gate, w_up, w_down, l, ln_g[l], ln_b[l], None, "lat")
        if need_ctx:
            ctx = _moe(c1, ch2, caff_t, mod_l, w_gate, w_up, w_down, l, ln_g[l], ln_b[l], ctx_row, "ctx")
    return x
```
